```python
import jax, jax.numpy as jnp
from jax import lax
import numpy as np

D_MODEL = 1024
BATCH = 8
SEQ = 4096
DEPTH = 1

GRID_W = 64
CTX_LEN = 256
HEAD_DIM = 64
N_Q_HEADS = 8
N_KV_HEADS = 2
Q_PER_KV = N_Q_HEADS // N_KV_HEADS
ATTN_DIM = N_Q_HEADS * HEAD_DIM
KV_DIM = N_KV_HEADS * HEAD_DIM
WINDOW = 128
ATTN_BLOCK = 128
ATTN_SCALE = HEAD_DIM ** -0.5
ROPE_THETA = 10000.0
POOL_WINDOWS = (2, 4, 8, 16)
N_POOL_GROUPS = len(POOL_WINDOWS)
POOL_DIM = D_MODEL // 2
POOL_GROUP_DIM = POOL_DIM // N_POOL_GROUPS
Q_OFF = 0
K_OFF = Q_OFF + ATTN_DIM
V_OFF = K_OFF + KV_DIM
P_OFF = V_OFF + KV_DIM
GA_OFF = P_OFF + POOL_DIM
GP_OFF = GA_OFF + D_MODEL
IN_DIM = GP_OFF + D_MODEL
N_EXPERTS = 256
TOP_K = 8
N_EXPERT_GROUPS = 8
TOPK_GROUPS = 4
EXPERT_DIM = 256
SHARED_DIM = 256
ROUTED_SCALE = 2.5
EXPERT_BLOCK = 128
N_ADA = 6
NORM_EPS = 1e-6
NEG_INF = -1e30

kernel_name = "hybrid_gated_window_gqa_pool_moe_dit_block"


def _rmsnorm(x, g):
    x32 = x.astype(jnp.float32)
    y = x32 * lax.rsqrt(jnp.mean(x32 * x32, axis=-1, keepdims=True) + NORM_EPS)
    return y.astype(x.dtype) * g


def _axial_rope(seq_len):
    rows = seq_len // GRID_W
    row = jnp.repeat(jnp.arange(rows), GRID_W).astype(jnp.float32)
    col = jnp.tile(jnp.arange(GRID_W), rows).astype(jnp.float32)
    n_freq = HEAD_DIM // 4
    inv = ROPE_THETA ** (-jnp.arange(n_freq, dtype=jnp.float32) / n_freq)
    ang = jnp.stack([row[:, None] * inv, col[:, None] * inv], axis=1)
    return jnp.cos(ang), jnp.sin(ang)


def _apply_rope(x, cos, sin):
    shp = x.shape
    nf = HEAD_DIM // 4
    xr = x.reshape(shp[:-1] + (2, 2, nf))
    bshape = (shp[1],) + (1,) * (x.ndim - 3) + (2, nf)
    c = cos.reshape(bshape).astype(x.dtype)
    s = sin.reshape(bshape).astype(x.dtype)
    a, b = xr[..., 0, :], xr[..., 1, :]
    return jnp.stack([a * c - b * s, b * c + a * s], axis=-2).reshape(shp)


def _window_attention(q, k, v, kc, vc, sink):
    B, S = q.shape[:2]
    C = kc.shape[1]
    nb = S // ATTN_BLOCK
    n_win = 3 * ATTN_BLOCK
    qb = q.reshape(B, nb, ATTN_BLOCK, N_KV_HEADS, Q_PER_KV, HEAD_DIM)

    def band(t):
        tp = jnp.pad(t, ((0, 0), (ATTN_BLOCK, ATTN_BLOCK), (0, 0), (0, 0)))
        tp = tp.reshape(B, nb + 2, ATTN_BLOCK, N_KV_HEADS, HEAD_DIM)
        return jnp.concatenate([tp[:, :-2], tp[:, 1:-1], tp[:, 2:]], axis=2)

    kw, vw = band(k), band(v)
    s_loc = jnp.einsum('bnqkgd,bnjkd->bnkgqj', qb, kw, preferred_element_type=jnp.float32) * ATTN_SCALE
    blk = jnp.arange(nb)[:, None, None]
    qpos = blk * ATTN_BLOCK + jnp.arange(ATTN_BLOCK)[None, :, None]
    kpos = (blk - 1) * ATTN_BLOCK + jnp.arange(n_win)[None, None, :]
    valid = (jnp.abs(qpos - kpos) <= WINDOW) & (kpos >= 0) & (kpos < S)
    s_loc = jnp.where(valid[None, :, None, None], s_loc, NEG_INF)
    s_ctx = jnp.einsum('bnqkgd,bckd->bnkgqc', qb, kc, preferred_element_type=jnp.float32) * ATTN_SCALE
    s_sink = jnp.broadcast_to(sink.astype(jnp.float32)[None, None, :, :, None, None], s_ctx.shape[:-1] + (1,))
    p = jax.nn.softmax(jnp.concatenate([s_loc, s_ctx, s_sink], axis=-1), axis=-1).astype(q.dtype)
    o = (jnp.einsum('bnkgqj,bnjkd->bnqkgd', p[..., :n_win], vw)
         + jnp.einsum('bnkgqc,bckd->bnqkgd', p[..., n_win:n_win + C], vc))
    return o.reshape(B, S, ATTN_DIM)


def _context_attention(qc, kc, vc, sink):
    B, C = qc.shape[:2]
    s = jnp.einsum('bckgd,bjkd->bkgcj', qc, kc, preferred_element_type=jnp.float32) * ATTN_SCALE
    s_sink = jnp.broadcast_to(sink.astype(jnp.float32)[None, :, :, None, None], s.shape[:-1] + (1,))
    p = jax.nn.softmax(jnp.concatenate([s, s_sink], axis=-1), axis=-1)[..., :-1].astype(vc.dtype)
    o = jnp.einsum('bkgcj,bjkd->bckgd', p, vc)
    return o.reshape(B, C, ATTN_DIM)


def _pool_mixer(u, w_pool, pool_scale):
    B, L, _ = u.shape
    u32 = u.astype(jnp.float32)
    cs = jnp.concatenate([jnp.zeros((B, 1, POOL_DIM), jnp.float32), jnp.cumsum(u32, axis=1)], axis=1)
    t = jnp.arange(L)
    groups = []
    for g, w in enumerate(POOL_WINDOWS):
        lo = jnp.clip(t - w // 2, 0, L)
        hi = jnp.clip(t + w // 2, 0, L)
        sl = slice(g * POOL_GROUP_DIM, (g + 1) * POOL_GROUP_DIM)
        csg = cs[:, :, sl]
        mean = (jnp.take(csg, hi, axis=1) - jnp.take(csg, lo, axis=1)) / (hi - lo).astype(jnp.float32)[None, :, None]
        groups.append(mean - u32[:, :, sl])
    d = jnp.stack(groups, axis=2).astype(u.dtype)
    y = jnp.einsum('blgc,gcd->blgd', d, w_pool).reshape(B, L, POOL_DIM)
    return y * pool_scale


def _merge_branches(attn, pool, gate_cols, w_up_attn, w_up_pool, w_out):
    ga, gp = gate_cols[..., :D_MODEL], gate_cols[..., D_MODEL:]
    y = jax.nn.sigmoid(ga) * (attn @ w_up_attn) + jax.nn.sigmoid(gp) * (pool @ w_up_pool)
    return y @ w_out


def _swiglu(x, wg, wu, wd):
    return (jax.nn.silu(x @ wg) * (x @ wu)) @ wd


def _route(h, w_router, router_bias):
    T = h.shape[0]
    s = jax.nn.sigmoid(jnp.einsum('td,de->te', h, w_router, preferred_element_type=jnp.float32))
    sel = s + router_bias.astype(jnp.float32)
    sel_g = sel.reshape(T, N_EXPERT_GROUPS, N_EXPERTS // N_EXPERT_GROUPS)
    group_score = lax.top_k(sel_g, 2)[0].sum(axis=-1)
    _, gidx = lax.top_k(group_score, TOPK_GROUPS)
    gmask = jax.nn.one_hot(gidx, N_EXPERT_GROUPS, dtype=jnp.float32).sum(axis=-2) > 0
    emask = jnp.repeat(gmask, N_EXPERTS // N_EXPERT_GROUPS, axis=-1)
    _, eidx = lax.top_k(jnp.where(emask, sel, NEG_INF), TOP_K)
    w = jnp.take_along_axis(s, eidx, axis=-1)
    w = w / jnp.sum(w, axis=-1, keepdims=True) * ROUTED_SCALE
    return eidx, w


def _moe(h, w_router, router_bias, w_exp_gate, w_exp_up, w_exp_down, w_sh_gate, w_sh_up, w_sh_down):
    T, D = h.shape
    eidx, w = _route(h, w_router, router_bias)
    A = T * TOP_K
    flat_e = eidx.reshape(A)
    flat_tok = jnp.repeat(jnp.arange(T, dtype=jnp.int32), TOP_K)
    flat_w = w.reshape(A)
    order = jnp.argsort(flat_e)
    sorted_e = flat_e[order]
    counts = jnp.bincount(flat_e, length=N_EXPERTS)
    starts = jnp.cumsum(counts) - counts
    padded = (counts + EXPERT_BLOCK - 1) // EXPERT_BLOCK * EXPERT_BLOCK
    pends = jnp.cumsum(padded)
    pstarts = pends - padded
    dest = pstarts[sorted_e] + jnp.arange(A) - starts[sorted_e]
    n_blocks = (A + N_EXPERTS * (EXPERT_BLOCK - 1) + EXPERT_BLOCK - 1) // EXPERT_BLOCK
    n_slots = n_blocks * EXPERT_BLOCK
    slot_tok = jnp.full((n_slots,), T, dtype=jnp.int32).at[dest].set(flat_tok[order])
    slot_w = jnp.zeros((n_slots,), h.dtype).at[dest].set(flat_w[order].astype(h.dtype))
    block_e = jnp.minimum(jnp.searchsorted(pends, jnp.arange(n_blocks) * EXPERT_BLOCK, side='right'), N_EXPERTS - 1)
    h_pad = jnp.concatenate([h, jnp.zeros((1, D), h.dtype)], axis=0)

    def run_block(args):
        tok, wt, e = args
        return _swiglu(h_pad[tok], w_exp_gate[e], w_exp_up[e], w_exp_down[e]) * wt[:, None]

    y = lax.map(run_block, (slot_tok.reshape(n_blocks, EXPERT_BLOCK),
                            slot_w.reshape(n_blocks, EXPERT_BLOCK), block_e))
    routed = jax.ops.segment_sum(y.reshape(n_slots, D), slot_tok, num_segments=T + 1)[:T]
    return routed + _swiglu(h, w_sh_gate, w_sh_up, w_sh_down)


def setup_inputs(seed: int = 0) -> dict:
    key = jax.random.key(seed)
    ks = jax.random.split(key, 24)
    D, E, F = D_MODEL, N_EXPERTS, EXPERT_DIM
    nrm = jax.random.normal
    return {
        "x": nrm(ks[0], (BATCH, SEQ, D), jnp.float32),
        "c": nrm(ks[1], (BATCH, D), jnp.float32),
        "ctx": nrm(ks[2], (BATCH, CTX_LEN, D), jnp.float32),
        "c_ctx": nrm(ks[3], (D,), jnp.float32),
        "w_ada": nrm(ks[4], (DEPTH, D, N_ADA * D), jnp.float32) * (0.5 * D ** -0.5),
        "b_ada": nrm(ks[5], (DEPTH, N_ADA * D), jnp.float32) * 0.02,
        "norm1_g": 1.0 + 0.02 * nrm(ks[6], (DEPTH, D), jnp.float32),
        "w_in": nrm(ks[7], (DEPTH, D, IN_DIM), jnp.float32) * D ** -0.5,
        "attn_sink": nrm(ks[8], (DEPTH, N_Q_HEADS), jnp.float32) * 0.5,
        "w_pool": nrm(ks[9], (DEPTH, N_POOL_GROUPS, POOL_GROUP_DIM, POOL_GROUP_DIM), jnp.float32) * POOL_GROUP_DIM ** -0.5,
        "pool_scale": 1.0 + 0.02 * nrm(ks[10], (DEPTH, POOL_DIM), jnp.float32),
        "w_up_attn": nrm(ks[11], (DEPTH, ATTN_DIM, D), jnp.float32) * ATTN_DIM ** -0.5,
        "w_up_pool": nrm(ks[12], (DEPTH, POOL_DIM, D), jnp.float32) * POOL_DIM ** -0.5,
        "w_out": nrm(ks[13], (DEPTH, D, D), jnp.float32) * D ** -0.5,
        "norm2_g": 1.0 + 0.02 * nrm(ks[14], (DEPTH, D), jnp.float32),
        "w_router": nrm(ks[15], (DEPTH, D, E), jnp.float32) * D ** -0.5,
        "router_bias": nrm(ks[16], (DEPTH, E), jnp.float32) * 0.01,
        "w_exp_gate": nrm(ks[17], (DEPTH, E, D, F), jnp.float32) * D ** -0.5,
        "w_exp_up": nrm(ks[18], (DEPTH, E, D, F), jnp.float32) * D ** -0.5,
        "w_exp_down": nrm(ks[19], (DEPTH, E, F, D), jnp.float32) * F ** -0.5,
        "w_sh_gate": nrm(ks[20], (DEPTH, D, SHARED_DIM), jnp.float32) * D ** -0.5,
        "w_sh_up": nrm(ks[21], (DEPTH, D, SHARED_DIM), jnp.float32) * D ** -0.5,
        "w_sh_down": nrm(ks[22], (DEPTH, SHARED_DIM, D), jnp.float32) * SHARED_DIM ** -0.5,
        "final_g": 1.0 + 0.02 * nrm(ks[23], (D,), jnp.float32),
    }


def reference(x, c, ctx, c_ctx, w_ada, b_ada, norm1_g, w_in, attn_sink, w_pool, pool_scale,
              w_up_attn, w_up_pool, w_out, norm2_g, w_router, router_bias,
              w_exp_gate, w_exp_up, w_exp_down, w_sh_gate, w_sh_up, w_sh_down, final_g):
    B, S, D = x.shape
    C = ctx.shape[1]
    cos, sin = _axial_rope(S)
    for l in range(DEPTH):
        last = l == DEPTH - 1
        mod = (jax.nn.silu(c) @ w_ada[l] + b_ada[l])[:, None, :]
        mod_c = jax.nn.silu(c_ctx) @ w_ada[l] + b_ada[l]
        sh1, sc1, g1, sh2, sc2, g2 = jnp.split(mod, N_ADA, axis=-1)
        sh1c, sc1c, g1c, sh2c, sc2c, g2c = jnp.split(mod_c, N_ADA, axis=-1)

        h = _rmsnorm(x, norm1_g[l]) * (1 + sc1) + sh1
        hc = _rmsnorm(ctx, norm1_g[l]) * (1 + sc1c) + sh1c
        proj = h @ w_in[l]
        q = _apply_rope(proj[..., Q_OFF:K_OFF].reshape(B, S, N_KV_HEADS, Q_PER_KV, HEAD_DIM), cos, sin)
        k = _apply_rope(proj[..., K_OFF:V_OFF].reshape(B, S, N_KV_HEADS, HEAD_DIM), cos, sin)
        v = proj[..., V_OFF:P_OFF].reshape(B, S, N_KV_HEADS, HEAD_DIM)
        kvc = hc @ w_in[l][:, K_OFF:P_OFF]
        kc = kvc[..., :KV_DIM].reshape(B, C, N_KV_HEADS, HEAD_DIM)
        vc = kvc[..., KV_DIM:].reshape(B, C, N_KV_HEADS, HEAD_DIM)
        sink = attn_sink[l].reshape(N_KV_HEADS, Q_PER_KV)
        attn = _window_attention(q, k, v, kc, vc, sink)
        pool = _pool_mixer(proj[..., P_OFF:GA_OFF], w_pool[l], pool_scale[l])
        mix = _merge_branches(attn, pool, proj[..., GA_OFF:IN_DIM], w_up_attn[l], w_up_pool[l], w_out[l])
        x = x + g1 * mix

        h2 = _rmsnorm(x, norm2_g[l]) * (1 + sc2) + sh2
        moe_args = (w_router[l], router_bias[l], w_exp_gate[l], w_exp_up[l], w_exp_down[l],
                    w_sh_gate[l], w_sh_up[l], w_sh_down[l])
        if last:
            f = _moe(h2.reshape(B * S, D), *moe_args).reshape(B, S, D)
        else:
            qc = (hc @ w_in[l][:, Q_OFF:K_OFF]).reshape(B, C, N_KV_HEADS, Q_PER_KV, HEAD_DIM)
            restc = hc @ w_in[l][:, P_OFF:IN_DIM]
            attn_c = _context_attention(qc, kc, vc, sink)
            pool_c = _pool_mixer(restc[..., :POOL_DIM], w_pool[l], pool_scale[l])
            mix_c = _merge_branches(attn_c, pool_c, restc[..., POOL_DIM:], w_up_attn[l], w_up_pool[l], w_out[l])
            ctx = ctx + g1c * mix_c
            h2c = _rmsnorm(ctx, norm2_g[l]) * (1 + sc2c) + sh2c
            fa = _moe(jnp.concatenate([h2.reshape(B * S, D), h2c.reshape(B * C, D)], axis=0), *moe_args)
            f = fa[:B * S].reshape(B, S, D)
            ctx = ctx + g2c * fa[B * S:].reshape(B, C, D)
        x = x + g2 * f
    return _rmsnorm(x, final_g)
```

```python
import functools

import jax
import jax.numpy as jnp
from jax import lax
from jax.experimental import pallas as pl
from jax.experimental.pallas import tpu as pltpu

F32 = jnp.float32
BF16 = jnp.bfloat16
I32 = jnp.int32

GRID_W = 64
HEAD_DIM = 64
N_Q_HEADS = 8
N_KV_HEADS = 2
Q_PER_KV = N_Q_HEADS // N_KV_HEADS
ATTN_DIM = N_Q_HEADS * HEAD_DIM
KV_DIM = N_KV_HEADS * HEAD_DIM
ATTN_BLOCK = 128
ATTN_SCALE = HEAD_DIM ** -0.5
ROPE_THETA = 10000.0
POOL_WINDOWS = (2, 4, 8, 16)
POOL_HALO = 8
N_EXPERT_GROUPS = 8
TOPK_GROUPS = 4
TOP_K = 8
ROUTED_SCALE = 2.5
N_ADA = 6
NORM_EPS = 1e-6
NEG_INF = -1e30
LANES = 128
SUBLANES = 8
TOKEN_TILE = 512
EXPERT_BLOCK_ROWS = 256
VMEM_LIMIT = 56 * 1024 * 1024


def _sigmoid(x):
    return 1.0 / (1.0 + jnp.exp(-x))


def _silu(x):
    return x * _sigmoid(x)


def _nt_dot(a, b):
    return lax.dot_general(a, b, (((1,), (1,)), ((), ())), preferred_element_type=F32)


def _dot(a, b):
    return jnp.dot(a, b, preferred_element_type=F32)


def _params(*sem):
    return pltpu.CompilerParams(dimension_semantics=sem, vmem_limit_bytes=VMEM_LIMIT)


def _ada_body(c_ref, w_ref, b_ref, o_ref):
    s = _silu(c_ref[...])
    o_ref[...] = jnp.dot(s, w_ref[...], preferred_element_type=F32,
                         precision=lax.Precision.HIGHEST) + b_ref[...]


def _ada(c_rows, w_ada, b_ada):
    rows, d = c_rows.shape
    n = w_ada.shape[1]
    bn = d
    return pl.pallas_call(
        _ada_body,
        grid=(n // bn,),
        in_specs=[pl.BlockSpec((rows, d), lambda j: (0, 0)),
                  pl.BlockSpec((d, bn), lambda j: (0, j)),
                  pl.BlockSpec((1, bn), lambda j: (0, j))],
        out_specs=pl.BlockSpec((rows, bn), lambda j: (0, j)),
        out_shape=jax.ShapeDtypeStruct((rows, n), F32),
        compiler_params=_params("arbitrary"),
        name="ada_mod",
    )(c_rows, w_ada, b_ada.reshape(1, n))


def _norm_mod(x, g, shift, scale):
    ms = jnp.mean(x * x, axis=-1, keepdims=True)
    y = x * lax.rsqrt(ms + NORM_EPS)
    return (y * g) * (1.0 + scale) + shift


def _rope(t, cos, sin_a, sin_b):
    return (t * cos + pltpu.roll(t, LANES - HEAD_DIM // 4, 1) * sin_a
            + pltpu.roll(t, HEAD_DIM // 4, 1) * sin_b)


def _inproj_body(x_ref, mod_ref, g_ref, w_ref, cos_ref, sa_ref, sb_ref,
                 q_ref, k_ref, v_ref, u_ref, gate_ref):
    h = _norm_mod(x_ref[...], g_ref[...], mod_ref[0:1, :], mod_ref[1:2, :]).astype(BF16)
    cos, sa, sb = cos_ref[...], sa_ref[...], sb_ref[...]
    heads_per_chunk = LANES // HEAD_DIM
    for j in range(ATTN_DIM // LANES):
        t = _rope(_dot(h, w_ref[:, j * LANES:(j + 1) * LANES]), cos, sa, sb) * ATTN_SCALE
        t = t.astype(BF16)
        for i in range(heads_per_chunk):
            q_ref[heads_per_chunk * j + i] = t[:, i * HEAD_DIM:(i + 1) * HEAD_DIM]
    k_off = ATTN_DIM
    t = _rope(_dot(h, w_ref[:, k_off:k_off + KV_DIM]), cos, sa, sb).astype(BF16)
    for i in range(N_KV_HEADS):
        k_ref[i] = t[:, i * HEAD_DIM:(i + 1) * HEAD_DIM]
    v_off = k_off + KV_DIM
    t = _dot(h, w_ref[:, v_off:v_off + KV_DIM]).astype(BF16)
    for i in range(N_KV_HEADS):
        v_ref[i] = t[:, i * HEAD_DIM:(i + 1) * HEAD_DIM]
    p_off = v_off + KV_DIM
    pool_dim = u_ref.shape[-1]
    u_ref[...] = _dot(h, w_ref[:, p_off:p_off + pool_dim])
    g_off = p_off + pool_dim
    gate_dim = gate_ref.shape[-1]
    chunk = 512
    for j in range(gate_dim // chunk):
        t = _dot(h, w_ref[:, g_off + j * chunk:g_off + (j + 1) * chunk])
        gate_ref[:, j * chunk:(j + 1) * chunk] = _sigmoid(t).astype(BF16)


def _inproj(x, mod3, g, w_in_bf, cos, sa, sb, pool_dim):
    b, s, d = x.shape
    tm = min(TOKEN_TILE, s)
    in_dim = w_in_bf.shape[1]
    gate_dim = in_dim - ATTN_DIM - 2 * KV_DIM - pool_dim
    grid = (b, s // tm)
    return pl.pallas_call(
        _inproj_body,
        grid=grid,
        in_specs=[
            pl.BlockSpec((None, tm, d), lambda bi, i: (bi, i, 0)),
            pl.BlockSpec((None, N_ADA, d), lambda bi, i: (bi, 0, 0)),
            pl.BlockSpec((1, d), lambda bi, i: (0, 0)),
            pl.BlockSpec((d, in_dim), lambda bi, i: (0, 0)),
            pl.BlockSpec((tm, LANES), lambda bi, i: (i, 0)),
            pl.BlockSpec((tm, LANES), lambda bi, i: (i, 0)),
            pl.BlockSpec((tm, LANES), lambda bi, i: (i, 0)),
        ],
        out_specs=[
            pl.BlockSpec((None, N_Q_HEADS, tm, HEAD_DIM), lambda bi, i: (bi, 0, i, 0)),
            pl.BlockSpec((None, N_KV_HEADS, tm, HEAD_DIM), lambda bi, i: (bi, 0, i, 0)),
            pl.BlockSpec((None, N_KV_HEADS, tm, HEAD_DIM), lambda bi, i: (bi, 0, i, 0)),
            pl.BlockSpec((None, tm, pool_dim), lambda bi, i: (bi, i, 0)),
            pl.BlockSpec((None, tm, gate_dim), lambda bi, i: (bi, i, 0)),
        ],
        out_shape=[
            jax.ShapeDtypeStruct((b, N_Q_HEADS, s, HEAD_DIM), BF16),
            jax.ShapeDtypeStruct((b, N_KV_HEADS, s, HEAD_DIM), BF16),
            jax.ShapeDtypeStruct((b, N_KV_HEADS, s, HEAD_DIM), BF16),
            jax.ShapeDtypeStruct((b, s, pool_dim), F32),
            jax.ShapeDtypeStruct((b, s, gate_dim), BF16),
        ],
        compiler_params=_params("parallel", "parallel"),
        name="in_proj",
    )(x, mod3, g, w_in_bf, cos, sa, sb)


def _ctxkv_body(ctx_ref, mod_ref, g_ref, w_ref, kc_ref, vc_ref):
    h = _norm_mod(ctx_ref[...], g_ref[...], mod_ref[0:1, :], mod_ref[1:2, :]).astype(BF16)
    t = _dot(h, w_ref[...]).astype(BF16)
    for i in range(N_KV_HEADS):
        kc_ref[i] = t[:, i * HEAD_DIM:(i + 1) * HEAD_DIM]
        vc_ref[i] = t[:, KV_DIM + i * HEAD_DIM:KV_DIM + (i + 1) * HEAD_DIM]


def _ctxkv(ctx, mod_c, g, w_kv_bf):
    b, c, d = ctx.shape
    out = jax.ShapeDtypeStruct((b, N_KV_HEADS, c, HEAD_DIM), BF16)
    spec = pl.BlockSpec((None, N_KV_HEADS, c, HEAD_DIM), lambda bi: (bi, 0, 0, 0))
    return pl.pallas_call(
        _ctxkv_body,
        grid=(b,),
        in_specs=[pl.BlockSpec((None, c, d), lambda bi: (bi, 0, 0)),
                  pl.BlockSpec((N_ADA, d), lambda bi: (0, 0)),
                  pl.BlockSpec((1, d), lambda bi: (0, 0)),
                  pl.BlockSpec((d, 2 * KV_DIM), lambda bi: (0, 0))],
        out_specs=[spec, spec],
        out_shape=[out, out],
        compiler_params=_params("parallel"),
        name="ctx_kv",
    )(ctx, mod_c, g, w_kv_bf)


def _attn_body(sink_ref, q_ref, kp_ref, kn_ref, kx_ref, vp_ref, vn_ref, vx_ref,
               kc_ref, vc_ref, lo_ref, hi_ref, o_ref):
    n = pl.program_id(1)
    last = pl.num_programs(1) - 1
    rows = Q_PER_KV * ATTN_BLOCK
    bias_lo = lo_ref[...] + jnp.where(n == 0, NEG_INF, 0.0)
    bias_hi = hi_ref[...] + jnp.where(n == last, NEG_INF, 0.0)
    row_head = lax.broadcasted_iota(I32, (rows, 1), 0) // ATTN_BLOCK
    outs = []
    for kk in range(N_KV_HEADS):
        qs = q_ref[kk * Q_PER_KV:(kk + 1) * Q_PER_KV].reshape(rows, HEAD_DIM)
        s0 = _nt_dot(qs, kp_ref[kk]) + bias_lo
        s1 = _nt_dot(qs, kn_ref[kk])
        s2 = _nt_dot(qs, kx_ref[kk]) + bias_hi
        sc = _nt_dot(qs, kc_ref[kk])
        sink = jnp.zeros((rows, 1), F32)
        for g in range(Q_PER_KV):
            sink = jnp.where(row_head == g, sink_ref[kk * Q_PER_KV + g], sink)
        m = jnp.maximum(
            jnp.maximum(jnp.max(s0, axis=1, keepdims=True), jnp.max(s1, axis=1, keepdims=True)),
            jnp.maximum(jnp.max(s2, axis=1, keepdims=True), jnp.max(sc, axis=1, keepdims=True)))
        m = jnp.maximum(m, sink)
        p0, p1, p2, pc = jnp.exp(s0 - m), jnp.exp(s1 - m), jnp.exp(s2 - m), jnp.exp(sc - m)
        denom = (jnp.sum(p0, axis=1, keepdims=True) + jnp.sum(p1, axis=1, keepdims=True)
                 + jnp.sum(p2, axis=1, keepdims=True) + jnp.sum(pc, axis=1, keepdims=True)
                 + jnp.exp(sink - m))
        o = (_dot(p0.astype(BF16), vp_ref[kk]) + _dot(p1.astype(BF16), vn_ref[kk])
             + _dot(p2.astype(BF16), vx_ref[kk]) + _dot(pc.astype(BF16), vc_ref[kk]))
        o = o / denom
        outs += [o[g * ATTN_BLOCK:(g + 1) * ATTN_BLOCK] for g in range(Q_PER_KV)]
    o_ref[...] = jnp.concatenate(outs, axis=1).astype(BF16)


def _attention(q, k, v, kc, vc, sink):
    b, _, s, _ = q.shape
    c = kc.shape[2]
    nb = s // ATTN_BLOCK
    rows = Q_PER_KV * ATTN_BLOCK
    qi = jnp.tile(jnp.arange(ATTN_BLOCK), Q_PER_KV)[:, None]
    kj = jnp.arange(ATTN_BLOCK)[None, :]
    bias_lo = jnp.where(kj >= qi, 0.0, NEG_INF).astype(F32)
    bias_hi = jnp.where(kj <= qi, 0.0, NEG_INF).astype(F32)
    kv_spec = lambda f: pl.BlockSpec((None, N_KV_HEADS, ATTN_BLOCK, HEAD_DIM), f)
    prev = lambda bi, n: (bi, 0, jnp.maximum(n - 1, 0), 0)
    own = lambda bi, n: (bi, 0, n, 0)
    nxt = lambda bi, n: (bi, 0, jnp.minimum(n + 1, nb - 1), 0)
    ctx_spec = pl.BlockSpec((None, N_KV_HEADS, c, HEAD_DIM), lambda bi, n: (bi, 0, 0, 0))
    bias_spec = pl.BlockSpec((rows, ATTN_BLOCK), lambda bi, n: (0, 0))
    return pl.pallas_call(
        _attn_body,
        grid=(b, nb),
        in_specs=[
            pl.BlockSpec(memory_space=pltpu.SMEM),
            pl.BlockSpec((None, N_Q_HEADS, ATTN_BLOCK, HEAD_DIM), own),
            kv_spec(prev), kv_spec(own), kv_spec(nxt),
            kv_spec(prev), kv_spec(own), kv_spec(nxt),
            ctx_spec, ctx_spec, bias_spec, bias_spec,
        ],
        out_specs=pl.BlockSpec((None, ATTN_BLOCK, ATTN_DIM), lambda bi, n: (bi, n, 0)),
        out_shape=jax.ShapeDtypeStruct((b, s, ATTN_DIM), BF16),
        compiler_params=_params("parallel", "parallel"),
        name="window_attn",
    )(sink, q, k, k, k, v, v, v, kc, vc, bias_lo, bias_hi)


def _pool_delta(ue, t_seq, seq_len):
    n_ext = ue.shape[0]
    tm = n_ext - 2 * POOL_HALO
    group_dim = ue.shape[1] // len(POOL_WINDOWS)
    outs = []
    for g, w in enumerate(POOL_WINDOWS):
        xs = ue[:, g * group_dim:(g + 1) * group_dim]
        acc = xs + pltpu.roll(xs, 1, 0)
        step = 1
        while 2 * step < w:
            acc = pltpu.roll(acc, step, 0) + pltpu.roll(acc, n_ext - step, 0)
            step *= 2
        half = w // 2
        cnt = (jnp.minimum(t_seq + half, seq_len) - jnp.maximum(t_seq - half, 0)).astype(F32)
        core = slice(POOL_HALO, POOL_HALO + tm)
        outs.append(acc[core] / cnt - xs[core])
    return outs


def _route(s, sel, n_tok):
    n_exp = s.shape[0]
    per_group = n_exp // N_EXPERT_GROUPS
    neg = float("-inf")
    iota_g = lax.broadcasted_iota(I32, (per_group, n_tok), 0).astype(F32)
    scores = []
    for g in range(N_EXPERT_GROUPS):
        blk = sel[g * per_group:(g + 1) * per_group]
        m1 = jnp.max(blk, axis=0, keepdims=True)
        first = jnp.min(jnp.where(blk == m1, iota_g, float(per_group)), axis=0, keepdims=True)
        m2 = jnp.max(jnp.where(iota_g == first, neg, blk), axis=0, keepdims=True)
        scores.append(m1 + m2)
    gs = jnp.concatenate(scores, axis=0)
    iota_ng = lax.broadcasted_iota(I32, (N_EXPERT_GROUPS, n_tok), 0).astype(F32)
    gsel = jnp.zeros((N_EXPERT_GROUPS, n_tok), F32)
    for _ in range(TOPK_GROUPS):
        m = jnp.max(gs, axis=0, keepdims=True)
        first = jnp.min(jnp.where(gs == m, iota_ng, float(N_EXPERT_GROUPS)), axis=0, keepdims=True)
        hit = iota_ng == first
        gsel = jnp.where(hit, 1.0, gsel)
        gs = jnp.where(hit, neg, gs)
    cur = jnp.concatenate(
        [jnp.where(gsel[g:g + 1] > 0.0, sel[g * per_group:(g + 1) * per_group], NEG_INF)
         for g in range(N_EXPERT_GROUPS)], axis=0)
    iota_e = lax.broadcasted_iota(I32, (n_exp, n_tok), 0).astype(F32)
    chosen = jnp.zeros((n_exp, n_tok), F32)
    ids, aff = [], []
    for _ in range(TOP_K):
        m = jnp.max(cur, axis=0, keepdims=True)
        first = jnp.min(jnp.where(cur == m, iota_e, float(n_exp)), axis=0, keepdims=True)
        hit = iota_e == first
        ids.append(first)
        aff.append(jnp.sum(jnp.where(hit, s, 0.0), axis=0, keepdims=True))
        chosen = jnp.where(hit, 1.0, chosen)
        cur = jnp.where(hit, neg, cur)
    return ids, aff, chosen


def _mixer_body(x_ref, attn_ref, up_ref, u_ref, un_ref, gate_ref, mod_ref, n2g_ref,
                wpool_ref, pscale_ref, wua_ref, wup_ref, wout_ref, wrh_ref, wrl_ref, rbias_ref,
                wsg_ref, wsu_ref, wsd_ref,
                base_ref, h2_ref, eidx_ref, wts_ref, rank_ref, cnt_ref, *, seq_len):
    bi, ti = pl.program_id(0), pl.program_id(1)
    n_t = pl.num_programs(1)
    tm, d = x_ref.shape
    pool_dim = u_ref.shape[-1]

    zero_halo = jnp.zeros((POOL_HALO, pool_dim), F32)
    u_prev = jnp.where(ti == 0, zero_halo, up_ref[...])
    u_next = jnp.where(ti == n_t - 1, zero_halo, un_ref[...])
    ue = jnp.concatenate([u_prev, u_ref[...], u_next], axis=0)
    t_seq = ti * tm + lax.broadcasted_iota(I32, (tm, 1), 0)
    deltas = _pool_delta(ue, t_seq, seq_len)
    pool = jnp.concatenate(
        [_dot(dl.astype(BF16), wpool_ref[g]) for g, dl in enumerate(deltas)], axis=1)
    pool = (pool * pscale_ref[...]).astype(BF16)

    sig_a = gate_ref[:, :d].astype(F32)
    sig_p = gate_ref[:, d:].astype(F32)
    y = sig_a * _dot(attn_ref[...], wua_ref[...]) + sig_p * _dot(pool, wup_ref[...])
    mix = _dot(y.astype(BF16), wout_ref[...])
    x1 = x_ref[...] + mod_ref[2:3, :] * mix

    h2 = _norm_mod(x1, n2g_ref[...], mod_ref[3:4, :], mod_ref[4:5, :])
    h2_hi = h2.astype(BF16)
    h2_ref[...] = h2_hi
    shared = _dot((_silu(_dot(h2_hi, wsg_ref[...])) * _dot(h2_hi, wsu_ref[...])).astype(BF16),
                  wsd_ref[...])
    base_ref[...] = x1 + mod_ref[5:6, :] * shared

    h2_lo = (h2 - h2_hi.astype(F32)).astype(BF16)
    logits = (_nt_dot(wrh_ref[...], h2_hi) + _nt_dot(wrh_ref[...], h2_lo)
              + _nt_dot(wrl_ref[...], h2_hi))
    s = _sigmoid(logits)
    sel = s + rbias_ref[...]
    ids, aff, chosen = _route(s, sel, tm)

    total = aff[0]
    for a in aff[1:]:
        total = total + a
    wts_ref[...] = jnp.concatenate([a / total * ROUTED_SCALE for a in aff], axis=0)
    eidx_ref[...] = jnp.concatenate(ids, axis=0).astype(I32)

    @pl.when((bi == 0) & (ti == 0))
    def _():
        cnt_ref[...] = jnp.zeros_like(cnt_ref)

    before = (lax.broadcasted_iota(I32, (tm, tm), 0) < lax.broadcasted_iota(I32, (tm, tm), 1))
    prefix = _dot(chosen.astype(BF16), jnp.where(before, 1.0, 0.0).astype(BF16))
    rank_dense = prefix + cnt_ref[:, 0:1]
    n_exp = s.shape[0]
    iota_e = lax.broadcasted_iota(I32, (n_exp, tm), 0).astype(F32)
    rank_ref[...] = jnp.concatenate(
        [jnp.sum(jnp.where(iota_e == i, rank_dense, 0.0), axis=0, keepdims=True) for i in ids],
        axis=0).astype(I32)
    cnt_ref[...] = cnt_ref[...] + jnp.sum(chosen, axis=1, keepdims=True)


def _mixer(x, attn, u, gates, mod3, n2g, wpool_bf, pscale, wua_bf, wup_bf, wout_bf,
           wr_hi, wr_lo, rbias, wsg_bf, wsu_bf, wsd_bf):
    b, s, d = x.shape
    tm = min(TOKEN_TILE, s)
    n_t = s // tm
    t_all = b * s
    pool_dim = u.shape[-1]
    n_exp = wr_hi.shape[0]
    sh = wsg_bf.shape[1]
    halo_blocks = tm // POOL_HALO
    n_halo = s // POOL_HALO
    full = lambda shape: pl.BlockSpec(shape, lambda bi, i: (0,) * len(shape))
    tok = lambda bi, i: (0, bi * n_t + i)
    return pl.pallas_call(
        functools.partial(_mixer_body, seq_len=s),
        grid=(b, n_t),
        in_specs=[
            pl.BlockSpec((None, tm, d), lambda bi, i: (bi, i, 0)),
            pl.BlockSpec((None, tm, ATTN_DIM), lambda bi, i: (bi, i, 0)),
            pl.BlockSpec((None, POOL_HALO, pool_dim),
                         lambda bi, i: (bi, jnp.maximum(i * halo_blocks - 1, 0), 0)),
            pl.BlockSpec((None, tm, pool_dim), lambda bi, i: (bi, i, 0)),
            pl.BlockSpec((None, POOL_HALO, pool_dim),
                         lambda bi, i: (bi, jnp.minimum((i + 1) * halo_blocks, n_halo - 1), 0)),
            pl.BlockSpec((None, tm, 2 * d), lambda bi, i: (bi, i, 0)),
            pl.BlockSpec((None, N_ADA, d), lambda bi, i: (bi, 0, 0)),
            full((1, d)),
            full((len(POOL_WINDOWS), pool_dim // len(POOL_WINDOWS), pool_dim // len(POOL_WINDOWS))),
            full((1, pool_dim)),
            full((ATTN_DIM, d)), full((pool_dim, d)), full((d, d)),
            full((n_exp, d)), full((n_exp, d)), full((n_exp, 1)),
            full((d, sh)), full((d, sh)), full((sh, d)),
        ],
        out_specs=[
            pl.BlockSpec((None, tm, d), lambda bi, i: (bi, i, 0)),
            pl.BlockSpec((None, tm, d), lambda bi, i: (bi, i, 0)),
            pl.BlockSpec((TOP_K, tm), tok),
            pl.BlockSpec((TOP_K, tm), tok),
            pl.BlockSpec((TOP_K, tm), tok),
            pl.BlockSpec((n_exp, LANES), lambda bi, i: (0, 0)),
        ],
        out_shape=[
            jax.ShapeDtypeStruct((b, s, d), F32),
            jax.ShapeDtypeStruct((b, s, d), BF16),
            jax.ShapeDtypeStruct((TOP_K, t_all), I32),
            jax.ShapeDtypeStruct((TOP_K, t_all), F32),
            jax.ShapeDtypeStruct((TOP_K, t_all), I32),
            jax.ShapeDtypeStruct((n_exp, LANES), F32),
        ],
        compiler_params=_params("arbitrary", "arbitrary"),
        name="mixer_router",
    )(x, attn, u, u, u, gates, mod3, n2g, wpool_bf, pscale, wua_bf, wup_bf, wout_bf,
      wr_hi, wr_lo, rbias, wsg_bf, wsu_bf, wsd_bf)


def _slots_body(eidx_ref, rank_ref, cnt_ref, pos_ref, bexp_ref, nused_ref, *, block_rows):
    n_exp = cnt_ref.shape[0]
    tm = eidx_ref.shape[1]
    cnt = cnt_ref[...]
    padded = jnp.floor((cnt + (block_rows - 1)) / block_rows) * block_rows
    hi = jnp.floor(padded / 256.0)
    lo = padded - hi * 256.0
    below = (lax.broadcasted_iota(I32, (n_exp, n_exp), 1) < lax.broadcasted_iota(I32, (n_exp, n_exp), 0))
    tri = jnp.where(below, 1.0, 0.0).astype(BF16)
    start = 256.0 * _dot(tri, hi.astype(BF16)) + _dot(tri, lo.astype(BF16))
    end = start + padded
    iota_e = lax.broadcasted_iota(I32, (n_exp, tm), 0)
    start_col = start[:, 0:1]
    rows = []
    for k in range(TOP_K):
        hit = iota_e == eidx_ref[k:k + 1, :]
        rows.append(jnp.sum(jnp.where(hit, start_col, 0.0), axis=0, keepdims=True))
    pos_ref[...] = jnp.concatenate(rows, axis=0).astype(I32) + rank_ref[...]

    @pl.when(pl.program_id(0) == 0)
    def _():
        n_blk = bexp_ref.shape[1]
        blk_start = (lax.broadcasted_iota(I32, (n_exp, n_blk), 1) * block_rows).astype(F32)
        owner = jnp.sum(jnp.where(end[:, 0:1] <= blk_start, 1.0, 0.0), axis=0, keepdims=True)
        bexp_ref[...] = jnp.minimum(owner, float(n_exp - 1)).astype(I32)
        nused_ref[...] = (end[n_exp - 1:n_exp, :] / block_rows).astype(I32)


def _slots(eidx, rank, cnt, n_blocks, block_rows):
    t_all = eidx.shape[1]
    n_exp = cnt.shape[0]
    tm = min(TOKEN_TILE, t_all)
    nb_pad = -(-n_blocks // LANES) * LANES
    return pl.pallas_call(
        functools.partial(_slots_body, block_rows=block_rows),
        grid=(t_all // tm,),
        in_specs=[pl.BlockSpec((TOP_K, tm), lambda i: (0, i)),
                  pl.BlockSpec((TOP_K, tm), lambda i: (0, i)),
                  pl.BlockSpec((n_exp, LANES), lambda i: (0, 0))],
        out_specs=[pl.BlockSpec((TOP_K, tm), lambda i: (0, i)),
                   pl.BlockSpec((1, nb_pad), lambda i: (0, 0)),
                   pl.BlockSpec((1, LANES), lambda i: (0, 0))],
        out_shape=[jax.ShapeDtypeStruct((TOP_K, t_all), I32),
                   jax.ShapeDtypeStruct((1, nb_pad), I32),
                   jax.ShapeDtypeStruct((1, LANES), I32)],
        compiler_params=_params("arbitrary"),
        name="slot_positions",
    )(eidx, rank, cnt)


def _expert_body(bexp_ref, nused_ref, xs_ref, wg_ref, wu_ref, wd_ref, y_ref):
    i = pl.program_id(0)

    @pl.when(i < nused_ref[0])
    def _():
        x = xs_ref[...]
        g = _dot(x, wg_ref[...].astype(BF16))
        u = _dot(x, wu_ref[...].astype(BF16))
        y_ref[...] = _dot((_silu(g) * u).astype(BF16), wd_ref[...].astype(BF16)).astype(y_ref.dtype)

    @pl.when(i >= nused_ref[0])
    def _():
        y_ref[...] = jnp.zeros_like(y_ref)


def _experts(xs, bexp, nused, w_gate, w_up, w_down, block_rows):
    n_slots, d = xs.shape
    f = w_gate.shape[-1]
    n_blocks = n_slots // block_rows
    grid_spec = pltpu.PrefetchScalarGridSpec(
        num_scalar_prefetch=2,
        grid=(n_blocks,),
        in_specs=[
            pl.BlockSpec((block_rows, d), lambda i, be, nu: (i, 0)),
            pl.BlockSpec((None, d, f), lambda i, be, nu: (be[i], 0, 0)),
            pl.BlockSpec((None, d, f), lambda i, be, nu: (be[i], 0, 0)),
            pl.BlockSpec((None, f, d), lambda i, be, nu: (be[i], 0, 0)),
        ],
        out_specs=pl.BlockSpec((block_rows, d), lambda i, be, nu: (i, 0)),
    )
    return pl.pallas_call(
        _expert_body,
        grid_spec=grid_spec,
        out_shape=jax.ShapeDtypeStruct((n_slots, d), BF16),
        compiler_params=_params("arbitrary"),
        name="expert_ffn",
    )(bexp, nused, xs, w_gate, w_up, w_down)


def _combine_body(yg_ref, w_ref, base_ref, mod_ref, fg_ref, o_ref):
    acc = yg_ref[0].astype(F32) * w_ref[:, 0:1]
    for k in range(1, TOP_K):
        acc = acc + yg_ref[k].astype(F32) * w_ref[:, k:k + 1]
    x2 = base_ref[...] + mod_ref[5:6, :] * acc
    ms = jnp.mean(x2 * x2, axis=-1, keepdims=True)
    o_ref[...] = (x2 * lax.rsqrt(ms + NORM_EPS)) * fg_ref[...]


def _combine(yg, wts_t, base, mod3, final_g):
    b, s, d = base.shape
    tm = min(256, s)
    n_t = s // tm
    return pl.pallas_call(
        _combine_body,
        grid=(b, n_t),
        in_specs=[pl.BlockSpec((TOP_K, tm, d), lambda bi, i: (0, bi * n_t + i, 0)),
                  pl.BlockSpec((tm, TOP_K), lambda bi, i: (bi * n_t + i, 0)),
                  pl.BlockSpec((None, tm, d), lambda bi, i: (bi, i, 0)),
                  pl.BlockSpec((None, N_ADA, d), lambda bi, i: (bi, 0, 0)),
                  pl.BlockSpec((1, d), lambda bi, i: (0, 0))],
        out_specs=pl.BlockSpec((None, tm, d), lambda bi, i: (bi, i, 0)),
        out_shape=jax.ShapeDtypeStruct((b, s, d), F32),
        compiler_params=_params("parallel", "parallel"),
        name="combine_norm",
    )(yg, wts_t, base, mod3, final_g)


def _rope_tables(seq_len):
    rows = seq_len // GRID_W
    row = jnp.repeat(jnp.arange(rows), GRID_W).astype(F32)
    col = jnp.tile(jnp.arange(GRID_W), rows).astype(F32)
    n_freq = HEAD_DIM // 4
    inv = ROPE_THETA ** (-jnp.arange(n_freq, dtype=F32) / n_freq)
    ar, ac = row[:, None] * inv, col[:, None] * inv
    zeros = jnp.zeros_like(ar)
    reps = LANES // HEAD_DIM
    cos = jnp.tile(jnp.concatenate([jnp.cos(ar), jnp.cos(ar), jnp.cos(ac), jnp.cos(ac)], 1), (1, reps))
    sin_a = jnp.tile(jnp.concatenate([-jnp.sin(ar), zeros, -jnp.sin(ac), zeros], 1), (1, reps))
    sin_b = jnp.tile(jnp.concatenate([zeros, jnp.sin(ar), zeros, jnp.sin(ac)], 1), (1, reps))
    return cos, sin_a, sin_b


def kernel(x, c, ctx, c_ctx, w_ada, b_ada, norm1_g, w_in, attn_sink, w_pool, pool_scale,
           w_up_attn, w_up_pool, w_out, norm2_g, w_router, router_bias,
           w_exp_gate, w_exp_up, w_exp_down, w_sh_gate, w_sh_up, w_sh_down, final_g):
    b, s, d = x.shape
    assert w_ada.shape[0] == 1, "single-layer block"
    assert s % ATTN_BLOCK == 0 and s % GRID_W == 0 and d % LANES == 0
    pool_dim = w_up_pool.shape[1]
    n_exp = w_router.shape[-1]
    t_all = b * s

    pad_rows = -(-(b + 1) // SUBLANES) * SUBLANES
    c_rows = jnp.concatenate([c, c_ctx[None, :], jnp.zeros((pad_rows - b - 1, d), F32)], axis=0)
    mod3 = _ada(c_rows, w_ada[0], b_ada[0]).reshape(pad_rows, N_ADA, d)

    w_in_bf = w_in[0].astype(BF16)
    g1 = norm1_g[0].reshape(1, d)
    cos, sin_a, sin_b = _rope_tables(s)
    q, k, v, u, gates = _inproj(x, mod3, g1, w_in_bf, cos, sin_a, sin_b, pool_dim)
    kc, vc = _ctxkv(ctx, mod3[b], g1, w_in_bf[:, ATTN_DIM:ATTN_DIM + 2 * KV_DIM])
    attn = _attention(q, k, v, kc, vc, attn_sink[0])

    w_r_t = w_router[0].T
    w_r_hi = w_r_t.astype(BF16)
    w_r_lo = (w_r_t - w_r_hi.astype(F32)).astype(BF16)
    base, h2, eidx, wts, rank, cnt = _mixer(
        x, attn, u, gates, mod3, norm2_g[0].reshape(1, d), w_pool[0].astype(BF16),
        pool_scale[0].reshape(1, pool_dim), w_up_attn[0].astype(BF16), w_up_pool[0].astype(BF16),
        w_out[0].astype(BF16), w_r_hi, w_r_lo, router_bias[0].reshape(n_exp, 1),
        w_sh_gate[0].astype(BF16), w_sh_up[0].astype(BF16), w_sh_down[0].astype(BF16))

    br = EXPERT_BLOCK_ROWS
    n_blocks = (t_all * TOP_K + n_exp * (br - 1) + br - 1) // br
    n_slots = n_blocks * br
    pos, bexp, nused = _slots(eidx, rank, cnt, n_blocks, br)
    h2_flat = h2.reshape(t_all, d)
    xs = jnp.zeros((n_slots, d), BF16).at[pos.reshape(-1)].set(jnp.tile(h2_flat, (TOP_K, 1)))
    y = _experts(xs, bexp[0, :n_blocks], nused[0, :1], w_exp_gate[0], w_exp_up[0], w_exp_down[0], br)
    yg = y[pos]

    return _combine(yg, wts.T, base, mod3, final_g.reshape(1, d))
```

```python
import functools

import jax
import jax.numpy as jnp
from jax import lax
from jax.experimental import pallas as pl
from jax.experimental.pallas import tpu as pltpu
from jax.experimental.pallas import tpu_sc as plsc

F32 = jnp.float32
BF16 = jnp.bfloat16
I32 = jnp.int32
U32 = jnp.uint32

GRID_W = 64
HEAD_DIM = 64
N_Q_HEADS = 8
N_KV_HEADS = 2
Q_PER_KV = N_Q_HEADS // N_KV_HEADS
ATTN_DIM = N_Q_HEADS * HEAD_DIM
KV_DIM = N_KV_HEADS * HEAD_DIM
ATTN_BLOCK = 128
ATTN_SCALE = HEAD_DIM ** -0.5
ROPE_THETA = 10000.0
POOL_WINDOWS = (2, 4, 8, 16)
POOL_HALO = 8
N_EXPERT_GROUPS = 8
TOPK_GROUPS = 4
TOP_K = 8
ROUTED_SCALE = 2.5
N_ADA = 6
NORM_EPS = 1e-6
NEG_INF = -1e30
LANES = 128
SUBLANES = 8
TOKEN_TILE = 512
EXPERT_BLOCK_ROWS = 256
VMEM_LIMIT = 56 * 1024 * 1024


def _sigmoid(x):
    return 1.0 / (1.0 + jnp.exp(-x))


def _silu(x):
    return x * _sigmoid(x)


def _nt_dot(a, b):
    return lax.dot_general(a, b, (((1,), (1,)), ((), ())), preferred_element_type=F32)


def _dot(a, b):
    return jnp.dot(a, b, preferred_element_type=F32)


def _pack_rows(x):
    n = x.shape[1] // 2
    bits = lax.bitcast_convert_type(x.astype(BF16).astype(F32), U32)
    return (bits[:, :n] >> 16) | (bits[:, n:] & jnp.uint32(0xFFFF0000))


def _unpack_rows(w):
    lo = lax.bitcast_convert_type(w << 16, F32)
    hi = lax.bitcast_convert_type(w & jnp.uint32(0xFFFF0000), F32)
    return lo, hi


def _params(*sem):
    return pltpu.CompilerParams(dimension_semantics=sem, vmem_limit_bytes=VMEM_LIMIT)


def _ada_body(c_ref, w_ref, b_ref, o_ref):
    s = _silu(c_ref[...])
    o_ref[...] = jnp.dot(s, w_ref[...], preferred_element_type=F32,
                         precision=lax.Precision.HIGHEST) + b_ref[...]


def _ada(c_rows, w_ada, b_ada):
    rows, d = c_rows.shape
    n = w_ada.shape[1]
    bn = d
    return pl.pallas_call(
        _ada_body,
        grid=(n // bn,),
        in_specs=[pl.BlockSpec((rows, d), lambda j: (0, 0)),
                  pl.BlockSpec((d, bn), lambda j: (0, j)),
                  pl.BlockSpec((1, bn), lambda j: (0, j))],
        out_specs=pl.BlockSpec((rows, bn), lambda j: (0, j)),
        out_shape=jax.ShapeDtypeStruct((rows, n), F32),
        compiler_params=_params("arbitrary"),
        name="ada_mod",
    )(c_rows, w_ada, b_ada.reshape(1, n))


def _norm_mod(x, g, shift, scale):
    ms = jnp.mean(x * x, axis=-1, keepdims=True)
    y = x * lax.rsqrt(ms + NORM_EPS)
    return (y * g) * (1.0 + scale) + shift


def _rope(t, cos, sin_a, sin_b):
    return (t * cos + pltpu.roll(t, LANES - HEAD_DIM // 4, 1) * sin_a
            + pltpu.roll(t, HEAD_DIM // 4, 1) * sin_b)


def _inproj_body(x_ref, mod_ref, g_ref, w_ref, cos_ref, sa_ref, sb_ref,
                 q_ref, k_ref, v_ref, u_ref, gate_ref):
    h = _norm_mod(x_ref[...], g_ref[...], mod_ref[0:1, :], mod_ref[1:2, :]).astype(BF16)
    cos, sa, sb = cos_ref[...], sa_ref[...], sb_ref[...]
    heads_per_chunk = LANES // HEAD_DIM
    for j in range(ATTN_DIM // LANES):
        t = _rope(_dot(h, w_ref[:, j * LANES:(j + 1) * LANES]), cos, sa, sb) * ATTN_SCALE
        t = t.astype(BF16)
        for i in range(heads_per_chunk):
            q_ref[heads_per_chunk * j + i] = t[:, i * HEAD_DIM:(i + 1) * HEAD_DIM]
    k_off = ATTN_DIM
    t = _rope(_dot(h, w_ref[:, k_off:k_off + KV_DIM]), cos, sa, sb).astype(BF16)
    for i in range(N_KV_HEADS):
        k_ref[i] = t[:, i * HEAD_DIM:(i + 1) * HEAD_DIM]
    v_off = k_off + KV_DIM
    t = _dot(h, w_ref[:, v_off:v_off + KV_DIM]).astype(BF16)
    for i in range(N_KV_HEADS):
        v_ref[i] = t[:, i * HEAD_DIM:(i + 1) * HEAD_DIM]
    p_off = v_off + KV_DIM
    pool_dim = u_ref.shape[-1]
    u_ref[...] = _dot(h, w_ref[:, p_off:p_off + pool_dim])
    g_off = p_off + pool_dim
    gate_dim = gate_ref.shape[-1]
    chunk = 512
    for j in range(gate_dim // chunk):
        t = _dot(h, w_ref[:, g_off + j * chunk:g_off + (j + 1) * chunk])
        gate_ref[:, j * chunk:(j + 1) * chunk] = _sigmoid(t).astype(BF16)


def _inproj(x, mod3, g, w_in_bf, cos, sa, sb, pool_dim):
    b, s, d = x.shape
    tm = min(TOKEN_TILE, s)
    in_dim = w_in_bf.shape[1]
    gate_dim = in_dim - ATTN_DIM - 2 * KV_DIM - pool_dim
    grid = (b, s // tm)
    return pl.pallas_call(
        _inproj_body,
        grid=grid,
        in_specs=[
            pl.BlockSpec((None, tm, d), lambda bi, i: (bi, i, 0)),
            pl.BlockSpec((None, N_ADA, d), lambda bi, i: (bi, 0, 0)),
            pl.BlockSpec((1, d), lambda bi, i: (0, 0)),
            pl.BlockSpec((d, in_dim), lambda bi, i: (0, 0)),
            pl.BlockSpec((tm, LANES), lambda bi, i: (i, 0)),
            pl.BlockSpec((tm, LANES), lambda bi, i: (i, 0)),
            pl.BlockSpec((tm, LANES), lambda bi, i: (i, 0)),
        ],
        out_specs=[
            pl.BlockSpec((None, N_Q_HEADS, tm, HEAD_DIM), lambda bi, i: (bi, 0, i, 0)),
            pl.BlockSpec((None, N_KV_HEADS, tm, HEAD_DIM), lambda bi, i: (bi, 0, i, 0)),
            pl.BlockSpec((None, N_KV_HEADS, tm, HEAD_DIM), lambda bi, i: (bi, 0, i, 0)),
            pl.BlockSpec((None, tm, pool_dim), lambda bi, i: (bi, i, 0)),
            pl.BlockSpec((None, tm, gate_dim), lambda bi, i: (bi, i, 0)),
        ],
        out_shape=[
            jax.ShapeDtypeStruct((b, N_Q_HEADS, s, HEAD_DIM), BF16),
            jax.ShapeDtypeStruct((b, N_KV_HEADS, s, HEAD_DIM), BF16),
            jax.ShapeDtypeStruct((b, N_KV_HEADS, s, HEAD_DIM), BF16),
            jax.ShapeDtypeStruct((b, s, pool_dim), F32),
            jax.ShapeDtypeStruct((b, s, gate_dim), BF16),
        ],
        compiler_params=_params("parallel", "parallel"),
        name="in_proj",
    )(x, mod3, g, w_in_bf, cos, sa, sb)


def _ctxkv_body(ctx_ref, mod_ref, g_ref, w_ref, kc_ref, vc_ref):
    h = _norm_mod(ctx_ref[...], g_ref[...], mod_ref[0:1, :], mod_ref[1:2, :]).astype(BF16)
    t = _dot(h, w_ref[...]).astype(BF16)
    for i in range(N_KV_HEADS):
        kc_ref[i] = t[:, i * HEAD_DIM:(i + 1) * HEAD_DIM]
        vc_ref[i] = t[:, KV_DIM + i * HEAD_DIM:KV_DIM + (i + 1) * HEAD_DIM]


def _ctxkv(ctx, mod_c, g, w_kv_bf):
    b, c, d = ctx.shape
    out = jax.ShapeDtypeStruct((b, N_KV_HEADS, c, HEAD_DIM), BF16)
    spec = pl.BlockSpec((None, N_KV_HEADS, c, HEAD_DIM), lambda bi: (bi, 0, 0, 0))
    return pl.pallas_call(
        _ctxkv_body,
        grid=(b,),
        in_specs=[pl.BlockSpec((None, c, d), lambda bi: (bi, 0, 0)),
                  pl.BlockSpec((N_ADA, d), lambda bi: (0, 0)),
                  pl.BlockSpec((1, d), lambda bi: (0, 0)),
                  pl.BlockSpec((d, 2 * KV_DIM), lambda bi: (0, 0))],
        out_specs=[spec, spec],
        out_shape=[out, out],
        compiler_params=_params("parallel"),
        name="ctx_kv",
    )(ctx, mod_c, g, w_kv_bf)


def _attn_body(sink_ref, q_ref, kp_ref, kn_ref, kx_ref, vp_ref, vn_ref, vx_ref,
               kc_ref, vc_ref, lo_ref, hi_ref, o_ref):
    n = pl.program_id(1)
    last = pl.num_programs(1) - 1
    rows = Q_PER_KV * ATTN_BLOCK
    bias_lo = lo_ref[...] + jnp.where(n == 0, NEG_INF, 0.0)
    bias_hi = hi_ref[...] + jnp.where(n == last, NEG_INF, 0.0)
    row_head = lax.broadcasted_iota(I32, (rows, 1), 0) // ATTN_BLOCK
    outs = []
    for kk in range(N_KV_HEADS):
        qs = q_ref[kk * Q_PER_KV:(kk + 1) * Q_PER_KV].reshape(rows, HEAD_DIM)
        s0 = _nt_dot(qs, kp_ref[kk]) + bias_lo
        s1 = _nt_dot(qs, kn_ref[kk])
        s2 = _nt_dot(qs, kx_ref[kk]) + bias_hi
        sc = _nt_dot(qs, kc_ref[kk])
        sink = jnp.zeros((rows, 1), F32)
        for g in range(Q_PER_KV):
            sink = jnp.where(row_head == g, sink_ref[kk * Q_PER_KV + g], sink)
        m = jnp.maximum(
            jnp.maximum(jnp.max(s0, axis=1, keepdims=True), jnp.max(s1, axis=1, keepdims=True)),
            jnp.maximum(jnp.max(s2, axis=1, keepdims=True), jnp.max(sc, axis=1, keepdims=True)))
        m = jnp.maximum(m, sink)
        p0, p1, p2, pc = jnp.exp(s0 - m), jnp.exp(s1 - m), jnp.exp(s2 - m), jnp.exp(sc - m)
        denom = (jnp.sum(p0, axis=1, keepdims=True) + jnp.sum(p1, axis=1, keepdims=True)
                 + jnp.sum(p2, axis=1, keepdims=True) + jnp.sum(pc, axis=1, keepdims=True)
                 + jnp.exp(sink - m))
        o = (_dot(p0.astype(BF16), vp_ref[kk]) + _dot(p1.astype(BF16), vn_ref[kk])
             + _dot(p2.astype(BF16), vx_ref[kk]) + _dot(pc.astype(BF16), vc_ref[kk]))
        o = o / denom
        outs += [o[g * ATTN_BLOCK:(g + 1) * ATTN_BLOCK] for g in range(Q_PER_KV)]
    o_ref[...] = jnp.concatenate(outs, axis=1).astype(BF16)


def _attention(q, k, v, kc, vc, sink):
    b, _, s, _ = q.shape
    c = kc.shape[2]
    nb = s // ATTN_BLOCK
    rows = Q_PER_KV * ATTN_BLOCK
    qi = jnp.tile(jnp.arange(ATTN_BLOCK), Q_PER_KV)[:, None]
    kj = jnp.arange(ATTN_BLOCK)[None, :]
    bias_lo = jnp.where(kj >= qi, 0.0, NEG_INF).astype(F32)
    bias_hi = jnp.where(kj <= qi, 0.0, NEG_INF).astype(F32)
    kv_spec = lambda f: pl.BlockSpec((None, N_KV_HEADS, ATTN_BLOCK, HEAD_DIM), f)
    prev = lambda bi, n: (bi, 0, jnp.maximum(n - 1, 0), 0)
    own = lambda bi, n: (bi, 0, n, 0)
    nxt = lambda bi, n: (bi, 0, jnp.minimum(n + 1, nb - 1), 0)
    ctx_spec = pl.BlockSpec((None, N_KV_HEADS, c, HEAD_DIM), lambda bi, n: (bi, 0, 0, 0))
    bias_spec = pl.BlockSpec((rows, ATTN_BLOCK), lambda bi, n: (0, 0))
    return pl.pallas_call(
        _attn_body,
        grid=(b, nb),
        in_specs=[
            pl.BlockSpec(memory_space=pltpu.SMEM),
            pl.BlockSpec((None, N_Q_HEADS, ATTN_BLOCK, HEAD_DIM), own),
            kv_spec(prev), kv_spec(own), kv_spec(nxt),
            kv_spec(prev), kv_spec(own), kv_spec(nxt),
            ctx_spec, ctx_spec, bias_spec, bias_spec,
        ],
        out_specs=pl.BlockSpec((None, ATTN_BLOCK, ATTN_DIM), lambda bi, n: (bi, n, 0)),
        out_shape=jax.ShapeDtypeStruct((b, s, ATTN_DIM), BF16),
        compiler_params=_params("parallel", "parallel"),
        name="window_attn",
    )(sink, q, k, k, k, v, v, v, kc, vc, bias_lo, bias_hi)


def _pool_delta(ue, t_seq, seq_len):
    n_ext = ue.shape[0]
    tm = n_ext - 2 * POOL_HALO
    group_dim = ue.shape[1] // len(POOL_WINDOWS)
    outs = []
    for g, w in enumerate(POOL_WINDOWS):
        xs = ue[:, g * group_dim:(g + 1) * group_dim]
        acc = xs + pltpu.roll(xs, 1, 0)
        step = 1
        while 2 * step < w:
            acc = pltpu.roll(acc, step, 0) + pltpu.roll(acc, n_ext - step, 0)
            step *= 2
        half = w // 2
        cnt = (jnp.minimum(t_seq + half, seq_len) - jnp.maximum(t_seq - half, 0)).astype(F32)
        core = slice(POOL_HALO, POOL_HALO + tm)
        outs.append(acc[core] / cnt - xs[core])
    return outs


def _route(s, sel, n_tok):
    n_exp = s.shape[0]
    per_group = n_exp // N_EXPERT_GROUPS
    neg = float("-inf")
    iota_g = lax.broadcasted_iota(I32, (per_group, n_tok), 0).astype(F32)
    scores = []
    for g in range(N_EXPERT_GROUPS):
        blk = sel[g * per_group:(g + 1) * per_group]
        m1 = jnp.max(blk, axis=0, keepdims=True)
        first = jnp.min(jnp.where(blk == m1, iota_g, float(per_group)), axis=0, keepdims=True)
        m2 = jnp.max(jnp.where(iota_g == first, neg, blk), axis=0, keepdims=True)
        scores.append(m1 + m2)
    gs = jnp.concatenate(scores, axis=0)
    iota_ng = lax.broadcasted_iota(I32, (N_EXPERT_GROUPS, n_tok), 0).astype(F32)
    gsel = jnp.zeros((N_EXPERT_GROUPS, n_tok), F32)
    for _ in range(TOPK_GROUPS):
        m = jnp.max(gs, axis=0, keepdims=True)
        first = jnp.min(jnp.where(gs == m, iota_ng, float(N_EXPERT_GROUPS)), axis=0, keepdims=True)
        hit = iota_ng == first
        gsel = jnp.where(hit, 1.0, gsel)
        gs = jnp.where(hit, neg, gs)
    cur = jnp.concatenate(
        [jnp.where(gsel[g:g + 1] > 0.0, sel[g * per_group:(g + 1) * per_group], NEG_INF)
         for g in range(N_EXPERT_GROUPS)], axis=0)
    iota_e = lax.broadcasted_iota(I32, (n_exp, n_tok), 0).astype(F32)
    chosen = jnp.zeros((n_exp, n_tok), F32)
    ids, aff = [], []
    for _ in range(TOP_K):
        m = jnp.max(cur, axis=0, keepdims=True)
        first = jnp.min(jnp.where(cur == m, iota_e, float(n_exp)), axis=0, keepdims=True)
        hit = iota_e == first
        ids.append(first)
        aff.append(jnp.sum(jnp.where(hit, s, 0.0), axis=0, keepdims=True))
        chosen = jnp.where(hit, 1.0, chosen)
        cur = jnp.where(hit, neg, cur)
    return ids, aff, chosen


def _mixer_body(x_ref, attn_ref, up_ref, u_ref, un_ref, gate_ref, mod_ref, n2g_ref,
                wpool_ref, pscale_ref, wua_ref, wup_ref, wout_ref, wrh_ref, wrl_ref, rbias_ref,
                wsg_ref, wsu_ref, wsd_ref,
                base_ref, h2_ref, eidx_ref, wts_ref, rank_ref, cnt_ref, *, seq_len):
    bi, ti = pl.program_id(0), pl.program_id(1)
    n_t = pl.num_programs(1)
    tm, d = x_ref.shape
    pool_dim = u_ref.shape[-1]

    zero_halo = jnp.zeros((POOL_HALO, pool_dim), F32)
    u_prev = jnp.where(ti == 0, zero_halo, up_ref[...])
    u_next = jnp.where(ti == n_t - 1, zero_halo, un_ref[...])
    ue = jnp.concatenate([u_prev, u_ref[...], u_next], axis=0)
    t_seq = ti * tm + lax.broadcasted_iota(I32, (tm, 1), 0)
    deltas = _pool_delta(ue, t_seq, seq_len)
    pool = jnp.concatenate(
        [_dot(dl.astype(BF16), wpool_ref[g]) for g, dl in enumerate(deltas)], axis=1)
    pool = (pool * pscale_ref[...]).astype(BF16)

    sig_a = gate_ref[:, :d].astype(F32)
    sig_p = gate_ref[:, d:].astype(F32)
    y = sig_a * _dot(attn_ref[...], wua_ref[...]) + sig_p * _dot(pool, wup_ref[...])
    mix = _dot(y.astype(BF16), wout_ref[...])
    x1 = x_ref[...] + mod_ref[2:3, :] * mix

    h2 = _norm_mod(x1, n2g_ref[...], mod_ref[3:4, :], mod_ref[4:5, :])
    h2_hi = h2.astype(BF16)
    h2_ref[...] = _pack_rows(h2)
    shared = _dot((_silu(_dot(h2_hi, wsg_ref[...])) * _dot(h2_hi, wsu_ref[...])).astype(BF16),
                  wsd_ref[...])
    base_ref[...] = x1 + mod_ref[5:6, :] * shared

    h2_lo = (h2 - h2_hi.astype(F32)).astype(BF16)
    logits = (_nt_dot(wrh_ref[...], h2_hi) + _nt_dot(wrh_ref[...], h2_lo)
              + _nt_dot(wrl_ref[...], h2_hi))
    s = _sigmoid(logits)
    sel = s + rbias_ref[...]
    ids, aff, chosen = _route(s, sel, tm)

    total = aff[0]
    for a in aff[1:]:
        total = total + a
    wts_ref[...] = jnp.concatenate([a / total * ROUTED_SCALE for a in aff], axis=0)
    eidx_ref[...] = jnp.concatenate(ids, axis=0).astype(I32)

    @pl.when((bi == 0) & (ti == 0))
    def _():
        cnt_ref[...] = jnp.zeros_like(cnt_ref)

    before = (lax.broadcasted_iota(I32, (tm, tm), 0) < lax.broadcasted_iota(I32, (tm, tm), 1))
    prefix = _dot(chosen.astype(BF16), jnp.where(before, 1.0, 0.0).astype(BF16))
    rank_dense = prefix + cnt_ref[:, 0:1]
    n_exp = s.shape[0]
    iota_e = lax.broadcasted_iota(I32, (n_exp, tm), 0).astype(F32)
    rank_ref[...] = jnp.concatenate(
        [jnp.sum(jnp.where(iota_e == i, rank_dense, 0.0), axis=0, keepdims=True) for i in ids],
        axis=0).astype(I32)
    cnt_ref[...] = cnt_ref[...] + jnp.sum(chosen, axis=1, keepdims=True)


def _mixer(x, attn, u, gates, mod3, n2g, wpool_bf, pscale, wua_bf, wup_bf, wout_bf,
           wr_hi, wr_lo, rbias, wsg_bf, wsu_bf, wsd_bf):
    b, s, d = x.shape
    tm = min(TOKEN_TILE, s)
    n_t = s // tm
    t_all = b * s
    pool_dim = u.shape[-1]
    n_exp = wr_hi.shape[0]
    sh = wsg_bf.shape[1]
    halo_blocks = tm // POOL_HALO
    n_halo = s // POOL_HALO
    full = lambda shape: pl.BlockSpec(shape, lambda bi, i: (0,) * len(shape))
    tok = lambda bi, i: (0, bi * n_t + i)
    return pl.pallas_call(
        functools.partial(_mixer_body, seq_len=s),
        grid=(b, n_t),
        in_specs=[
            pl.BlockSpec((None, tm, d), lambda bi, i: (bi, i, 0)),
            pl.BlockSpec((None, tm, ATTN_DIM), lambda bi, i: (bi, i, 0)),
            pl.BlockSpec((None, POOL_HALO, pool_dim),
                         lambda bi, i: (bi, jnp.maximum(i * halo_blocks - 1, 0), 0)),
            pl.BlockSpec((None, tm, pool_dim), lambda bi, i: (bi, i, 0)),
            pl.BlockSpec((None, POOL_HALO, pool_dim),
                         lambda bi, i: (bi, jnp.minimum((i + 1) * halo_blocks, n_halo - 1), 0)),
            pl.BlockSpec((None, tm, 2 * d), lambda bi, i: (bi, i, 0)),
            pl.BlockSpec((None, N_ADA, d), lambda bi, i: (bi, 0, 0)),
            full((1, d)),
            full((len(POOL_WINDOWS), pool_dim // len(POOL_WINDOWS), pool_dim // len(POOL_WINDOWS))),
            full((1, pool_dim)),
            full((ATTN_DIM, d)), full((pool_dim, d)), full((d, d)),
            full((n_exp, d)), full((n_exp, d)), full((n_exp, 1)),
            full((d, sh)), full((d, sh)), full((sh, d)),
        ],
        out_specs=[
            pl.BlockSpec((None, tm, d), lambda bi, i: (bi, i, 0)),
            pl.BlockSpec((tm, d // 2), lambda bi, i: (bi * n_t + i, 0)),
            pl.BlockSpec((TOP_K, tm), tok),
            pl.BlockSpec((TOP_K, tm), tok),
            pl.BlockSpec((TOP_K, tm), tok),
            pl.BlockSpec((n_exp, LANES), lambda bi, i: (0, 0)),
        ],
        out_shape=[
            jax.ShapeDtypeStruct((b, s, d), F32),
            jax.ShapeDtypeStruct((t_all, d // 2), U32),
            jax.ShapeDtypeStruct((TOP_K, t_all), I32),
            jax.ShapeDtypeStruct((TOP_K, t_all), F32),
            jax.ShapeDtypeStruct((TOP_K, t_all), I32),
            jax.ShapeDtypeStruct((n_exp, LANES), F32),
        ],
        compiler_params=_params("arbitrary", "arbitrary"),
        name="mixer_router",
    )(x, attn, u, u, u, gates, mod3, n2g, wpool_bf, pscale, wua_bf, wup_bf, wout_bf,
      wr_hi, wr_lo, rbias, wsg_bf, wsu_bf, wsd_bf)


def _slots_body(eidx_ref, rank_ref, cnt_ref, pos_ref, bexp_ref, nused_ref, *, block_rows):
    n_exp = cnt_ref.shape[0]
    tm = eidx_ref.shape[1]
    cnt = cnt_ref[...]
    padded = jnp.floor((cnt + (block_rows - 1)) / block_rows) * block_rows
    hi = jnp.floor(padded / 256.0)
    lo = padded - hi * 256.0
    below = (lax.broadcasted_iota(I32, (n_exp, n_exp), 1) < lax.broadcasted_iota(I32, (n_exp, n_exp), 0))
    tri = jnp.where(below, 1.0, 0.0).astype(BF16)
    start = 256.0 * _dot(tri, hi.astype(BF16)) + _dot(tri, lo.astype(BF16))
    end = start + padded
    iota_e = lax.broadcasted_iota(I32, (n_exp, tm), 0)
    start_col = start[:, 0:1]
    rows = []
    for k in range(TOP_K):
        hit = iota_e == eidx_ref[k:k + 1, :]
        rows.append(jnp.sum(jnp.where(hit, start_col, 0.0), axis=0, keepdims=True))
    pos_ref[...] = jnp.concatenate(rows, axis=0).astype(I32) + rank_ref[...]

    @pl.when(pl.program_id(0) == 0)
    def _():
        n_blk = bexp_ref.shape[1]
        blk_start = (lax.broadcasted_iota(I32, (n_exp, n_blk), 1) * block_rows).astype(F32)
        owner = jnp.sum(jnp.where(end[:, 0:1] <= blk_start, 1.0, 0.0), axis=0, keepdims=True)
        bexp_ref[...] = jnp.minimum(owner, float(n_exp - 1)).astype(I32)
        nused_ref[...] = (end[n_exp - 1:n_exp, :] / block_rows).astype(I32)


def _slots(eidx, rank, cnt, n_blocks, block_rows):
    t_all = eidx.shape[1]
    n_exp = cnt.shape[0]
    tm = min(TOKEN_TILE, t_all)
    nb_pad = -(-n_blocks // LANES) * LANES
    return pl.pallas_call(
        functools.partial(_slots_body, block_rows=block_rows),
        grid=(t_all // tm,),
        in_specs=[pl.BlockSpec((TOP_K, tm), lambda i: (0, i)),
                  pl.BlockSpec((TOP_K, tm), lambda i: (0, i)),
                  pl.BlockSpec((n_exp, LANES), lambda i: (0, 0))],
        out_specs=[pl.BlockSpec((TOP_K, tm), lambda i: (0, i)),
                   pl.BlockSpec((1, nb_pad), lambda i: (0, 0)),
                   pl.BlockSpec((1, LANES), lambda i: (0, 0))],
        out_shape=[jax.ShapeDtypeStruct((TOP_K, t_all), I32),
                   jax.ShapeDtypeStruct((1, nb_pad), I32),
                   jax.ShapeDtypeStruct((1, LANES), I32)],
        compiler_params=_params("arbitrary"),
        name="slot_positions",
    )(eidx, rank, cnt)


def _expert_body(bexp_ref, nused_ref, xs_ref, wg_ref, wu_ref, wd_ref, y_ref):
    @pl.when(pl.program_id(0) < nused_ref[0])
    def _():
        lo, hi = _unpack_rows(xs_ref[...])
        x = jnp.concatenate([lo, hi], axis=1).astype(BF16)
        g = _dot(x, wg_ref[...].astype(BF16))
        u = _dot(x, wu_ref[...].astype(BF16))
        y_ref[...] = _pack_rows(_dot((_silu(g) * u).astype(BF16), wd_ref[...].astype(BF16)))


def _experts(xs, bexp, nused, w_gate, w_up, w_down, block_rows):
    n_slots, dw = xs.shape
    d, f = w_gate.shape[-2:]
    n_blocks = n_slots // block_rows
    rows = lambda i, be, nu: (jnp.minimum(i, nu[0] - 1), 0)
    expert = lambda i, be, nu: (be[jnp.minimum(i, nu[0] - 1)], 0, 0)
    grid_spec = pltpu.PrefetchScalarGridSpec(
        num_scalar_prefetch=2,
        grid=(n_blocks,),
        in_specs=[
            pl.BlockSpec((block_rows, dw), rows),
            pl.BlockSpec((None, d, f), expert),
            pl.BlockSpec((None, d, f), expert),
            pl.BlockSpec((None, f, d), expert),
        ],
        out_specs=pl.BlockSpec((block_rows, dw), rows),
    )
    return pl.pallas_call(
        _expert_body,
        grid_spec=grid_spec,
        out_shape=jax.ShapeDtypeStruct((n_slots, dw), U32),
        compiler_params=_params("arbitrary"),
        name="expert_ffn",
    )(bexp, nused, xs, w_gate, w_up, w_down)


SC_ROWS = 128


def _sc_workers():
    info = plsc.get_sparse_core_info()
    return info.num_cores, info.num_subcores


def _dispatch(h_rows, pos, n_slots):
    t_all, dw = h_rows.shape
    n_cores, n_sub = _sc_workers()
    per_worker = t_all // (n_cores * n_sub)
    assert per_worker * n_cores * n_sub == t_all and per_worker % SC_ROWS == 0
    mesh = plsc.VectorSubcoreMesh(core_axis_name="c", subcore_axis_name="s")

    @functools.partial(
        pl.kernel, mesh=mesh,
        out_type=jax.ShapeDtypeStruct((n_slots, dw), h_rows.dtype),
        scratch_types=[pltpu.VMEM((TOP_K, SC_ROWS), I32), pltpu.VMEM((SC_ROWS, dw), h_rows.dtype),
                       pltpu.SemaphoreType.DMA],
    )
    def body(h_hbm, pos_hbm, out_hbm, idx_v, rows_v, sem):
        base = (lax.axis_index("s") * n_cores + lax.axis_index("c")) * per_worker

        @pl.loop(0, per_worker // SC_ROWS)
        def _(j):
            t0 = pl.multiple_of(base + j * SC_ROWS, SC_ROWS)
            pltpu.sync_copy(pos_hbm.at[:, pl.ds(t0, SC_ROWS)], idx_v)
            pltpu.sync_copy(h_hbm.at[pl.ds(t0, SC_ROWS)], rows_v)
            copies = [pltpu.async_copy(rows_v, out_hbm.at[idx_v.at[k]], sem) for k in range(TOP_K)]
            for cp in copies:
                cp.wait()

    return body(h_rows, pos)


def _gather_back(y_rows, pos):
    top_k, t_all = pos.shape
    dw = y_rows.shape[1]
    n_cores, n_sub = _sc_workers()
    per_worker = t_all // (n_cores * n_sub)
    assert per_worker * n_cores * n_sub == t_all and per_worker % SC_ROWS == 0
    half = SC_ROWS // 2
    mesh = plsc.VectorSubcoreMesh(core_axis_name="c", subcore_axis_name="s")

    @functools.partial(
        pl.kernel, mesh=mesh,
        out_type=jax.ShapeDtypeStruct((top_k, t_all, dw), y_rows.dtype),
        scratch_types=[pltpu.VMEM((top_k, SC_ROWS), I32),
                       pltpu.VMEM((half, dw), y_rows.dtype), pltpu.VMEM((half, dw), y_rows.dtype),
                       pltpu.SemaphoreType.DMA, pltpu.SemaphoreType.DMA, pltpu.SemaphoreType.DMA],
    )
    def body(y_hbm, pos_hbm, out_hbm, idx_v, buf_a, buf_b, sem_g, sem_a, sem_b):
        base = (lax.axis_index("s") * n_cores + lax.axis_index("c")) * per_worker
        bufs, sems = (buf_a, buf_b), (sem_a, sem_b)

        @pl.loop(0, per_worker // SC_ROWS)
        def _(j):
            t0 = pl.multiple_of(base + j * SC_ROWS, SC_ROWS)
            pltpu.sync_copy(pos_hbm.at[:, pl.ds(t0, SC_ROWS)], idx_v)
            pending = [None, None]
            for step in range(2 * top_k):
                k, h = step // 2, step % 2
                slot = step % 2
                if pending[slot] is not None:
                    pending[slot].wait()
                pltpu.async_copy(y_hbm.at[idx_v.at[k, pl.ds(h * half, half)]], bufs[slot], sem_g).wait()
                pending[slot] = pltpu.async_copy(
                    bufs[slot], out_hbm.at[k, pl.ds(t0 + h * half, half)], sems[slot])
            for p in pending:
                p.wait()

    return body(y_rows, pos)


def _combine_body(yg_ref, w_ref, base_ref, mod_ref, fg_ref, o_ref):
    acc_lo = acc_hi = None
    for k in range(TOP_K):
        lo, hi = _unpack_rows(yg_ref[k])
        wk = w_ref[:, k:k + 1]
        acc_lo = lo * wk if acc_lo is None else acc_lo + lo * wk
        acc_hi = hi * wk if acc_hi is None else acc_hi + hi * wk
    acc = jnp.concatenate([acc_lo, acc_hi], axis=1)
    x2 = base_ref[...] + mod_ref[5:6, :] * acc
    ms = jnp.mean(x2 * x2, axis=-1, keepdims=True)
    o_ref[...] = (x2 * lax.rsqrt(ms + NORM_EPS)) * fg_ref[...]


def _combine(yg, wts_t, base, mod3, final_g):
    b, s, d = base.shape
    tm = min(256, s)
    n_t = s // tm
    return pl.pallas_call(
        _combine_body,
        grid=(b, n_t),
        in_specs=[pl.BlockSpec((TOP_K, tm, d // 2), lambda bi, i: (0, bi * n_t + i, 0)),
                  pl.BlockSpec((tm, TOP_K), lambda bi, i: (bi * n_t + i, 0)),
                  pl.BlockSpec((None, tm, d), lambda bi, i: (bi, i, 0)),
                  pl.BlockSpec((None, N_ADA, d), lambda bi, i: (bi, 0, 0)),
                  pl.BlockSpec((1, d), lambda bi, i: (0, 0))],
        out_specs=pl.BlockSpec((None, tm, d), lambda bi, i: (bi, i, 0)),
        out_shape=jax.ShapeDtypeStruct((b, s, d), F32),
        compiler_params=_params("parallel", "parallel"),
        name="combine_norm",
    )(yg, wts_t, base, mod3, final_g)


def _rope_tables(seq_len):
    rows = seq_len // GRID_W
    row = jnp.repeat(jnp.arange(rows), GRID_W).astype(F32)
    col = jnp.tile(jnp.arange(GRID_W), rows).astype(F32)
    n_freq = HEAD_DIM // 4
    inv = ROPE_THETA ** (-jnp.arange(n_freq, dtype=F32) / n_freq)
    ar, ac = row[:, None] * inv, col[:, None] * inv
    zeros = jnp.zeros_like(ar)
    reps = LANES // HEAD_DIM
    cos = jnp.tile(jnp.concatenate([jnp.cos(ar), jnp.cos(ar), jnp.cos(ac), jnp.cos(ac)], 1), (1, reps))
    sin_a = jnp.tile(jnp.concatenate([-jnp.sin(ar), zeros, -jnp.sin(ac), zeros], 1), (1, reps))
    sin_b = jnp.tile(jnp.concatenate([zeros, jnp.sin(ar), zeros, jnp.sin(ac)], 1), (1, reps))
    return cos, sin_a, sin_b


def kernel(x, c, ctx, c_ctx, w_ada, b_ada, norm1_g, w_in, attn_sink, w_pool, pool_scale,
           w_up_attn, w_up_pool, w_out, norm2_g, w_router, router_bias,
           w_exp_gate, w_exp_up, w_exp_down, w_sh_gate, w_sh_up, w_sh_down, final_g):
    b, s, d = x.shape
    assert w_ada.shape[0] == 1, "single-layer block"
    assert s % ATTN_BLOCK == 0 and s % GRID_W == 0 and d % LANES == 0
    pool_dim = w_up_pool.shape[1]
    n_exp = w_router.shape[-1]
    t_all = b * s

    pad_rows = -(-(b + 1) // SUBLANES) * SUBLANES
    c_rows = jnp.concatenate([c, c_ctx[None, :], jnp.zeros((pad_rows - b - 1, d), F32)], axis=0)
    mod3 = _ada(c_rows, w_ada[0], b_ada[0]).reshape(pad_rows, N_ADA, d)

    w_in_bf = w_in[0].astype(BF16)
    g1 = norm1_g[0].reshape(1, d)
    cos, sin_a, sin_b = _rope_tables(s)
    q, k, v, u, gates = _inproj(x, mod3, g1, w_in_bf, cos, sin_a, sin_b, pool_dim)
    kc, vc = _ctxkv(ctx, mod3[b], g1, w_in_bf[:, ATTN_DIM:ATTN_DIM + 2 * KV_DIM])
    attn = _attention(q, k, v, kc, vc, attn_sink[0])

    w_r_t = w_router[0].T
    w_r_hi = w_r_t.astype(BF16)
    w_r_lo = (w_r_t - w_r_hi.astype(F32)).astype(BF16)
    base, h2, eidx, wts, rank, cnt = _mixer(
        x, attn, u, gates, mod3, norm2_g[0].reshape(1, d), w_pool[0].astype(BF16),
        pool_scale[0].reshape(1, pool_dim), w_up_attn[0].astype(BF16), w_up_pool[0].astype(BF16),
        w_out[0].astype(BF16), w_r_hi, w_r_lo, router_bias[0].reshape(n_exp, 1),
        w_sh_gate[0].astype(BF16), w_sh_up[0].astype(BF16), w_sh_down[0].astype(BF16))

    br = EXPERT_BLOCK_ROWS
    n_blocks = (t_all * TOP_K + n_exp * (br - 1) + br - 1) // br
    n_slots = n_blocks * br
    pos, bexp, nused = _slots(eidx, rank, cnt, n_blocks, br)
    xs = _dispatch(h2, pos, n_slots)
    y = _experts(xs, bexp[0, :n_blocks], nused[0, :1], w_exp_gate[0], w_exp_up[0], w_exp_down[0], br)
    yg = _gather_back(y, pos)

    return _combine(yg, wts.T, base, mod3, final_g.reshape(1, d))
```

```python
import functools

import jax
import jax.numpy as jnp
from jax import lax
from jax.experimental import pallas as pl
from jax.experimental.pallas import tpu as pltpu
from jax.experimental.pallas import tpu_sc as plsc

F32 = jnp.float32
BF16 = jnp.bfloat16
I32 = jnp.int32
U32 = jnp.uint32

GRID_W = 64
HEAD_DIM = 64
N_Q_HEADS = 8
N_KV_HEADS = 2
Q_PER_KV = N_Q_HEADS // N_KV_HEADS
ATTN_DIM = N_Q_HEADS * HEAD_DIM
KV_DIM = N_KV_HEADS * HEAD_DIM
ATTN_BLOCK = 128
ATTN_SCALE = HEAD_DIM ** -0.5
ROPE_THETA = 10000.0
POOL_WINDOWS = (2, 4, 8, 16)
POOL_HALO = 8
N_EXPERT_GROUPS = 8
TOPK_GROUPS = 4
TOP_K = 8
ROUTED_SCALE = 2.5
N_ADA = 6
NORM_EPS = 1e-6
NEG_INF = -1e30
LANES = 128
SUBLANES = 8
TOKEN_TILE = 512
EXPERT_BLOCK_ROWS = 256
VMEM_LIMIT = 56 * 1024 * 1024


def _sigmoid(x):
    return 1.0 / (1.0 + jnp.exp(-x))


def _silu(x):
    return x * _sigmoid(x)


def _nt_dot(a, b):
    return lax.dot_general(a, b, (((1,), (1,)), ((), ())), preferred_element_type=F32)


def _dot(a, b):
    return jnp.dot(a, b, preferred_element_type=F32)


def _pack_rows(x):
    n = x.shape[1] // 2
    bits = lax.bitcast_convert_type(x.astype(BF16).astype(F32), U32)
    return (bits[:, :n] >> 16) | (bits[:, n:] & jnp.uint32(0xFFFF0000))


def _unpack_rows(w):
    lo = lax.bitcast_convert_type(w << 16, F32)
    hi = lax.bitcast_convert_type(w & jnp.uint32(0xFFFF0000), F32)
    return lo, hi


def _params(*sem):
    return pltpu.CompilerParams(dimension_semantics=sem, vmem_limit_bytes=VMEM_LIMIT)


def _ada_body(c_ref, w_ref, b_ref, o_ref):
    s = _silu(c_ref[...])
    o_ref[...] = jnp.dot(s, w_ref[...], preferred_element_type=F32,
                         precision=lax.Precision.HIGHEST) + b_ref[...]


def _ada(c_rows, w_ada, b_ada):
    rows, d = c_rows.shape
    n = w_ada.shape[1]
    bn = d
    return pl.pallas_call(
        _ada_body,
        grid=(n // bn,),
        in_specs=[pl.BlockSpec((rows, d), lambda j: (0, 0)),
                  pl.BlockSpec((d, bn), lambda j: (0, j)),
                  pl.BlockSpec((1, bn), lambda j: (0, j))],
        out_specs=pl.BlockSpec((rows, bn), lambda j: (0, j)),
        out_shape=jax.ShapeDtypeStruct((rows, n), F32),
        compiler_params=_params("arbitrary"),
        name="ada_mod",
    )(c_rows, w_ada, b_ada.reshape(1, n))


def _norm_mod(x, g, shift, scale):
    ms = jnp.mean(x * x, axis=-1, keepdims=True)
    y = x * lax.rsqrt(ms + NORM_EPS)
    return (y * g) * (1.0 + scale) + shift


def _rope(t, cos, sin_a, sin_b):
    return (t * cos + pltpu.roll(t, LANES - HEAD_DIM // 4, 1) * sin_a
            + pltpu.roll(t, HEAD_DIM // 4, 1) * sin_b)


def _inproj_body(x_ref, mod_ref, g_ref, w_ref, cos_ref, sa_ref, sb_ref,
                 q_ref, k_ref, v_ref, u_ref, gate_ref):
    h = _norm_mod(x_ref[...], g_ref[...], mod_ref[0:1, :], mod_ref[1:2, :]).astype(BF16)
    cos, sa, sb = cos_ref[...], sa_ref[...], sb_ref[...]
    heads_per_chunk = LANES // HEAD_DIM
    for j in range(ATTN_DIM // LANES):
        t = _rope(_dot(h, w_ref[:, j * LANES:(j + 1) * LANES]), cos, sa, sb) * ATTN_SCALE
        t = t.astype(BF16)
        for i in range(heads_per_chunk):
            q_ref[heads_per_chunk * j + i] = t[:, i * HEAD_DIM:(i + 1) * HEAD_DIM]
    k_off = ATTN_DIM
    t = _rope(_dot(h, w_ref[:, k_off:k_off + KV_DIM]), cos, sa, sb).astype(BF16)
    for i in range(N_KV_HEADS):
        k_ref[i] = t[:, i * HEAD_DIM:(i + 1) * HEAD_DIM]
    v_off = k_off + KV_DIM
    t = _dot(h, w_ref[:, v_off:v_off + KV_DIM]).astype(BF16)
    for i in range(N_KV_HEADS):
        v_ref[i] = t[:, i * HEAD_DIM:(i + 1) * HEAD_DIM]
    p_off = v_off + KV_DIM
    pool_dim = u_ref.shape[-1]
    u_ref[...] = _dot(h, w_ref[:, p_off:p_off + pool_dim])
    g_off = p_off + pool_dim
    gate_dim = gate_ref.shape[-1]
    chunk = 512
    for j in range(gate_dim // chunk):
        t = _dot(h, w_ref[:, g_off + j * chunk:g_off + (j + 1) * chunk])
        gate_ref[:, j * chunk:(j + 1) * chunk] = _sigmoid(t).astype(BF16)


def _inproj(x, mod3, g, w_in_bf, cos, sa, sb, pool_dim):
    b, s, d = x.shape
    tm = min(TOKEN_TILE, s)
    in_dim = w_in_bf.shape[1]
    gate_dim = in_dim - ATTN_DIM - 2 * KV_DIM - pool_dim
    grid = (b, s // tm)
    return pl.pallas_call(
        _inproj_body,
        grid=grid,
        in_specs=[
            pl.BlockSpec((None, tm, d), lambda bi, i: (bi, i, 0)),
            pl.BlockSpec((None, N_ADA, d), lambda bi, i: (bi, 0, 0)),
            pl.BlockSpec((1, d), lambda bi, i: (0, 0)),
            pl.BlockSpec((d, in_dim), lambda bi, i: (0, 0)),
            pl.BlockSpec((tm, LANES), lambda bi, i: (i, 0)),
            pl.BlockSpec((tm, LANES), lambda bi, i: (i, 0)),
            pl.BlockSpec((tm, LANES), lambda bi, i: (i, 0)),
        ],
        out_specs=[
            pl.BlockSpec((None, N_Q_HEADS, tm, HEAD_DIM), lambda bi, i: (bi, 0, i, 0)),
            pl.BlockSpec((None, N_KV_HEADS, tm, HEAD_DIM), lambda bi, i: (bi, 0, i, 0)),
            pl.BlockSpec((None, N_KV_HEADS, tm, HEAD_DIM), lambda bi, i: (bi, 0, i, 0)),
            pl.BlockSpec((None, tm, pool_dim), lambda bi, i: (bi, i, 0)),
            pl.BlockSpec((None, tm, gate_dim), lambda bi, i: (bi, i, 0)),
        ],
        out_shape=[
            jax.ShapeDtypeStruct((b, N_Q_HEADS, s, HEAD_DIM), BF16),
            jax.ShapeDtypeStruct((b, N_KV_HEADS, s, HEAD_DIM), BF16),
            jax.ShapeDtypeStruct((b, N_KV_HEADS, s, HEAD_DIM), BF16),
            jax.ShapeDtypeStruct((b, s, pool_dim), F32),
            jax.ShapeDtypeStruct((b, s, gate_dim), BF16),
        ],
        compiler_params=_params("parallel", "parallel"),
        name="in_proj",
    )(x, mod3, g, w_in_bf, cos, sa, sb)


def _ctxkv_body(ctx_ref, mod_ref, g_ref, w_ref, kc_ref, vc_ref):
    h = _norm_mod(ctx_ref[...], g_ref[...], mod_ref[0:1, :], mod_ref[1:2, :]).astype(BF16)
    t = _dot(h, w_ref[...]).astype(BF16)
    for i in range(N_KV_HEADS):
        kc_ref[i] = t[:, i * HEAD_DIM:(i + 1) * HEAD_DIM]
        vc_ref[i] = t[:, KV_DIM + i * HEAD_DIM:KV_DIM + (i + 1) * HEAD_DIM]


def _ctxkv(ctx, mod_c, g, w_kv_bf):
    b, c, d = ctx.shape
    out = jax.ShapeDtypeStruct((b, N_KV_HEADS, c, HEAD_DIM), BF16)
    spec = pl.BlockSpec((None, N_KV_HEADS, c, HEAD_DIM), lambda bi: (bi, 0, 0, 0))
    return pl.pallas_call(
        _ctxkv_body,
        grid=(b,),
        in_specs=[pl.BlockSpec((None, c, d), lambda bi: (bi, 0, 0)),
                  pl.BlockSpec((N_ADA, d), lambda bi: (0, 0)),
                  pl.BlockSpec((1, d), lambda bi: (0, 0)),
                  pl.BlockSpec((d, 2 * KV_DIM), lambda bi: (0, 0))],
        out_specs=[spec, spec],
        out_shape=[out, out],
        compiler_params=_params("parallel"),
        name="ctx_kv",
    )(ctx, mod_c, g, w_kv_bf)


def _attn_body(sink_ref, q_ref, kp_ref, kn_ref, kx_ref, vp_ref, vn_ref, vx_ref,
               kc_ref, vc_ref, lo_ref, hi_ref, o_ref):
    n = pl.program_id(1)
    last = pl.num_programs(1) - 1
    rows = Q_PER_KV * ATTN_BLOCK
    bias_lo = lo_ref[...] + jnp.where(n == 0, NEG_INF, 0.0)
    bias_hi = hi_ref[...] + jnp.where(n == last, NEG_INF, 0.0)
    row_head = lax.broadcasted_iota(I32, (rows, 1), 0) // ATTN_BLOCK
    outs = []
    for kk in range(N_KV_HEADS):
        qs = q_ref[kk * Q_PER_KV:(kk + 1) * Q_PER_KV].reshape(rows, HEAD_DIM)
        s0 = _nt_dot(qs, kp_ref[kk]) + bias_lo
        s1 = _nt_dot(qs, kn_ref[kk])
        s2 = _nt_dot(qs, kx_ref[kk]) + bias_hi
        sc = _nt_dot(qs, kc_ref[kk])
        sink = jnp.zeros((rows, 1), F32)
        for g in range(Q_PER_KV):
            sink = jnp.where(row_head == g, sink_ref[kk * Q_PER_KV + g], sink)
        m = jnp.maximum(
            jnp.maximum(jnp.max(s0, axis=1, keepdims=True), jnp.max(s1, axis=1, keepdims=True)),
            jnp.maximum(jnp.max(s2, axis=1, keepdims=True), jnp.max(sc, axis=1, keepdims=True)))
        m = jnp.maximum(m, sink)
        p0, p1, p2, pc = jnp.exp(s0 - m), jnp.exp(s1 - m), jnp.exp(s2 - m), jnp.exp(sc - m)
        denom = (jnp.sum(p0, axis=1, keepdims=True) + jnp.sum(p1, axis=1, keepdims=True)
                 + jnp.sum(p2, axis=1, keepdims=True) + jnp.sum(pc, axis=1, keepdims=True)
                 + jnp.exp(sink - m))
        o = (_dot(p0.astype(BF16), vp_ref[kk]) + _dot(p1.astype(BF16), vn_ref[kk])
             + _dot(p2.astype(BF16), vx_ref[kk]) + _dot(pc.astype(BF16), vc_ref[kk]))
        o = o / denom
        outs += [o[g * ATTN_BLOCK:(g + 1) * ATTN_BLOCK] for g in range(Q_PER_KV)]
    o_ref[...] = jnp.concatenate(outs, axis=1).astype(BF16)


def _attention(q, k, v, kc, vc, sink):
    b, _, s, _ = q.shape
    c = kc.shape[2]
    nb = s // ATTN_BLOCK
    rows = Q_PER_KV * ATTN_BLOCK
    qi = jnp.tile(jnp.arange(ATTN_BLOCK), Q_PER_KV)[:, None]
    kj = jnp.arange(ATTN_BLOCK)[None, :]
    bias_lo = jnp.where(kj >= qi, 0.0, NEG_INF).astype(F32)
    bias_hi = jnp.where(kj <= qi, 0.0, NEG_INF).astype(F32)
    kv_spec = lambda f: pl.BlockSpec((None, N_KV_HEADS, ATTN_BLOCK, HEAD_DIM), f)
    prev = lambda bi, n: (bi, 0, jnp.maximum(n - 1, 0), 0)
    own = lambda bi, n: (bi, 0, n, 0)
    nxt = lambda bi, n: (bi, 0, jnp.minimum(n + 1, nb - 1), 0)
    ctx_spec = pl.BlockSpec((None, N_KV_HEADS, c, HEAD_DIM), lambda bi, n: (bi, 0, 0, 0))
    bias_spec = pl.BlockSpec((rows, ATTN_BLOCK), lambda bi, n: (0, 0))
    return pl.pallas_call(
        _attn_body,
        grid=(b, nb),
        in_specs=[
            pl.BlockSpec(memory_space=pltpu.SMEM),
            pl.BlockSpec((None, N_Q_HEADS, ATTN_BLOCK, HEAD_DIM), own),
            kv_spec(prev), kv_spec(own), kv_spec(nxt),
            kv_spec(prev), kv_spec(own), kv_spec(nxt),
            ctx_spec, ctx_spec, bias_spec, bias_spec,
        ],
        out_specs=pl.BlockSpec((None, ATTN_BLOCK, ATTN_DIM), lambda bi, n: (bi, n, 0)),
        out_shape=jax.ShapeDtypeStruct((b, s, ATTN_DIM), BF16),
        compiler_params=_params("parallel", "parallel"),
        name="window_attn",
    )(sink, q, k, k, k, v, v, v, kc, vc, bias_lo, bias_hi)


def _pool_delta(ue, t_seq, seq_len):
    n_ext = ue.shape[0]
    tm = n_ext - 2 * POOL_HALO
    group_dim = ue.shape[1] // len(POOL_WINDOWS)
    outs = []
    for g, w in enumerate(POOL_WINDOWS):
        xs = ue[:, g * group_dim:(g + 1) * group_dim]
        acc = xs + pltpu.roll(xs, 1, 0)
        step = 1
        while 2 * step < w:
            acc = pltpu.roll(acc, step, 0) + pltpu.roll(acc, n_ext - step, 0)
            step *= 2
        half = w // 2
        cnt = (jnp.minimum(t_seq + half, seq_len) - jnp.maximum(t_seq - half, 0)).astype(F32)
        core = slice(POOL_HALO, POOL_HALO + tm)
        outs.append(acc[core] / cnt - xs[core])
    return outs


def _route(s, sel, n_tok):
    n_exp = s.shape[0]
    per_group = n_exp // N_EXPERT_GROUPS
    neg = float("-inf")
    iota_g = lax.broadcasted_iota(I32, (per_group, n_tok), 0).astype(F32)
    scores = []
    for g in range(N_EXPERT_GROUPS):
        blk = sel[g * per_group:(g + 1) * per_group]
        m1 = jnp.max(blk, axis=0, keepdims=True)
        first = jnp.min(jnp.where(blk == m1, iota_g, float(per_group)), axis=0, keepdims=True)
        m2 = jnp.max(jnp.where(iota_g == first, neg, blk), axis=0, keepdims=True)
        scores.append(m1 + m2)
    gs = jnp.concatenate(scores, axis=0)
    iota_ng = lax.broadcasted_iota(I32, (N_EXPERT_GROUPS, n_tok), 0).astype(F32)
    gsel = jnp.zeros((N_EXPERT_GROUPS, n_tok), F32)
    for _ in range(TOPK_GROUPS):
        m = jnp.max(gs, axis=0, keepdims=True)
        first = jnp.min(jnp.where(gs == m, iota_ng, float(N_EXPERT_GROUPS)), axis=0, keepdims=True)
        hit = iota_ng == first
        gsel = jnp.where(hit, 1.0, gsel)
        gs = jnp.where(hit, neg, gs)
    cur = jnp.concatenate(
        [jnp.where(gsel[g:g + 1] > 0.0, sel[g * per_group:(g + 1) * per_group], NEG_INF)
         for g in range(N_EXPERT_GROUPS)], axis=0)
    iota_e = lax.broadcasted_iota(I32, (n_exp, n_tok), 0).astype(F32)
    chosen = jnp.zeros((n_exp, n_tok), F32)
    ids, aff = [], []
    for _ in range(TOP_K):
        m = jnp.max(cur, axis=0, keepdims=True)
        first = jnp.min(jnp.where(cur == m, iota_e, float(n_exp)), axis=0, keepdims=True)
        hit = iota_e == first
        ids.append(first)
        aff.append(jnp.sum(jnp.where(hit, s, 0.0), axis=0, keepdims=True))
        chosen = jnp.where(hit, 1.0, chosen)
        cur = jnp.where(hit, neg, cur)
    return ids, aff, chosen


def _mixer_body(x_ref, attn_ref, up_ref, u_ref, un_ref, gate_ref, mod_ref, n2g_ref,
                wpool_ref, pscale_ref, wua_ref, wup_ref, wout_ref, wrh_ref, wrl_ref, rbias_ref,
                wsg_ref, wsu_ref, wsd_ref,
                base_ref, h2_ref, eidx_ref, wts_ref, rank_ref, cnt_ref, *, seq_len):
    bi, ti = pl.program_id(0), pl.program_id(1)
    n_t = pl.num_programs(1)
    tm, d = x_ref.shape
    pool_dim = u_ref.shape[-1]

    zero_halo = jnp.zeros((POOL_HALO, pool_dim), F32)
    u_prev = jnp.where(ti == 0, zero_halo, up_ref[...])
    u_next = jnp.where(ti == n_t - 1, zero_halo, un_ref[...])
    ue = jnp.concatenate([u_prev, u_ref[...], u_next], axis=0)
    t_seq = ti * tm + lax.broadcasted_iota(I32, (tm, 1), 0)
    deltas = _pool_delta(ue, t_seq, seq_len)
    pool = jnp.concatenate(
        [_dot(dl.astype(BF16), wpool_ref[g]) for g, dl in enumerate(deltas)], axis=1)
    pool = (pool * pscale_ref[...]).astype(BF16)

    sig_a = gate_ref[:, :d].astype(F32)
    sig_p = gate_ref[:, d:].astype(F32)
    y = sig_a * _dot(attn_ref[...], wua_ref[...]) + sig_p * _dot(pool, wup_ref[...])
    mix = _dot(y.astype(BF16), wout_ref[...])
    x1 = x_ref[...] + mod_ref[2:3, :] * mix

    h2 = _norm_mod(x1, n2g_ref[...], mod_ref[3:4, :], mod_ref[4:5, :])
    h2_hi = h2.astype(BF16)
    h2_ref[...] = _pack_rows(h2)
    shared = _dot((_silu(_dot(h2_hi, wsg_ref[...])) * _dot(h2_hi, wsu_ref[...])).astype(BF16),
                  wsd_ref[...])
    base_ref[...] = x1 + mod_ref[5:6, :] * shared

    h2_lo = (h2 - h2_hi.astype(F32)).astype(BF16)
    logits = (_nt_dot(wrh_ref[...], h2_hi) + _nt_dot(wrh_ref[...], h2_lo)
              + _nt_dot(wrl_ref[...], h2_hi))
    s = _sigmoid(logits)
    sel = s + rbias_ref[...]
    ids, aff, chosen = _route(s, sel, tm)

    total = aff[0]
    for a in aff[1:]:
        total = total + a
    wts_ref[...] = jnp.concatenate([a / total * ROUTED_SCALE for a in aff], axis=0)
    eidx_ref[...] = jnp.concatenate(ids, axis=0).astype(I32)

    @pl.when((bi == 0) & (ti == 0))
    def _():
        cnt_ref[...] = jnp.zeros_like(cnt_ref)

    before = (lax.broadcasted_iota(I32, (tm, tm), 0) < lax.broadcasted_iota(I32, (tm, tm), 1))
    prefix = _dot(chosen.astype(BF16), jnp.where(before, 1.0, 0.0).astype(BF16))
    rank_dense = prefix + cnt_ref[:, 0:1]
    n_exp = s.shape[0]
    iota_e = lax.broadcasted_iota(I32, (n_exp, tm), 0).astype(F32)
    rank_ref[...] = jnp.concatenate(
        [jnp.sum(jnp.where(iota_e == i, rank_dense, 0.0), axis=0, keepdims=True) for i in ids],
        axis=0).astype(I32)
    cnt_ref[...] = cnt_ref[...] + jnp.sum(chosen, axis=1, keepdims=True)


def _mixer(x, attn, u, gates, mod3, n2g, wpool_bf, pscale, wua_bf, wup_bf, wout_bf,
           wr_hi, wr_lo, rbias, wsg_bf, wsu_bf, wsd_bf):
    b, s, d = x.shape
    tm = min(TOKEN_TILE, s)
    n_t = s // tm
    t_all = b * s
    pool_dim = u.shape[-1]
    n_exp = wr_hi.shape[0]
    sh = wsg_bf.shape[1]
    halo_blocks = tm // POOL_HALO
    n_halo = s // POOL_HALO
    full = lambda shape: pl.BlockSpec(shape, lambda bi, i: (0,) * len(shape))
    tok = lambda bi, i: (0, bi * n_t + i)
    return pl.pallas_call(
        functools.partial(_mixer_body, seq_len=s),
        grid=(b, n_t),
        in_specs=[
            pl.BlockSpec((None, tm, d), lambda bi, i: (bi, i, 0)),
            pl.BlockSpec((None, tm, ATTN_DIM), lambda bi, i: (bi, i, 0)),
            pl.BlockSpec((None, POOL_HALO, pool_dim),
                         lambda bi, i: (bi, jnp.maximum(i * halo_blocks - 1, 0), 0)),
            pl.BlockSpec((None, tm, pool_dim), lambda bi, i: (bi, i, 0)),
            pl.BlockSpec((None, POOL_HALO, pool_dim),
                         lambda bi, i: (bi, jnp.minimum((i + 1) * halo_blocks, n_halo - 1), 0)),
            pl.BlockSpec((None, tm, 2 * d), lambda bi, i: (bi, i, 0)),
            pl.BlockSpec((None, N_ADA, d), lambda bi, i: (bi, 0, 0)),
            full((1, d)),
            full((len(POOL_WINDOWS), pool_dim // len(POOL_WINDOWS), pool_dim // len(POOL_WINDOWS))),
            full((1, pool_dim)),
            full((ATTN_DIM, d)), full((pool_dim, d)), full((d, d)),
            full((n_exp, d)), full((n_exp, d)), full((n_exp, 1)),
            full((d, sh)), full((d, sh)), full((sh, d)),
        ],
        out_specs=[
            pl.BlockSpec((None, tm, d), lambda bi, i: (bi, i, 0)),
            pl.BlockSpec((tm, d // 2), lambda bi, i: (bi * n_t + i, 0)),
            pl.BlockSpec((TOP_K, tm), tok),
            pl.BlockSpec((TOP_K, tm), tok),
            pl.BlockSpec((TOP_K, tm), tok),
            pl.BlockSpec((n_exp, LANES), lambda bi, i: (0, 0)),
        ],
        out_shape=[
            jax.ShapeDtypeStruct((b, s, d), F32),
            jax.ShapeDtypeStruct((t_all, d // 2), U32),
            jax.ShapeDtypeStruct((TOP_K, t_all), I32),
            jax.ShapeDtypeStruct((TOP_K, t_all), F32),
            jax.ShapeDtypeStruct((TOP_K, t_all), I32),
            jax.ShapeDtypeStruct((n_exp, LANES), F32),
        ],
        compiler_params=_params("arbitrary", "arbitrary"),
        name="mixer_router",
    )(x, attn, u, u, u, gates, mod3, n2g, wpool_bf, pscale, wua_bf, wup_bf, wout_bf,
      wr_hi, wr_lo, rbias, wsg_bf, wsu_bf, wsd_bf)


def _slots_body(eidx_ref, rank_ref, cnt_ref, pos_ref, bstart_ref, *, block_rows):
    n_exp = cnt_ref.shape[0]
    tm = eidx_ref.shape[1]
    cnt = cnt_ref[...]
    padded = jnp.floor((cnt + (block_rows - 1)) / block_rows) * block_rows
    hi = jnp.floor(padded / 256.0)
    lo = padded - hi * 256.0
    below = (lax.broadcasted_iota(I32, (n_exp, n_exp), 1) < lax.broadcasted_iota(I32, (n_exp, n_exp), 0))
    tri = jnp.where(below, 1.0, 0.0).astype(BF16)
    start = 256.0 * _dot(tri, hi.astype(BF16)) + _dot(tri, lo.astype(BF16))
    end = start + padded
    iota_e = lax.broadcasted_iota(I32, (n_exp, tm), 0)
    start_col = start[:, 0:1]
    rows = []
    for k in range(TOP_K):
        hit = iota_e == eidx_ref[k:k + 1, :]
        rows.append(jnp.sum(jnp.where(hit, start_col, 0.0), axis=0, keepdims=True))
    pos_ref[...] = jnp.concatenate(rows, axis=0).astype(I32) + rank_ref[...]

    @pl.when(pl.program_id(0) == 0)
    def _():
        bstart_ref[0:n_exp, :] = (start / block_rows).astype(I32)
        bstart_ref[n_exp:, :] = jnp.broadcast_to(
            (end[n_exp - 1:n_exp, :] / block_rows).astype(I32), (SUBLANES, LANES))


def _slots(eidx, rank, cnt, block_rows):
    t_all = eidx.shape[1]
    n_exp = cnt.shape[0]
    tm = min(TOKEN_TILE, t_all)
    return pl.pallas_call(
        functools.partial(_slots_body, block_rows=block_rows),
        grid=(t_all // tm,),
        in_specs=[pl.BlockSpec((TOP_K, tm), lambda i: (0, i)),
                  pl.BlockSpec((TOP_K, tm), lambda i: (0, i)),
                  pl.BlockSpec((n_exp, LANES), lambda i: (0, 0))],
        out_specs=[pl.BlockSpec((TOP_K, tm), lambda i: (0, i)),
                   pl.BlockSpec((n_exp + SUBLANES, LANES), lambda i: (0, 0))],
        out_shape=[jax.ShapeDtypeStruct((TOP_K, t_all), I32),
                   jax.ShapeDtypeStruct((n_exp + SUBLANES, LANES), I32)],
        compiler_params=_params("arbitrary"),
        name="slot_positions",
    )(eidx, rank, cnt)


def _expert_body(bstart_ref, xs_hbm, wg_ref, wu_ref, wd_ref, y_hbm,
                 xbuf, ybuf, wgu, wdn, sem_in, sem_out, *, block_rows):
    e = pl.program_id(0)
    n_exp = pl.num_programs(0)
    first, last, total = bstart_ref[e], bstart_ref[e + 1], bstart_ref[n_exp]
    f = wg_ref.shape[1]

    def rows(i):
        return pl.ds(pl.multiple_of(i * block_rows, block_rows), block_rows)

    def in_copy(i, slot):
        return pltpu.make_async_copy(xs_hbm.at[rows(i)], xbuf.at[slot], sem_in.at[slot])

    def out_copy(i, slot):
        return pltpu.make_async_copy(ybuf.at[slot], y_hbm.at[rows(i)], sem_out.at[slot])

    @pl.when((e == 0) & (total > 0))
    def _():
        in_copy(0, 0).start()

    wgu[:, :f] = wg_ref[...].astype(BF16)
    wgu[:, f:] = wu_ref[...].astype(BF16)
    wdn[...] = wd_ref[...].astype(BF16)

    def block(i, carry):
        slot = i & 1
        in_copy(i, slot).wait()

        @pl.when(i + 1 < total)
        def _():
            in_copy(i + 1, 1 - slot).start()

        lo, hi = _unpack_rows(xbuf[slot])
        x = jnp.concatenate([lo, hi], axis=1).astype(BF16)
        gu = _dot(x, wgu[...])
        y = _dot((_silu(gu[:, :f]) * gu[:, f:]).astype(BF16), wdn[...])

        @pl.when(i >= 2)
        def _():
            out_copy(i - 2, slot).wait()

        ybuf[slot] = _pack_rows(y)
        out_copy(i, slot).start()
        return carry

    lax.fori_loop(first, last, block, 0)

    @pl.when(e == n_exp - 1)
    def _():
        @pl.when(total >= 2)
        def _():
            out_copy(total - 2, total & 1).wait()

        @pl.when(total >= 1)
        def _():
            out_copy(total - 1, (total - 1) & 1).wait()


def _experts(xs, bstart, w_gate, w_up, w_down, block_rows):
    n_slots, dw = xs.shape
    n_exp, d, f = w_gate.shape
    expert = lambda e, bs: (e, 0, 0)
    grid_spec = pltpu.PrefetchScalarGridSpec(
        num_scalar_prefetch=1,
        grid=(n_exp,),
        in_specs=[
            pl.BlockSpec(memory_space=pl.ANY),
            pl.BlockSpec((None, d, f), expert),
            pl.BlockSpec((None, d, f), expert),
            pl.BlockSpec((None, f, d), expert),
        ],
        out_specs=pl.BlockSpec(memory_space=pl.ANY),
        scratch_shapes=[
            pltpu.VMEM((2, block_rows, dw), U32),
            pltpu.VMEM((2, block_rows, dw), U32),
            pltpu.VMEM((d, 2 * f), BF16),
            pltpu.VMEM((f, d), BF16),
            pltpu.SemaphoreType.DMA((2,)),
            pltpu.SemaphoreType.DMA((2,)),
        ],
    )
    return pl.pallas_call(
        functools.partial(_expert_body, block_rows=block_rows),
        grid_spec=grid_spec,
        out_shape=jax.ShapeDtypeStruct((n_slots, dw), U32),
        compiler_params=_params("arbitrary"),
        name="expert_ffn",
    )(bstart, xs, w_gate, w_up, w_down)


SC_ROWS = 128


def _sc_workers():
    info = plsc.get_sparse_core_info()
    return info.num_cores, info.num_subcores


def _dispatch(h_rows, pos, n_slots):
    t_all, dw = h_rows.shape
    n_cores, n_sub = _sc_workers()
    per_worker = t_all // (n_cores * n_sub)
    assert per_worker * n_cores * n_sub == t_all and per_worker % SC_ROWS == 0
    mesh = plsc.VectorSubcoreMesh(core_axis_name="c", subcore_axis_name="s")

    @functools.partial(
        pl.kernel, mesh=mesh,
        out_type=jax.ShapeDtypeStruct((n_slots, dw), h_rows.dtype),
        scratch_types=[pltpu.VMEM((TOP_K, SC_ROWS), I32), pltpu.VMEM((SC_ROWS, dw), h_rows.dtype),
                       pltpu.SemaphoreType.DMA],
    )
    def body(h_hbm, pos_hbm, out_hbm, idx_v, rows_v, sem):
        base = (lax.axis_index("s") * n_cores + lax.axis_index("c")) * per_worker

        @pl.loop(0, per_worker // SC_ROWS)
        def _(j):
            t0 = pl.multiple_of(base + j * SC_ROWS, SC_ROWS)
            pltpu.sync_copy(pos_hbm.at[:, pl.ds(t0, SC_ROWS)], idx_v)
            pltpu.sync_copy(h_hbm.at[pl.ds(t0, SC_ROWS)], rows_v)
            copies = [pltpu.async_copy(rows_v, out_hbm.at[idx_v.at[k]], sem) for k in range(TOP_K)]
            for cp in copies:
                cp.wait()

    return body(h_rows, pos)


def _gather_back(y_rows, pos):
    top_k, t_all = pos.shape
    dw = y_rows.shape[1]
    n_cores, n_sub = _sc_workers()
    per_worker = t_all // (n_cores * n_sub)
    assert per_worker * n_cores * n_sub == t_all and per_worker % SC_ROWS == 0
    half = SC_ROWS // 2
    mesh = plsc.VectorSubcoreMesh(core_axis_name="c", subcore_axis_name="s")

    @functools.partial(
        pl.kernel, mesh=mesh,
        out_type=jax.ShapeDtypeStruct((top_k, t_all, dw), y_rows.dtype),
        scratch_types=[pltpu.VMEM((top_k, SC_ROWS), I32),
                       pltpu.VMEM((half, dw), y_rows.dtype), pltpu.VMEM((half, dw), y_rows.dtype),
                       pltpu.SemaphoreType.DMA, pltpu.SemaphoreType.DMA, pltpu.SemaphoreType.DMA],
    )
    def body(y_hbm, pos_hbm, out_hbm, idx_v, buf_a, buf_b, sem_g, sem_a, sem_b):
        base = (lax.axis_index("s") * n_cores + lax.axis_index("c")) * per_worker
        bufs, sems = (buf_a, buf_b), (sem_a, sem_b)

        @pl.loop(0, per_worker // SC_ROWS)
        def _(j):
            t0 = pl.multiple_of(base + j * SC_ROWS, SC_ROWS)
            pltpu.sync_copy(pos_hbm.at[:, pl.ds(t0, SC_ROWS)], idx_v)
            pending = [None, None]
            for step in range(2 * top_k):
                k, h = step // 2, step % 2
                slot = step % 2
                if pending[slot] is not None:
                    pending[slot].wait()
                pltpu.async_copy(y_hbm.at[idx_v.at[k, pl.ds(h * half, half)]], bufs[slot], sem_g).wait()
                pending[slot] = pltpu.async_copy(
                    bufs[slot], out_hbm.at[k, pl.ds(t0 + h * half, half)], sems[slot])
            for p in pending:
                p.wait()

    return body(y_rows, pos)


def _combine_body(yg_ref, w_ref, base_ref, mod_ref, fg_ref, o_ref):
    acc_lo = acc_hi = None
    for k in range(TOP_K):
        lo, hi = _unpack_rows(yg_ref[k])
        wk = w_ref[:, k:k + 1]
        acc_lo = lo * wk if acc_lo is None else acc_lo + lo * wk
        acc_hi = hi * wk if acc_hi is None else acc_hi + hi * wk
    acc = jnp.concatenate([acc_lo, acc_hi], axis=1)
    x2 = base_ref[...] + mod_ref[5:6, :] * acc
    ms = jnp.mean(x2 * x2, axis=-1, keepdims=True)
    o_ref[...] = (x2 * lax.rsqrt(ms + NORM_EPS)) * fg_ref[...]


def _combine(yg, wts_t, base, mod3, final_g):
    b, s, d = base.shape
    tm = min(256, s)
    n_t = s // tm
    return pl.pallas_call(
        _combine_body,
        grid=(b, n_t),
        in_specs=[pl.BlockSpec((TOP_K, tm, d // 2), lambda bi, i: (0, bi * n_t + i, 0)),
                  pl.BlockSpec((tm, TOP_K), lambda bi, i: (bi * n_t + i, 0)),
                  pl.BlockSpec((None, tm, d), lambda bi, i: (bi, i, 0)),
                  pl.BlockSpec((None, N_ADA, d), lambda bi, i: (bi, 0, 0)),
                  pl.BlockSpec((1, d), lambda bi, i: (0, 0))],
        out_specs=pl.BlockSpec((None, tm, d), lambda bi, i: (bi, i, 0)),
        out_shape=jax.ShapeDtypeStruct((b, s, d), F32),
        compiler_params=_params("parallel", "parallel"),
        name="combine_norm",
    )(yg, wts_t, base, mod3, final_g)


def _rope_tables(seq_len):
    rows = seq_len // GRID_W
    row = jnp.repeat(jnp.arange(rows), GRID_W).astype(F32)
    col = jnp.tile(jnp.arange(GRID_W), rows).astype(F32)
    n_freq = HEAD_DIM // 4
    inv = ROPE_THETA ** (-jnp.arange(n_freq, dtype=F32) / n_freq)
    ar, ac = row[:, None] * inv, col[:, None] * inv
    zeros = jnp.zeros_like(ar)
    reps = LANES // HEAD_DIM
    cos = jnp.tile(jnp.concatenate([jnp.cos(ar), jnp.cos(ar), jnp.cos(ac), jnp.cos(ac)], 1), (1, reps))
    sin_a = jnp.tile(jnp.concatenate([-jnp.sin(ar), zeros, -jnp.sin(ac), zeros], 1), (1, reps))
    sin_b = jnp.tile(jnp.concatenate([zeros, jnp.sin(ar), zeros, jnp.sin(ac)], 1), (1, reps))
    return cos, sin_a, sin_b


def kernel(x, c, ctx, c_ctx, w_ada, b_ada, norm1_g, w_in, attn_sink, w_pool, pool_scale,
           w_up_attn, w_up_pool, w_out, norm2_g, w_router, router_bias,
           w_exp_gate, w_exp_up, w_exp_down, w_sh_gate, w_sh_up, w_sh_down, final_g):
    b, s, d = x.shape
    assert w_ada.shape[0] == 1, "single-layer block"
    assert s % ATTN_BLOCK == 0 and s % GRID_W == 0 and d % LANES == 0
    pool_dim = w_up_pool.shape[1]
    n_exp = w_router.shape[-1]
    t_all = b * s

    pad_rows = -(-(b + 1) // SUBLANES) * SUBLANES
    c_rows = jnp.concatenate([c, c_ctx[None, :], jnp.zeros((pad_rows - b - 1, d), F32)], axis=0)
    mod3 = _ada(c_rows, w_ada[0], b_ada[0]).reshape(pad_rows, N_ADA, d)

    w_in_bf = w_in[0].astype(BF16)
    g1 = norm1_g[0].reshape(1, d)
    cos, sin_a, sin_b = _rope_tables(s)
    q, k, v, u, gates = _inproj(x, mod3, g1, w_in_bf, cos, sin_a, sin_b, pool_dim)
    kc, vc = _ctxkv(ctx, mod3[b], g1, w_in_bf[:, ATTN_DIM:ATTN_DIM + 2 * KV_DIM])
    attn = _attention(q, k, v, kc, vc, attn_sink[0])

    w_r_t = w_router[0].T
    w_r_hi = w_r_t.astype(BF16)
    w_r_lo = (w_r_t - w_r_hi.astype(F32)).astype(BF16)
    base, h2, eidx, wts, rank, cnt = _mixer(
        x, attn, u, gates, mod3, norm2_g[0].reshape(1, d), w_pool[0].astype(BF16),
        pool_scale[0].reshape(1, pool_dim), w_up_attn[0].astype(BF16), w_up_pool[0].astype(BF16),
        w_out[0].astype(BF16), w_r_hi, w_r_lo, router_bias[0].reshape(n_exp, 1),
        w_sh_gate[0].astype(BF16), w_sh_up[0].astype(BF16), w_sh_down[0].astype(BF16))

    br = EXPERT_BLOCK_ROWS
    n_blocks = (t_all * TOP_K + n_exp * (br - 1) + br - 1) // br
    n_slots = n_blocks * br
    pos, bstart = _slots(eidx, rank, cnt, br)
    xs = _dispatch(h2, pos, n_slots)
    y = _experts(xs, bstart[:n_exp + 1, 0], w_exp_gate[0], w_exp_up[0], w_exp_down[0], br)
    yg = _gather_back(y, pos)

    return _combine(yg, wts.T, base, mod3, final_g.reshape(1, d))
```

```python
import functools

import jax
import jax.numpy as jnp
from jax import lax
from jax.experimental import pallas as pl
from jax.experimental.pallas import tpu as pltpu
from jax.experimental.pallas import tpu_sc as plsc

F32 = jnp.float32
BF16 = jnp.bfloat16
I32 = jnp.int32
U32 = jnp.uint32

GRID_W = 64
HEAD_DIM = 64
N_Q_HEADS = 8
N_KV_HEADS = 2
Q_PER_KV = N_Q_HEADS // N_KV_HEADS
ATTN_DIM = N_Q_HEADS * HEAD_DIM
KV_DIM = N_KV_HEADS * HEAD_DIM
ATTN_BLOCK = 128
ATTN_SCALE = HEAD_DIM ** -0.5
ROPE_THETA = 10000.0
POOL_WINDOWS = (2, 4, 8, 16)
POOL_HALO = 8
N_EXPERT_GROUPS = 8
TOPK_GROUPS = 4
TOP_K = 8
ROUTED_SCALE = 2.5
N_ADA = 6
NORM_EPS = 1e-6
NEG_INF = -1e30
LANES = 128
SUBLANES = 8
TOKEN_TILE = 512
EXPERT_BLOCK_ROWS = 256
EXPERT_GROUP = 2
EXPERT_RING = 8
VMEM_LIMIT = 56 * 1024 * 1024


def _sigmoid(x):
    return 1.0 / (1.0 + jnp.exp(-x))


def _silu(x):
    return x * _sigmoid(x)


def _nt_dot(a, b):
    return lax.dot_general(a, b, (((1,), (1,)), ((), ())), preferred_element_type=F32)


def _dot(a, b):
    return jnp.dot(a, b, preferred_element_type=F32)


def _pack_rows(x):
    n = x.shape[1] // 2
    bits = lax.bitcast_convert_type(x.astype(BF16).astype(F32), U32)
    return (bits[:, :n] >> 16) | (bits[:, n:] & jnp.uint32(0xFFFF0000))


def _unpack_rows(w):
    lo = lax.bitcast_convert_type(w << 16, F32)
    hi = lax.bitcast_convert_type(w & jnp.uint32(0xFFFF0000), F32)
    return lo, hi


def _params(*sem):
    return pltpu.CompilerParams(dimension_semantics=sem, vmem_limit_bytes=VMEM_LIMIT)


def _ada_body(c_ref, w_ref, b_ref, o_ref):
    s = _silu(c_ref[...])
    o_ref[...] = jnp.dot(s, w_ref[...], preferred_element_type=F32,
                         precision=lax.Precision.HIGHEST) + b_ref[...]


def _ada(c_rows, w_ada, b_ada):
    rows, d = c_rows.shape
    n = w_ada.shape[1]
    bn = d
    return pl.pallas_call(
        _ada_body,
        grid=(n // bn,),
        in_specs=[pl.BlockSpec((rows, d), lambda j: (0, 0)),
                  pl.BlockSpec((d, bn), lambda j: (0, j)),
                  pl.BlockSpec((1, bn), lambda j: (0, j))],
        out_specs=pl.BlockSpec((rows, bn), lambda j: (0, j)),
        out_shape=jax.ShapeDtypeStruct((rows, n), F32),
        compiler_params=_params("arbitrary"),
        name="ada_mod",
    )(c_rows, w_ada, b_ada.reshape(1, n))


def _norm_mod(x, g, shift, scale):
    ms = jnp.mean(x * x, axis=-1, keepdims=True)
    y = x * lax.rsqrt(ms + NORM_EPS)
    return (y * g) * (1.0 + scale) + shift


def _rope(t, cos, sin_a, sin_b):
    return (t * cos + pltpu.roll(t, LANES - HEAD_DIM // 4, 1) * sin_a
            + pltpu.roll(t, HEAD_DIM // 4, 1) * sin_b)


def _inproj_body(x_ref, mod_ref, g_ref, w_ref, cos_ref, sa_ref, sb_ref,
                 q_ref, k_ref, v_ref, u_ref, gate_ref):
    h = _norm_mod(x_ref[...], g_ref[...], mod_ref[0:1, :], mod_ref[1:2, :]).astype(BF16)
    cos, sa, sb = cos_ref[...], sa_ref[...], sb_ref[...]
    heads_per_chunk = LANES // HEAD_DIM
    for j in range(ATTN_DIM // LANES):
        t = _rope(_dot(h, w_ref[:, j * LANES:(j + 1) * LANES]), cos, sa, sb) * ATTN_SCALE
        t = t.astype(BF16)
        for i in range(heads_per_chunk):
            q_ref[heads_per_chunk * j + i] = t[:, i * HEAD_DIM:(i + 1) * HEAD_DIM]
    k_off = ATTN_DIM
    t = _rope(_dot(h, w_ref[:, k_off:k_off + KV_DIM]), cos, sa, sb).astype(BF16)
    for i in range(N_KV_HEADS):
        k_ref[i] = t[:, i * HEAD_DIM:(i + 1) * HEAD_DIM]
    v_off = k_off + KV_DIM
    t = _dot(h, w_ref[:, v_off:v_off + KV_DIM]).astype(BF16)
    for i in range(N_KV_HEADS):
        v_ref[i] = t[:, i * HEAD_DIM:(i + 1) * HEAD_DIM]
    p_off = v_off + KV_DIM
    pool_dim = u_ref.shape[-1]
    u_ref[...] = _dot(h, w_ref[:, p_off:p_off + pool_dim])
    g_off = p_off + pool_dim
    gate_dim = gate_ref.shape[-1]
    chunk = 512
    for j in range(gate_dim // chunk):
        t = _dot(h, w_ref[:, g_off + j * chunk:g_off + (j + 1) * chunk])
        gate_ref[:, j * chunk:(j + 1) * chunk] = _sigmoid(t).astype(BF16)


def _inproj(x, mod3, g, w_in_bf, cos, sa, sb, pool_dim):
    b, s, d = x.shape
    tm = min(TOKEN_TILE, s)
    in_dim = w_in_bf.shape[1]
    gate_dim = in_dim - ATTN_DIM - 2 * KV_DIM - pool_dim
    grid = (b, s // tm)
    return pl.pallas_call(
        _inproj_body,
        grid=grid,
        in_specs=[
            pl.BlockSpec((None, tm, d), lambda bi, i: (bi, i, 0)),
            pl.BlockSpec((None, N_ADA, d), lambda bi, i: (bi, 0, 0)),
            pl.BlockSpec((1, d), lambda bi, i: (0, 0)),
            pl.BlockSpec((d, in_dim), lambda bi, i: (0, 0)),
            pl.BlockSpec((tm, LANES), lambda bi, i: (i, 0)),
            pl.BlockSpec((tm, LANES), lambda bi, i: (i, 0)),
            pl.BlockSpec((tm, LANES), lambda bi, i: (i, 0)),
        ],
        out_specs=[
            pl.BlockSpec((None, N_Q_HEADS, tm, HEAD_DIM), lambda bi, i: (bi, 0, i, 0)),
            pl.BlockSpec((None, N_KV_HEADS, tm, HEAD_DIM), lambda bi, i: (bi, 0, i, 0)),
            pl.BlockSpec((None, N_KV_HEADS, tm, HEAD_DIM), lambda bi, i: (bi, 0, i, 0)),
            pl.BlockSpec((None, tm, pool_dim), lambda bi, i: (bi, i, 0)),
            pl.BlockSpec((None, tm, gate_dim), lambda bi, i: (bi, i, 0)),
        ],
        out_shape=[
            jax.ShapeDtypeStruct((b, N_Q_HEADS, s, HEAD_DIM), BF16),
            jax.ShapeDtypeStruct((b, N_KV_HEADS, s, HEAD_DIM), BF16),
            jax.ShapeDtypeStruct((b, N_KV_HEADS, s, HEAD_DIM), BF16),
            jax.ShapeDtypeStruct((b, s, pool_dim), F32),
            jax.ShapeDtypeStruct((b, s, gate_dim), BF16),
        ],
        compiler_params=_params("parallel", "parallel"),
        name="in_proj",
    )(x, mod3, g, w_in_bf, cos, sa, sb)


def _ctxkv_body(ctx_ref, mod_ref, g_ref, w_ref, kc_ref, vc_ref):
    h = _norm_mod(ctx_ref[...], g_ref[...], mod_ref[0:1, :], mod_ref[1:2, :]).astype(BF16)
    t = _dot(h, w_ref[...]).astype(BF16)
    for i in range(N_KV_HEADS):
        kc_ref[i] = t[:, i * HEAD_DIM:(i + 1) * HEAD_DIM]
        vc_ref[i] = t[:, KV_DIM + i * HEAD_DIM:KV_DIM + (i + 1) * HEAD_DIM]


def _ctxkv(ctx, mod_c, g, w_kv_bf):
    b, c, d = ctx.shape
    out = jax.ShapeDtypeStruct((b, N_KV_HEADS, c, HEAD_DIM), BF16)
    spec = pl.BlockSpec((None, N_KV_HEADS, c, HEAD_DIM), lambda bi: (bi, 0, 0, 0))
    return pl.pallas_call(
        _ctxkv_body,
        grid=(b,),
        in_specs=[pl.BlockSpec((None, c, d), lambda bi: (bi, 0, 0)),
                  pl.BlockSpec((N_ADA, d), lambda bi: (0, 0)),
                  pl.BlockSpec((1, d), lambda bi: (0, 0)),
                  pl.BlockSpec((d, 2 * KV_DIM), lambda bi: (0, 0))],
        out_specs=[spec, spec],
        out_shape=[out, out],
        compiler_params=_params("parallel"),
        name="ctx_kv",
    )(ctx, mod_c, g, w_kv_bf)


def _attn_body(sink_ref, q_ref, kp_ref, kn_ref, kx_ref, vp_ref, vn_ref, vx_ref,
               kc_ref, vc_ref, lo_ref, hi_ref, o_ref):
    n = pl.program_id(1)
    last = pl.num_programs(1) - 1
    rows = Q_PER_KV * ATTN_BLOCK
    bias_lo = lo_ref[...] + jnp.where(n == 0, NEG_INF, 0.0)
    bias_hi = hi_ref[...] + jnp.where(n == last, NEG_INF, 0.0)
    row_head = lax.broadcasted_iota(I32, (rows, 1), 0) // ATTN_BLOCK
    outs = []
    for kk in range(N_KV_HEADS):
        qs = q_ref[kk * Q_PER_KV:(kk + 1) * Q_PER_KV].reshape(rows, HEAD_DIM)
        s0 = _nt_dot(qs, kp_ref[kk]) + bias_lo
        s1 = _nt_dot(qs, kn_ref[kk])
        s2 = _nt_dot(qs, kx_ref[kk]) + bias_hi
        sc = _nt_dot(qs, kc_ref[kk])
        sink = jnp.zeros((rows, 1), F32)
        for g in range(Q_PER_KV):
            sink = jnp.where(row_head == g, sink_ref[kk * Q_PER_KV + g], sink)
        ctx_parts = [sc[:, j * ATTN_BLOCK:(j + 1) * ATTN_BLOCK] for j in range(sc.shape[1] // ATTN_BLOCK)]
        folded = jnp.maximum(jnp.maximum(s0, s1), s2)
        for part in ctx_parts:
            folded = jnp.maximum(folded, part)
        m = jnp.maximum(jnp.max(folded, axis=1, keepdims=True), sink)
        p0, p1, p2, pc = jnp.exp(s0 - m), jnp.exp(s1 - m), jnp.exp(s2 - m), jnp.exp(sc - m)
        folded = p0 + p1 + p2
        for j in range(len(ctx_parts)):
            folded = folded + pc[:, j * ATTN_BLOCK:(j + 1) * ATTN_BLOCK]
        denom = jnp.sum(folded, axis=1, keepdims=True) + jnp.exp(sink - m)
        o = (_dot(p0.astype(BF16), vp_ref[kk]) + _dot(p1.astype(BF16), vn_ref[kk])
             + _dot(p2.astype(BF16), vx_ref[kk]) + _dot(pc.astype(BF16), vc_ref[kk]))
        o = o / denom
        outs += [o[g * ATTN_BLOCK:(g + 1) * ATTN_BLOCK] for g in range(Q_PER_KV)]
    o_ref[...] = jnp.concatenate(outs, axis=1).astype(BF16)


def _attention(q, k, v, kc, vc, sink):
    b, _, s, _ = q.shape
    c = kc.shape[2]
    nb = s // ATTN_BLOCK
    rows = Q_PER_KV * ATTN_BLOCK
    qi = jnp.tile(jnp.arange(ATTN_BLOCK), Q_PER_KV)[:, None]
    kj = jnp.arange(ATTN_BLOCK)[None, :]
    bias_lo = jnp.where(kj >= qi, 0.0, NEG_INF).astype(F32)
    bias_hi = jnp.where(kj <= qi, 0.0, NEG_INF).astype(F32)
    kv_spec = lambda f: pl.BlockSpec((None, N_KV_HEADS, ATTN_BLOCK, HEAD_DIM), f)
    prev = lambda bi, n: (bi, 0, jnp.maximum(n - 1, 0), 0)
    own = lambda bi, n: (bi, 0, n, 0)
    nxt = lambda bi, n: (bi, 0, jnp.minimum(n + 1, nb - 1), 0)
    ctx_spec = pl.BlockSpec((None, N_KV_HEADS, c, HEAD_DIM), lambda bi, n: (bi, 0, 0, 0))
    bias_spec = pl.BlockSpec((rows, ATTN_BLOCK), lambda bi, n: (0, 0))
    return pl.pallas_call(
        _attn_body,
        grid=(b, nb),
        in_specs=[
            pl.BlockSpec(memory_space=pltpu.SMEM),
            pl.BlockSpec((None, N_Q_HEADS, ATTN_BLOCK, HEAD_DIM), own),
            kv_spec(prev), kv_spec(own), kv_spec(nxt),
            kv_spec(prev), kv_spec(own), kv_spec(nxt),
            ctx_spec, ctx_spec, bias_spec, bias_spec,
        ],
        out_specs=pl.BlockSpec((None, ATTN_BLOCK, ATTN_DIM), lambda bi, n: (bi, n, 0)),
        out_shape=jax.ShapeDtypeStruct((b, s, ATTN_DIM), BF16),
        compiler_params=_params("parallel", "parallel"),
        name="window_attn",
    )(sink, q, k, k, k, v, v, v, kc, vc, bias_lo, bias_hi)


def _pool_delta(ue, t_seq, seq_len):
    n_ext = ue.shape[0]
    tm = n_ext - 2 * POOL_HALO
    group_dim = ue.shape[1] // len(POOL_WINDOWS)
    outs = []
    for g, w in enumerate(POOL_WINDOWS):
        xs = ue[:, g * group_dim:(g + 1) * group_dim]
        acc = xs + pltpu.roll(xs, 1, 0)
        step = 1
        while 2 * step < w:
            acc = pltpu.roll(acc, step, 0) + pltpu.roll(acc, n_ext - step, 0)
            step *= 2
        half = w // 2
        cnt = (jnp.minimum(t_seq + half, seq_len) - jnp.maximum(t_seq - half, 0)).astype(F32)
        core = slice(POOL_HALO, POOL_HALO + tm)
        outs.append(acc[core] / cnt - xs[core])
    return outs


def _route(s, sel, n_tok):
    n_exp = s.shape[0]
    per_group = n_exp // N_EXPERT_GROUPS
    neg = float("-inf")
    iota_g = lax.broadcasted_iota(I32, (per_group, n_tok), 0).astype(F32)
    scores = []
    for g in range(N_EXPERT_GROUPS):
        blk = sel[g * per_group:(g + 1) * per_group]
        m1 = jnp.max(blk, axis=0, keepdims=True)
        first = jnp.min(jnp.where(blk == m1, iota_g, float(per_group)), axis=0, keepdims=True)
        m2 = jnp.max(jnp.where(iota_g == first, neg, blk), axis=0, keepdims=True)
        scores.append(m1 + m2)
    gs = jnp.concatenate(scores, axis=0)
    iota_ng = lax.broadcasted_iota(I32, (N_EXPERT_GROUPS, n_tok), 0).astype(F32)
    gsel = jnp.zeros((N_EXPERT_GROUPS, n_tok), F32)
    for _ in range(TOPK_GROUPS):
        m = jnp.max(gs, axis=0, keepdims=True)
        first = jnp.min(jnp.where(gs == m, iota_ng, float(N_EXPERT_GROUPS)), axis=0, keepdims=True)
        hit = iota_ng == first
        gsel = jnp.where(hit, 1.0, gsel)
        gs = jnp.where(hit, neg, gs)
    cur = jnp.concatenate(
        [jnp.where(gsel[g:g + 1] > 0.0, sel[g * per_group:(g + 1) * per_group], NEG_INF)
         for g in range(N_EXPERT_GROUPS)], axis=0)
    iota_e = lax.broadcasted_iota(I32, (n_exp, n_tok), 0).astype(F32)
    chosen = jnp.zeros((n_exp, n_tok), F32)
    ids, aff = [], []
    for _ in range(TOP_K):
        m = jnp.max(cur, axis=0, keepdims=True)
        first = jnp.min(jnp.where(cur == m, iota_e, float(n_exp)), axis=0, keepdims=True)
        hit = iota_e == first
        ids.append(first)
        aff.append(jnp.sum(jnp.where(hit, s, 0.0), axis=0, keepdims=True))
        chosen = jnp.where(hit, 1.0, chosen)
        cur = jnp.where(hit, neg, cur)
    return ids, aff, chosen


def _mixer_body(x_ref, attn_ref, up_ref, u_ref, un_ref, gate_ref, mod_ref, n2g_ref,
                wpool_ref, pscale_ref, wua_ref, wup_ref, wout_ref, wrh_ref, wrl_ref, rbias_ref,
                wsg_ref, wsu_ref, wsd_ref,
                base_ref, h2_ref, eidx_ref, wts_ref, rank_ref, cnt_ref, *, seq_len):
    bi, ti = pl.program_id(0), pl.program_id(1)
    n_t = pl.num_programs(1)
    tm, d = x_ref.shape
    pool_dim = u_ref.shape[-1]

    zero_halo = jnp.zeros((POOL_HALO, pool_dim), F32)
    u_prev = jnp.where(ti == 0, zero_halo, up_ref[...])
    u_next = jnp.where(ti == n_t - 1, zero_halo, un_ref[...])
    ue = jnp.concatenate([u_prev, u_ref[...], u_next], axis=0)
    t_seq = ti * tm + lax.broadcasted_iota(I32, (tm, 1), 0)
    deltas = _pool_delta(ue, t_seq, seq_len)
    pool = jnp.concatenate(
        [_dot(dl.astype(BF16), wpool_ref[g]) for g, dl in enumerate(deltas)], axis=1)
    pool = (pool * pscale_ref[...]).astype(BF16)

    sig_a = gate_ref[:, :d].astype(F32)
    sig_p = gate_ref[:, d:].astype(F32)
    y = sig_a * _dot(attn_ref[...], wua_ref[...]) + sig_p * _dot(pool, wup_ref[...])
    mix = _dot(y.astype(BF16), wout_ref[...])
    x1 = x_ref[...] + mod_ref[2:3, :] * mix

    h2 = _norm_mod(x1, n2g_ref[...], mod_ref[3:4, :], mod_ref[4:5, :])
    h2_hi = h2.astype(BF16)
    h2_ref[...] = _pack_rows(h2)
    shared = _dot((_silu(_dot(h2_hi, wsg_ref[...])) * _dot(h2_hi, wsu_ref[...])).astype(BF16),
                  wsd_ref[...])
    base_ref[...] = x1 + mod_ref[5:6, :] * shared

    h2_lo = (h2 - h2_hi.astype(F32)).astype(BF16)
    logits = (_nt_dot(wrh_ref[...], h2_hi) + _nt_dot(wrh_ref[...], h2_lo)
              + _nt_dot(wrl_ref[...], h2_hi))
    s = _sigmoid(logits)
    sel = s + rbias_ref[...]
    ids, aff, chosen = _route(s, sel, tm)

    total = aff[0]
    for a in aff[1:]:
        total = total + a
    wts_ref[...] = jnp.concatenate([a / total * ROUTED_SCALE for a in aff], axis=0)
    eidx_ref[...] = jnp.concatenate(ids, axis=0).astype(I32)

    @pl.when((bi == 0) & (ti == 0))
    def _():
        cnt_ref[...] = jnp.zeros_like(cnt_ref)

    before = (lax.broadcasted_iota(I32, (tm, tm), 0) < lax.broadcasted_iota(I32, (tm, tm), 1))
    prefix = _dot(chosen.astype(BF16), jnp.where(before, 1.0, 0.0).astype(BF16))
    rank_dense = prefix + cnt_ref[:, 0:1]
    n_exp = s.shape[0]
    iota_e = lax.broadcasted_iota(I32, (n_exp, tm), 0).astype(F32)
    rank_ref[...] = jnp.concatenate(
        [jnp.sum(jnp.where(iota_e == i, rank_dense, 0.0), axis=0, keepdims=True) for i in ids],
        axis=0).astype(I32)
    cnt_ref[...] = cnt_ref[...] + jnp.sum(chosen, axis=1, keepdims=True)


def _mixer(x, attn, u, gates, mod3, n2g, wpool_bf, pscale, wua_bf, wup_bf, wout_bf,
           wr_hi, wr_lo, rbias, wsg_bf, wsu_bf, wsd_bf):
    b, s, d = x.shape
    tm = min(TOKEN_TILE, s)
    n_t = s // tm
    t_all = b * s
    pool_dim = u.shape[-1]
    n_exp = wr_hi.shape[0]
    sh = wsg_bf.shape[1]
    halo_blocks = tm // POOL_HALO
    n_halo = s // POOL_HALO
    full = lambda shape: pl.BlockSpec(shape, lambda bi, i: (0,) * len(shape))
    tok = lambda bi, i: (0, bi * n_t + i)
    return pl.pallas_call(
        functools.partial(_mixer_body, seq_len=s),
        grid=(b, n_t),
        in_specs=[
            pl.BlockSpec((None, tm, d), lambda bi, i: (bi, i, 0)),
            pl.BlockSpec((None, tm, ATTN_DIM), lambda bi, i: (bi, i, 0)),
            pl.BlockSpec((None, POOL_HALO, pool_dim),
                         lambda bi, i: (bi, jnp.maximum(i * halo_blocks - 1, 0), 0)),
            pl.BlockSpec((None, tm, pool_dim), lambda bi, i: (bi, i, 0)),
            pl.BlockSpec((None, POOL_HALO, pool_dim),
                         lambda bi, i: (bi, jnp.minimum((i + 1) * halo_blocks, n_halo - 1), 0)),
            pl.BlockSpec((None, tm, 2 * d), lambda bi, i: (bi, i, 0)),
            pl.BlockSpec((None, N_ADA, d), lambda bi, i: (bi, 0, 0)),
            full((1, d)),
            full((len(POOL_WINDOWS), pool_dim // len(POOL_WINDOWS), pool_dim // len(POOL_WINDOWS))),
            full((1, pool_dim)),
            full((ATTN_DIM, d)), full((pool_dim, d)), full((d, d)),
            full((n_exp, d)), full((n_exp, d)), full((n_exp, 1)),
            full((d, sh)), full((d, sh)), full((sh, d)),
        ],
        out_specs=[
            pl.BlockSpec((None, tm, d), lambda bi, i: (bi, i, 0)),
            pl.BlockSpec((tm, d // 2), lambda bi, i: (bi * n_t + i, 0)),
            pl.BlockSpec((TOP_K, tm), tok),
            pl.BlockSpec((TOP_K, tm), tok),
            pl.BlockSpec((TOP_K, tm), tok),
            pl.BlockSpec((n_exp, LANES), lambda bi, i: (0, 0)),
        ],
        out_shape=[
            jax.ShapeDtypeStruct((b, s, d), F32),
            jax.ShapeDtypeStruct((t_all, d // 2), U32),
            jax.ShapeDtypeStruct((TOP_K, t_all), I32),
            jax.ShapeDtypeStruct((TOP_K, t_all), F32),
            jax.ShapeDtypeStruct((TOP_K, t_all), I32),
            jax.ShapeDtypeStruct((n_exp, LANES), F32),
        ],
        compiler_params=_params("arbitrary", "arbitrary"),
        name="mixer_router",
    )(x, attn, u, u, u, gates, mod3, n2g, wpool_bf, pscale, wua_bf, wup_bf, wout_bf,
      wr_hi, wr_lo, rbias, wsg_bf, wsu_bf, wsd_bf)


def _slots_body(eidx_ref, rank_ref, cnt_ref, pos_ref, bstart_ref, *, block_rows):
    n_exp = cnt_ref.shape[0]
    tm = eidx_ref.shape[1]
    cnt = cnt_ref[...]
    padded = jnp.floor((cnt + (block_rows - 1)) / block_rows) * block_rows
    hi = jnp.floor(padded / 256.0)
    lo = padded - hi * 256.0
    below = (lax.broadcasted_iota(I32, (n_exp, n_exp), 1) < lax.broadcasted_iota(I32, (n_exp, n_exp), 0))
    tri = jnp.where(below, 1.0, 0.0).astype(BF16)
    start = 256.0 * _dot(tri, hi.astype(BF16)) + _dot(tri, lo.astype(BF16))
    end = start + padded
    iota_e = lax.broadcasted_iota(I32, (n_exp, tm), 0)
    start_col = start[:, 0:1]
    rows = []
    for k in range(TOP_K):
        hit = iota_e == eidx_ref[k:k + 1, :]
        rows.append(jnp.sum(jnp.where(hit, start_col, 0.0), axis=0, keepdims=True))
    pos_ref[...] = jnp.concatenate(rows, axis=0).astype(I32) + rank_ref[...]

    @pl.when(pl.program_id(0) == 0)
    def _():
        bstart_ref[0:n_exp, :] = (start / block_rows).astype(I32)
        bstart_ref[n_exp:, :] = jnp.broadcast_to(
            (end[n_exp - 1:n_exp, :] / block_rows).astype(I32), (SUBLANES, LANES))


def _slots(eidx, rank, cnt, block_rows):
    t_all = eidx.shape[1]
    n_exp = cnt.shape[0]
    tm = min(TOKEN_TILE, t_all)
    return pl.pallas_call(
        functools.partial(_slots_body, block_rows=block_rows),
        grid=(t_all // tm,),
        in_specs=[pl.BlockSpec((TOP_K, tm), lambda i: (0, i)),
                  pl.BlockSpec((TOP_K, tm), lambda i: (0, i)),
                  pl.BlockSpec((n_exp, LANES), lambda i: (0, 0))],
        out_specs=[pl.BlockSpec((TOP_K, tm), lambda i: (0, i)),
                   pl.BlockSpec((n_exp + SUBLANES, LANES), lambda i: (0, 0))],
        out_shape=[jax.ShapeDtypeStruct((TOP_K, t_all), I32),
                   jax.ShapeDtypeStruct((n_exp + SUBLANES, LANES), I32)],
        compiler_params=_params("arbitrary"),
        name="slot_positions",
    )(eidx, rank, cnt)


def _expert_body(bstart_ref, xs_hbm, wg_ref, wu_ref, wd_ref, y_hbm,
                 xbuf, ybuf, wgu, wdn, sem_in, sem_out, *, block_rows):
    e = pl.program_id(0)
    n_exp = pl.num_programs(0)
    first, last, total = bstart_ref[e], bstart_ref[e + 1], bstart_ref[n_exp]
    f = wg_ref.shape[1]
    ahead = EXPERT_RING - EXPERT_GROUP

    def rows(i):
        return pl.ds(pl.multiple_of(i * block_rows, block_rows), block_rows)

    def slot_of(i):
        return i & (EXPERT_RING - 1)

    def in_copy(i):
        return pltpu.make_async_copy(xs_hbm.at[rows(i)], xbuf.at[slot_of(i)], sem_in.at[slot_of(i)])

    def out_copy(i):
        return pltpu.make_async_copy(ybuf.at[slot_of(i)], y_hbm.at[rows(i)], sem_out.at[slot_of(i)])

    def start_in(i):
        @pl.when(i < total)
        def _():
            in_copy(i).start()

    def wait_out(i):
        @pl.when(i >= 0)
        def _():
            out_copy(i).wait()

    @pl.when(e == 0)
    def _():
        for j in range(ahead):
            start_in(jnp.int32(j))

    wgu[:, :f] = wg_ref[...].astype(BF16)
    wgu[:, f:] = wu_ref[...].astype(BF16)
    wdn[...] = wd_ref[...].astype(BF16)

    def ffn(i):
        lo, hi = _unpack_rows(xbuf[slot_of(i)])
        x = jnp.concatenate([lo, hi], axis=1).astype(BF16)
        gu = _dot(x, wgu[...])
        return _pack_rows(_dot((_silu(gu[:, :f]) * gu[:, f:]).astype(BF16), wdn[...]))

    def run_blocks(i, n):
        for j in range(n):
            in_copy(i + j).wait()
        for j in range(n):
            start_in(i + ahead + j)
        res = [ffn(i + j) for j in range(n)]
        for j in range(n):
            wait_out(i + j - EXPERT_RING)
        for j in range(n):
            ybuf[slot_of(i + j)] = res[j]
        for j in range(n):
            out_copy(i + j).start()

    n_blk = last - first
    n_groups = n_blk // EXPERT_GROUP

    def group(p, carry):
        run_blocks(first + EXPERT_GROUP * p, EXPERT_GROUP)
        return carry

    lax.fori_loop(0, n_groups, group, 0)

    @pl.when(n_blk - n_groups * EXPERT_GROUP == 1)
    def _():
        run_blocks(last - 1, 1)

    @pl.when(e == n_exp - 1)
    def _():
        for j in range(EXPERT_RING, 0, -1):
            wait_out(total - j)


def _experts(xs, bstart, w_gate, w_up, w_down, block_rows):
    n_slots, dw = xs.shape
    n_exp, d, f = w_gate.shape
    expert = lambda e, bs: (e, 0, 0)
    grid_spec = pltpu.PrefetchScalarGridSpec(
        num_scalar_prefetch=1,
        grid=(n_exp,),
        in_specs=[
            pl.BlockSpec(memory_space=pl.ANY),
            pl.BlockSpec((None, d, f), expert),
            pl.BlockSpec((None, d, f), expert),
            pl.BlockSpec((None, f, d), expert),
        ],
        out_specs=pl.BlockSpec(memory_space=pl.ANY),
        scratch_shapes=[
            pltpu.VMEM((EXPERT_RING, block_rows, dw), U32),
            pltpu.VMEM((EXPERT_RING, block_rows, dw), U32),
            pltpu.VMEM((d, 2 * f), BF16),
            pltpu.VMEM((f, d), BF16),
            pltpu.SemaphoreType.DMA((EXPERT_RING,)),
            pltpu.SemaphoreType.DMA((EXPERT_RING,)),
        ],
    )
    return pl.pallas_call(
        functools.partial(_expert_body, block_rows=block_rows),
        grid_spec=grid_spec,
        out_shape=jax.ShapeDtypeStruct((n_slots, dw), U32),
        compiler_params=_params("arbitrary"),
        name="expert_ffn",
    )(bstart, xs, w_gate, w_up, w_down)


SC_ROWS = 128


def _sc_workers():
    info = plsc.get_sparse_core_info()
    return info.num_cores, info.num_subcores


def _dispatch(h_rows, pos, n_slots):
    t_all, dw = h_rows.shape
    n_cores, n_sub = _sc_workers()
    per_worker = t_all // (n_cores * n_sub)
    assert per_worker * n_cores * n_sub == t_all and per_worker % SC_ROWS == 0
    mesh = plsc.VectorSubcoreMesh(core_axis_name="c", subcore_axis_name="s")

    @functools.partial(
        pl.kernel, mesh=mesh,
        out_type=jax.ShapeDtypeStruct((n_slots, dw), h_rows.dtype),
        scratch_types=[pltpu.VMEM((TOP_K, SC_ROWS), I32), pltpu.VMEM((SC_ROWS, dw), h_rows.dtype),
                       pltpu.SemaphoreType.DMA],
    )
    def body(h_hbm, pos_hbm, out_hbm, idx_v, rows_v, sem):
        base = (lax.axis_index("s") * n_cores + lax.axis_index("c")) * per_worker

        @pl.loop(0, per_worker // SC_ROWS)
        def _(j):
            t0 = pl.multiple_of(base + j * SC_ROWS, SC_ROWS)
            pltpu.sync_copy(pos_hbm.at[:, pl.ds(t0, SC_ROWS)], idx_v)
            pltpu.sync_copy(h_hbm.at[pl.ds(t0, SC_ROWS)], rows_v)
            copies = [pltpu.async_copy(rows_v, out_hbm.at[idx_v.at[k]], sem) for k in range(TOP_K)]
            for cp in copies:
                cp.wait()

    return body(h_rows, pos)


def _gather_back(y_rows, pos):
    top_k, t_all = pos.shape
    dw = y_rows.shape[1]
    n_cores, n_sub = _sc_workers()
    per_worker = t_all // (n_cores * n_sub)
    assert per_worker * n_cores * n_sub == t_all and per_worker % SC_ROWS == 0
    half = SC_ROWS // 2
    mesh = plsc.VectorSubcoreMesh(core_axis_name="c", subcore_axis_name="s")

    @functools.partial(
        pl.kernel, mesh=mesh,
        out_type=jax.ShapeDtypeStruct((top_k, t_all, dw), y_rows.dtype),
        scratch_types=[pltpu.VMEM((top_k, SC_ROWS), I32),
                       pltpu.VMEM((half, dw), y_rows.dtype), pltpu.VMEM((half, dw), y_rows.dtype),
                       pltpu.SemaphoreType.DMA, pltpu.SemaphoreType.DMA, pltpu.SemaphoreType.DMA],
    )
    def body(y_hbm, pos_hbm, out_hbm, idx_v, buf_a, buf_b, sem_g, sem_a, sem_b):
        base = (lax.axis_index("s") * n_cores + lax.axis_index("c")) * per_worker
        bufs, sems = (buf_a, buf_b), (sem_a, sem_b)

        @pl.loop(0, per_worker // SC_ROWS)
        def _(j):
            t0 = pl.multiple_of(base + j * SC_ROWS, SC_ROWS)
            pltpu.sync_copy(pos_hbm.at[:, pl.ds(t0, SC_ROWS)], idx_v)
            pending = [None, None]
            for step in range(2 * top_k):
                k, h = step // 2, step % 2
                slot = step % 2
                if pending[slot] is not None:
                    pending[slot].wait()
                pltpu.async_copy(y_hbm.at[idx_v.at[k, pl.ds(h * half, half)]], bufs[slot], sem_g).wait()
                pending[slot] = pltpu.async_copy(
                    bufs[slot], out_hbm.at[k, pl.ds(t0 + h * half, half)], sems[slot])
            for p in pending:
                p.wait()

    return body(y_rows, pos)


def _combine_body(yg_ref, w_ref, base_ref, mod_ref, fg_ref, o_ref):
    acc_lo = acc_hi = None
    for k in range(TOP_K):
        lo, hi = _unpack_rows(yg_ref[k])
        wk = w_ref[:, k:k + 1]
        acc_lo = lo * wk if acc_lo is None else acc_lo + lo * wk
        acc_hi = hi * wk if acc_hi is None else acc_hi + hi * wk
    acc = jnp.concatenate([acc_lo, acc_hi], axis=1)
    x2 = base_ref[...] + mod_ref[5:6, :] * acc
    ms = jnp.mean(x2 * x2, axis=-1, keepdims=True)
    o_ref[...] = (x2 * lax.rsqrt(ms + NORM_EPS)) * fg_ref[...]


def _combine(yg, wts_t, base, mod3, final_g):
    b, s, d = base.shape
    tm = min(256, s)
    n_t = s // tm
    return pl.pallas_call(
        _combine_body,
        grid=(b, n_t),
        in_specs=[pl.BlockSpec((TOP_K, tm, d // 2), lambda bi, i: (0, bi * n_t + i, 0)),
                  pl.BlockSpec((tm, TOP_K), lambda bi, i: (bi * n_t + i, 0)),
                  pl.BlockSpec((None, tm, d), lambda bi, i: (bi, i, 0)),
                  pl.BlockSpec((None, N_ADA, d), lambda bi, i: (bi, 0, 0)),
                  pl.BlockSpec((1, d), lambda bi, i: (0, 0))],
        out_specs=pl.BlockSpec((None, tm, d), lambda bi, i: (bi, i, 0)),
        out_shape=jax.ShapeDtypeStruct((b, s, d), F32),
        compiler_params=_params("parallel", "parallel"),
        name="combine_norm",
    )(yg, wts_t, base, mod3, final_g)


def _rope_tables(seq_len):
    rows = seq_len // GRID_W
    row = jnp.repeat(jnp.arange(rows), GRID_W).astype(F32)
    col = jnp.tile(jnp.arange(GRID_W), rows).astype(F32)
    n_freq = HEAD_DIM // 4
    inv = ROPE_THETA ** (-jnp.arange(n_freq, dtype=F32) / n_freq)
    ar, ac = row[:, None] * inv, col[:, None] * inv
    zeros = jnp.zeros_like(ar)
    reps = LANES // HEAD_DIM
    cos = jnp.tile(jnp.concatenate([jnp.cos(ar), jnp.cos(ar), jnp.cos(ac), jnp.cos(ac)], 1), (1, reps))
    sin_a = jnp.tile(jnp.concatenate([-jnp.sin(ar), zeros, -jnp.sin(ac), zeros], 1), (1, reps))
    sin_b = jnp.tile(jnp.concatenate([zeros, jnp.sin(ar), zeros, jnp.sin(ac)], 1), (1, reps))
    return cos, sin_a, sin_b


def kernel(x, c, ctx, c_ctx, w_ada, b_ada, norm1_g, w_in, attn_sink, w_pool, pool_scale,
           w_up_attn, w_up_pool, w_out, norm2_g, w_router, router_bias,
           w_exp_gate, w_exp_up, w_exp_down, w_sh_gate, w_sh_up, w_sh_down, final_g):
    b, s, d = x.shape
    assert w_ada.shape[0] == 1, "single-layer block"
    assert s % ATTN_BLOCK == 0 and s % GRID_W == 0 and d % LANES == 0
    pool_dim = w_up_pool.shape[1]
    n_exp = w_router.shape[-1]
    t_all = b * s

    pad_rows = -(-(b + 1) // SUBLANES) * SUBLANES
    c_rows = jnp.concatenate([c, c_ctx[None, :], jnp.zeros((pad_rows - b - 1, d), F32)], axis=0)
    mod3 = _ada(c_rows, w_ada[0], b_ada[0]).reshape(pad_rows, N_ADA, d)

    w_in_bf = w_in[0].astype(BF16)
    g1 = norm1_g[0].reshape(1, d)
    cos, sin_a, sin_b = _rope_tables(s)
    q, k, v, u, gates = _inproj(x, mod3, g1, w_in_bf, cos, sin_a, sin_b, pool_dim)
    kc, vc = _ctxkv(ctx, mod3[b], g1, w_in_bf[:, ATTN_DIM:ATTN_DIM + 2 * KV_DIM])
    attn = _attention(q, k, v, kc, vc, attn_sink[0])

    w_r_t = w_router[0].T
    w_r_hi = w_r_t.astype(BF16)
    w_r_lo = (w_r_t - w_r_hi.astype(F32)).astype(BF16)
    base, h2, eidx, wts, rank, cnt = _mixer(
        x, attn, u, gates, mod3, norm2_g[0].reshape(1, d), w_pool[0].astype(BF16),
        pool_scale[0].reshape(1, pool_dim), w_up_attn[0].astype(BF16), w_up_pool[0].astype(BF16),
        w_out[0].astype(BF16), w_r_hi, w_r_lo, router_bias[0].reshape(n_exp, 1),
        w_sh_gate[0].astype(BF16), w_sh_up[0].astype(BF16), w_sh_down[0].astype(BF16))

    br = EXPERT_BLOCK_ROWS
    n_blocks = (t_all * TOP_K + n_exp * (br - 1) + br - 1) // br
    n_slots = n_blocks * br
    pos, bstart = _slots(eidx, rank, cnt, br)
    xs = _dispatch(h2, pos, n_slots)
    y = _experts(xs, bstart[:n_exp + 1, 0], w_exp_gate[0], w_exp_up[0], w_exp_down[0], br)
    yg = _gather_back(y, pos)

    return _combine(yg, wts.T, base, mod3, final_g.reshape(1, d))
```

```python
import functools

import jax
import jax.numpy as jnp
from jax import lax
from jax.experimental import pallas as pl
from jax.experimental.pallas import tpu as pltpu
from jax.experimental.pallas import tpu_sc as plsc

F32 = jnp.float32
BF16 = jnp.bfloat16
I32 = jnp.int32
U32 = jnp.uint32

GRID_W = 64
HEAD_DIM = 64
N_Q_HEADS = 8
N_KV_HEADS = 2
Q_PER_KV = N_Q_HEADS // N_KV_HEADS
ATTN_DIM = N_Q_HEADS * HEAD_DIM
KV_DIM = N_KV_HEADS * HEAD_DIM
ATTN_BLOCK = 128
ATTN_SCALE = HEAD_DIM ** -0.5
ROPE_THETA = 10000.0
POOL_WINDOWS = (2, 4, 8, 16)
POOL_HALO = 8
N_EXPERT_GROUPS = 8
TOPK_GROUPS = 4
TOP_K = 8
ROUTED_SCALE = 2.5
N_ADA = 6
NORM_EPS = 1e-6
NEG_INF = -1e30
LANES = 128
SUBLANES = 8
ATTN_QBLOCKS = 4
TOKEN_TILE = 512
EXPERT_BLOCK_ROWS = 256
EXPERT_GROUP = 2
EXPERT_RING = 8
VMEM_LIMIT = 56 * 1024 * 1024


def _sigmoid(x):
    return 1.0 / (1.0 + jnp.exp(-x))


def _silu(x):
    return x * _sigmoid(x)


def _nt_dot(a, b):
    return lax.dot_general(a, b, (((1,), (1,)), ((), ())), preferred_element_type=F32)


def _dot(a, b):
    return jnp.dot(a, b, preferred_element_type=F32)


def _pack_rows(x):
    n = x.shape[1] // 2
    bits = lax.bitcast_convert_type(x.astype(BF16).astype(F32), U32)
    return (bits[:, :n] >> 16) | (bits[:, n:] & jnp.uint32(0xFFFF0000))


def _unpack_rows(w):
    lo = lax.bitcast_convert_type(w << 16, F32)
    hi = lax.bitcast_convert_type(w & jnp.uint32(0xFFFF0000), F32)
    return lo, hi


def _params(*sem):
    return pltpu.CompilerParams(dimension_semantics=sem, vmem_limit_bytes=VMEM_LIMIT)


def _ada_body(c_ref, w_ref, b_ref, o_ref):
    s = _silu(c_ref[...])
    o_ref[...] = jnp.dot(s, w_ref[...], preferred_element_type=F32,
                         precision=lax.Precision.HIGHEST) + b_ref[...]


def _ada(c_rows, w_ada, b_ada):
    rows, d = c_rows.shape
    n = w_ada.shape[1]
    bn = d
    return pl.pallas_call(
        _ada_body,
        grid=(n // bn,),
        in_specs=[pl.BlockSpec((rows, d), lambda j: (0, 0)),
                  pl.BlockSpec((d, bn), lambda j: (0, j)),
                  pl.BlockSpec((1, bn), lambda j: (0, j))],
        out_specs=pl.BlockSpec((rows, bn), lambda j: (0, j)),
        out_shape=jax.ShapeDtypeStruct((rows, n), F32),
        compiler_params=_params("arbitrary"),
        name="ada_mod",
    )(c_rows, w_ada, b_ada.reshape(1, n))


def _norm_mod(x, g, shift, scale):
    ms = jnp.mean(x * x, axis=-1, keepdims=True)
    y = x * lax.rsqrt(ms + NORM_EPS)
    return (y * g) * (1.0 + scale) + shift


def _rope(t, cos, sin_a, sin_b):
    return (t * cos + pltpu.roll(t, LANES - HEAD_DIM // 4, 1) * sin_a
            + pltpu.roll(t, HEAD_DIM // 4, 1) * sin_b)


def _inproj_body(x_ref, mod_ref, g_ref, w_ref, cos_ref, sa_ref, sb_ref,
                 q_ref, k_ref, v_ref, u_ref, gate_ref):
    h = _norm_mod(x_ref[...], g_ref[...], mod_ref[0:1, :], mod_ref[1:2, :]).astype(BF16)
    cos, sa, sb = cos_ref[...], sa_ref[...], sb_ref[...]
    heads_per_chunk = LANES // HEAD_DIM
    for j in range(ATTN_DIM // LANES):
        t = _rope(_dot(h, w_ref[:, j * LANES:(j + 1) * LANES]), cos, sa, sb) * ATTN_SCALE
        t = t.astype(BF16)
        for i in range(heads_per_chunk):
            q_ref[heads_per_chunk * j + i] = t[:, i * HEAD_DIM:(i + 1) * HEAD_DIM]
    k_off = ATTN_DIM
    t = _rope(_dot(h, w_ref[:, k_off:k_off + KV_DIM]), cos, sa, sb).astype(BF16)
    for i in range(N_KV_HEADS):
        k_ref[i] = t[:, i * HEAD_DIM:(i + 1) * HEAD_DIM]
    v_off = k_off + KV_DIM
    t = _dot(h, w_ref[:, v_off:v_off + KV_DIM]).astype(BF16)
    for i in range(N_KV_HEADS):
        v_ref[i] = t[:, i * HEAD_DIM:(i + 1) * HEAD_DIM]
    p_off = v_off + KV_DIM
    pool_dim = u_ref.shape[-1]
    u_ref[...] = _dot(h, w_ref[:, p_off:p_off + pool_dim])
    g_off = p_off + pool_dim
    gate_dim = gate_ref.shape[-1]
    chunk = 512
    for j in range(gate_dim // chunk):
        t = _dot(h, w_ref[:, g_off + j * chunk:g_off + (j + 1) * chunk])
        gate_ref[:, j * chunk:(j + 1) * chunk] = _sigmoid(t).astype(BF16)


def _inproj(x, mod3, g, w_in_bf, cos, sa, sb, pool_dim):
    b, s, d = x.shape
    tm = min(TOKEN_TILE, s)
    in_dim = w_in_bf.shape[1]
    gate_dim = in_dim - ATTN_DIM - 2 * KV_DIM - pool_dim
    grid = (b, s // tm)
    return pl.pallas_call(
        _inproj_body,
        grid=grid,
        in_specs=[
            pl.BlockSpec((None, tm, d), lambda bi, i: (bi, i, 0)),
            pl.BlockSpec((None, N_ADA, d), lambda bi, i: (bi, 0, 0)),
            pl.BlockSpec((1, d), lambda bi, i: (0, 0)),
            pl.BlockSpec((d, in_dim), lambda bi, i: (0, 0)),
            pl.BlockSpec((tm, LANES), lambda bi, i: (i, 0)),
            pl.BlockSpec((tm, LANES), lambda bi, i: (i, 0)),
            pl.BlockSpec((tm, LANES), lambda bi, i: (i, 0)),
        ],
        out_specs=[
            pl.BlockSpec((None, N_Q_HEADS, tm, HEAD_DIM), lambda bi, i: (bi, 0, i, 0)),
            pl.BlockSpec((None, N_KV_HEADS, tm, HEAD_DIM), lambda bi, i: (bi, 0, i, 0)),
            pl.BlockSpec((None, N_KV_HEADS, tm, HEAD_DIM), lambda bi, i: (bi, 0, i, 0)),
            pl.BlockSpec((None, tm, pool_dim), lambda bi, i: (bi, i, 0)),
            pl.BlockSpec((None, tm, gate_dim), lambda bi, i: (bi, i, 0)),
        ],
        out_shape=[
            jax.ShapeDtypeStruct((b, N_Q_HEADS, s, HEAD_DIM), BF16),
            jax.ShapeDtypeStruct((b, N_KV_HEADS, s, HEAD_DIM), BF16),
            jax.ShapeDtypeStruct((b, N_KV_HEADS, s, HEAD_DIM), BF16),
            jax.ShapeDtypeStruct((b, s, pool_dim), F32),
            jax.ShapeDtypeStruct((b, s, gate_dim), BF16),
        ],
        compiler_params=_params("parallel", "parallel"),
        name="in_proj",
    )(x, mod3, g, w_in_bf, cos, sa, sb)


def _ctxkv_body(ctx_ref, mod_ref, g_ref, w_ref, kc_ref, vc_ref):
    h = _norm_mod(ctx_ref[...], g_ref[...], mod_ref[0:1, :], mod_ref[1:2, :]).astype(BF16)
    t = _dot(h, w_ref[...]).astype(BF16)
    for i in range(N_KV_HEADS):
        kc_ref[i] = t[:, i * HEAD_DIM:(i + 1) * HEAD_DIM]
        vc_ref[i] = t[:, KV_DIM + i * HEAD_DIM:KV_DIM + (i + 1) * HEAD_DIM]


def _ctxkv(ctx, mod_c, g, w_kv_bf):
    b, c, d = ctx.shape
    out = jax.ShapeDtypeStruct((b, N_KV_HEADS, c, HEAD_DIM), BF16)
    spec = pl.BlockSpec((None, N_KV_HEADS, c, HEAD_DIM), lambda bi: (bi, 0, 0, 0))
    return pl.pallas_call(
        _ctxkv_body,
        grid=(b,),
        in_specs=[pl.BlockSpec((None, c, d), lambda bi: (bi, 0, 0)),
                  pl.BlockSpec((N_ADA, d), lambda bi: (0, 0)),
                  pl.BlockSpec((1, d), lambda bi: (0, 0)),
                  pl.BlockSpec((d, 2 * KV_DIM), lambda bi: (0, 0))],
        out_specs=[spec, spec],
        out_shape=[out, out],
        compiler_params=_params("parallel"),
        name="ctx_kv",
    )(ctx, mod_c, g, w_kv_bf)


def _attn_body(sink_ref, q_ref, *refs):
    nkb = ATTN_QBLOCKS + 2
    k_refs, v_refs = refs[:nkb], refs[nkb:2 * nkb]
    kc_ref, vc_ref, lo_ref, hi_ref, o_ref = refs[2 * nkb:]
    n = pl.program_id(1)
    last = pl.num_programs(1) - 1
    rows = Q_PER_KV * ATTN_BLOCK
    row_head = lax.broadcasted_iota(I32, (rows, 1), 0) // ATTN_BLOCK
    for qb in range(ATTN_QBLOCKS):
        bias_lo, bias_hi = lo_ref[...], hi_ref[...]
        if qb == 0:
            bias_lo = bias_lo + jnp.where(n == 0, NEG_INF, 0.0)
        if qb == ATTN_QBLOCKS - 1:
            bias_hi = bias_hi + jnp.where(n == last, NEG_INF, 0.0)
        q_rows = slice(qb * ATTN_BLOCK, (qb + 1) * ATTN_BLOCK)
        outs = []
        for kk in range(N_KV_HEADS):
            qs = q_ref[kk * Q_PER_KV:(kk + 1) * Q_PER_KV, q_rows, :].reshape(rows, HEAD_DIM)
            s0 = _nt_dot(qs, k_refs[qb][kk]) + bias_lo
            s1 = _nt_dot(qs, k_refs[qb + 1][kk])
            s2 = _nt_dot(qs, k_refs[qb + 2][kk]) + bias_hi
            sc = _nt_dot(qs, kc_ref[kk])
            sink = jnp.zeros((rows, 1), F32)
            for g in range(Q_PER_KV):
                sink = jnp.where(row_head == g, sink_ref[kk * Q_PER_KV + g], sink)
            n_ctx = sc.shape[1] // ATTN_BLOCK
            folded = jnp.maximum(jnp.maximum(s0, s1), s2)
            for j in range(n_ctx):
                folded = jnp.maximum(folded, sc[:, j * ATTN_BLOCK:(j + 1) * ATTN_BLOCK])
            m = jnp.maximum(jnp.max(folded, axis=1, keepdims=True), sink)
            p0, p1, p2, pc = jnp.exp(s0 - m), jnp.exp(s1 - m), jnp.exp(s2 - m), jnp.exp(sc - m)
            folded = p0 + p1 + p2
            for j in range(n_ctx):
                folded = folded + pc[:, j * ATTN_BLOCK:(j + 1) * ATTN_BLOCK]
            denom = jnp.sum(folded, axis=1, keepdims=True) + jnp.exp(sink - m)
            o = (_dot(p0.astype(BF16), v_refs[qb][kk]) + _dot(p1.astype(BF16), v_refs[qb + 1][kk])
                 + _dot(p2.astype(BF16), v_refs[qb + 2][kk]) + _dot(pc.astype(BF16), vc_ref[kk]))
            o = o / denom
            outs += [o[g * ATTN_BLOCK:(g + 1) * ATTN_BLOCK] for g in range(Q_PER_KV)]
        o_ref[q_rows, :] = jnp.concatenate(outs, axis=1).astype(BF16)


def _attention(q, k, v, kc, vc, sink):
    b, _, s, _ = q.shape
    c = kc.shape[2]
    nb = s // ATTN_BLOCK
    assert nb % ATTN_QBLOCKS == 0 and c % ATTN_BLOCK == 0
    rows = Q_PER_KV * ATTN_BLOCK
    q_rows = ATTN_QBLOCKS * ATTN_BLOCK
    qi = jnp.tile(jnp.arange(ATTN_BLOCK), Q_PER_KV)[:, None]
    kj = jnp.arange(ATTN_BLOCK)[None, :]
    bias_lo = jnp.where(kj >= qi, 0.0, NEG_INF).astype(F32)
    bias_hi = jnp.where(kj <= qi, 0.0, NEG_INF).astype(F32)

    def kv_spec(i):
        return pl.BlockSpec(
            (None, N_KV_HEADS, ATTN_BLOCK, HEAD_DIM),
            lambda bi, n: (bi, 0, jnp.clip(ATTN_QBLOCKS * n - 1 + i, 0, nb - 1), 0))

    kv_specs = [kv_spec(i) for i in range(ATTN_QBLOCKS + 2)]
    ctx_spec = pl.BlockSpec((None, N_KV_HEADS, c, HEAD_DIM), lambda bi, n: (bi, 0, 0, 0))
    bias_spec = pl.BlockSpec((rows, ATTN_BLOCK), lambda bi, n: (0, 0))
    return pl.pallas_call(
        _attn_body,
        grid=(b, nb // ATTN_QBLOCKS),
        in_specs=[
            pl.BlockSpec(memory_space=pltpu.SMEM),
            pl.BlockSpec((None, N_Q_HEADS, q_rows, HEAD_DIM), lambda bi, n: (bi, 0, n, 0)),
            *kv_specs, *kv_specs, ctx_spec, ctx_spec, bias_spec, bias_spec,
        ],
        out_specs=pl.BlockSpec((None, q_rows, ATTN_DIM), lambda bi, n: (bi, n, 0)),
        out_shape=jax.ShapeDtypeStruct((b, s, ATTN_DIM), BF16),
        compiler_params=_params("parallel", "parallel"),
        name="window_attn",
    )(sink, q, *([k] * (ATTN_QBLOCKS + 2)), *([v] * (ATTN_QBLOCKS + 2)), kc, vc, bias_lo, bias_hi)


def _pool_delta(ue, t_seq, seq_len):
    n_ext = ue.shape[0]
    tm = n_ext - 2 * POOL_HALO
    group_dim = ue.shape[1] // len(POOL_WINDOWS)
    outs = []
    for g, w in enumerate(POOL_WINDOWS):
        xs = ue[:, g * group_dim:(g + 1) * group_dim]
        acc = xs + pltpu.roll(xs, 1, 0)
        step = 1
        while 2 * step < w:
            acc = pltpu.roll(acc, step, 0) + pltpu.roll(acc, n_ext - step, 0)
            step *= 2
        half = w // 2
        cnt = (jnp.minimum(t_seq + half, seq_len) - jnp.maximum(t_seq - half, 0)).astype(F32)
        core = slice(POOL_HALO, POOL_HALO + tm)
        outs.append(acc[core] / cnt - xs[core])
    return outs


def _route(s, sel, n_tok):
    n_exp = s.shape[0]
    per_group = n_exp // N_EXPERT_GROUPS
    neg = float("-inf")
    iota_g = lax.broadcasted_iota(I32, (per_group, n_tok), 0).astype(F32)
    scores = []
    for g in range(N_EXPERT_GROUPS):
        blk = sel[g * per_group:(g + 1) * per_group]
        m1 = jnp.max(blk, axis=0, keepdims=True)
        first = jnp.min(jnp.where(blk == m1, iota_g, float(per_group)), axis=0, keepdims=True)
        m2 = jnp.max(jnp.where(iota_g == first, neg, blk), axis=0, keepdims=True)
        scores.append(m1 + m2)
    gs = jnp.concatenate(scores, axis=0)
    iota_ng = lax.broadcasted_iota(I32, (N_EXPERT_GROUPS, n_tok), 0).astype(F32)
    gsel = jnp.zeros((N_EXPERT_GROUPS, n_tok), F32)
    for _ in range(TOPK_GROUPS):
        m = jnp.max(gs, axis=0, keepdims=True)
        first = jnp.min(jnp.where(gs == m, iota_ng, float(N_EXPERT_GROUPS)), axis=0, keepdims=True)
        hit = iota_ng == first
        gsel = jnp.where(hit, 1.0, gsel)
        gs = jnp.where(hit, neg, gs)
    cur = jnp.concatenate(
        [jnp.where(gsel[g:g + 1] > 0.0, sel[g * per_group:(g + 1) * per_group], NEG_INF)
         for g in range(N_EXPERT_GROUPS)], axis=0)
    iota_e = lax.broadcasted_iota(I32, (n_exp, n_tok), 0).astype(F32)
    chosen = jnp.zeros((n_exp, n_tok), F32)
    ids, aff = [], []
    for _ in range(TOP_K):
        m = jnp.max(cur, axis=0, keepdims=True)
        first = jnp.min(jnp.where(cur == m, iota_e, float(n_exp)), axis=0, keepdims=True)
        hit = iota_e == first
        ids.append(first)
        aff.append(jnp.sum(jnp.where(hit, s, 0.0), axis=0, keepdims=True))
        chosen = jnp.where(hit, 1.0, chosen)
        cur = jnp.where(hit, neg, cur)
    return ids, aff, chosen


def _mixer_body(x_ref, attn_ref, up_ref, u_ref, un_ref, gate_ref, mod_ref, n2g_ref,
                wpool_ref, pscale_ref, wua_ref, wup_ref, wout_ref, wrh_ref, wrl_ref, rbias_ref,
                wsg_ref, wsu_ref, wsd_ref,
                base_ref, h2_ref, eidx_ref, wts_ref, rank_ref, cnt_ref, *, seq_len):
    bi, ti = pl.program_id(0), pl.program_id(1)
    n_t = pl.num_programs(1)
    tm, d = x_ref.shape
    pool_dim = u_ref.shape[-1]

    zero_halo = jnp.zeros((POOL_HALO, pool_dim), F32)
    u_prev = jnp.where(ti == 0, zero_halo, up_ref[...])
    u_next = jnp.where(ti == n_t - 1, zero_halo, un_ref[...])
    ue = jnp.concatenate([u_prev, u_ref[...], u_next], axis=0)
    t_seq = ti * tm + lax.broadcasted_iota(I32, (tm, 1), 0)
    deltas = _pool_delta(ue, t_seq, seq_len)
    pool = jnp.concatenate(
        [_dot(dl.astype(BF16), wpool_ref[g]) for g, dl in enumerate(deltas)], axis=1)
    pool = (pool * pscale_ref[...]).astype(BF16)

    sig_a = gate_ref[:, :d].astype(F32)
    sig_p = gate_ref[:, d:].astype(F32)
    y = sig_a * _dot(attn_ref[...], wua_ref[...]) + sig_p * _dot(pool, wup_ref[...])
    mix = _dot(y.astype(BF16), wout_ref[...])
    x1 = x_ref[...] + mod_ref[2:3, :] * mix

    h2 = _norm_mod(x1, n2g_ref[...], mod_ref[3:4, :], mod_ref[4:5, :])
    h2_hi = h2.astype(BF16)
    h2_ref[...] = _pack_rows(h2)
    shared = _dot((_silu(_dot(h2_hi, wsg_ref[...])) * _dot(h2_hi, wsu_ref[...])).astype(BF16),
                  wsd_ref[...])
    base_ref[...] = x1 + mod_ref[5:6, :] * shared

    h2_lo = (h2 - h2_hi.astype(F32)).astype(BF16)
    logits = (_nt_dot(wrh_ref[...], h2_hi) + _nt_dot(wrh_ref[...], h2_lo)
              + _nt_dot(wrl_ref[...], h2_hi))
    s = _sigmoid(logits)
    sel = s + rbias_ref[...]
    ids, aff, chosen = _route(s, sel, tm)

    total = aff[0]
    for a in aff[1:]:
        total = total + a
    wts_ref[...] = jnp.concatenate([a / total * ROUTED_SCALE for a in aff], axis=0)
    eidx_ref[...] = jnp.concatenate(ids, axis=0).astype(I32)

    @pl.when((bi == 0) & (ti == 0))
    def _():
        cnt_ref[...] = jnp.zeros_like(cnt_ref)

    before = (lax.broadcasted_iota(I32, (tm, tm), 0) < lax.broadcasted_iota(I32, (tm, tm), 1))
    prefix = _dot(chosen.astype(BF16), jnp.where(before, 1.0, 0.0).astype(BF16))
    rank_dense = prefix + cnt_ref[:, 0:1]
    n_exp = s.shape[0]
    iota_e = lax.broadcasted_iota(I32, (n_exp, tm), 0).astype(F32)
    rank_ref[...] = jnp.concatenate(
        [jnp.sum(jnp.where(iota_e == i, rank_dense, 0.0), axis=0, keepdims=True) for i in ids],
        axis=0).astype(I32)
    cnt_ref[...] = cnt_ref[...] + jnp.sum(chosen, axis=1, keepdims=True)


def _mixer(x, attn, u, gates, mod3, n2g, wpool_bf, pscale, wua_bf, wup_bf, wout_bf,
           wr_hi, wr_lo, rbias, wsg_bf, wsu_bf, wsd_bf):
    b, s, d = x.shape
    tm = min(TOKEN_TILE, s)
    n_t = s // tm
    t_all = b * s
    pool_dim = u.shape[-1]
    n_exp = wr_hi.shape[0]
    sh = wsg_bf.shape[1]
    halo_blocks = tm // POOL_HALO
    n_halo = s // POOL_HALO
    full = lambda shape: pl.BlockSpec(shape, lambda bi, i: (0,) * len(shape))
    tok = lambda bi, i: (0, bi * n_t + i)
    return pl.pallas_call(
        functools.partial(_mixer_body, seq_len=s),
        grid=(b, n_t),
        in_specs=[
            pl.BlockSpec((None, tm, d), lambda bi, i: (bi, i, 0)),
            pl.BlockSpec((None, tm, ATTN_DIM), lambda bi, i: (bi, i, 0)),
            pl.BlockSpec((None, POOL_HALO, pool_dim),
                         lambda bi, i: (bi, jnp.maximum(i * halo_blocks - 1, 0), 0)),
            pl.BlockSpec((None, tm, pool_dim), lambda bi, i: (bi, i, 0)),
            pl.BlockSpec((None, POOL_HALO, pool_dim),
                         lambda bi, i: (bi, jnp.minimum((i + 1) * halo_blocks, n_halo - 1), 0)),
            pl.BlockSpec((None, tm, 2 * d), lambda bi, i: (bi, i, 0)),
            pl.BlockSpec((None, N_ADA, d), lambda bi, i: (bi, 0, 0)),
            full((1, d)),
            full((len(POOL_WINDOWS), pool_dim // len(POOL_WINDOWS), pool_dim // len(POOL_WINDOWS))),
            full((1, pool_dim)),
            full((ATTN_DIM, d)), full((pool_dim, d)), full((d, d)),
            full((n_exp, d)), full((n_exp, d)), full((n_exp, 1)),
            full((d, sh)), full((d, sh)), full((sh, d)),
        ],
        out_specs=[
            pl.BlockSpec((None, tm, d), lambda bi, i: (bi, i, 0)),
            pl.BlockSpec((tm, d // 2), lambda bi, i: (bi * n_t + i, 0)),
            pl.BlockSpec((TOP_K, tm), tok),
            pl.BlockSpec((TOP_K, tm), tok),
            pl.BlockSpec((TOP_K, tm), tok),
            pl.BlockSpec((n_exp, LANES), lambda bi, i: (0, 0)),
        ],
        out_shape=[
            jax.ShapeDtypeStruct((b, s, d), F32),
            jax.ShapeDtypeStruct((t_all, d // 2), U32),
            jax.ShapeDtypeStruct((TOP_K, t_all), I32),
            jax.ShapeDtypeStruct((TOP_K, t_all), F32),
            jax.ShapeDtypeStruct((TOP_K, t_all), I32),
            jax.ShapeDtypeStruct((n_exp, LANES), F32),
        ],
        compiler_params=_params("arbitrary", "arbitrary"),
        name="mixer_router",
    )(x, attn, u, u, u, gates, mod3, n2g, wpool_bf, pscale, wua_bf, wup_bf, wout_bf,
      wr_hi, wr_lo, rbias, wsg_bf, wsu_bf, wsd_bf)


def _slots_body(eidx_ref, rank_ref, cnt_ref, pos_ref, bstart_ref, *, block_rows):
    n_exp = cnt_ref.shape[0]
    tm = eidx_ref.shape[1]
    cnt = cnt_ref[...]
    padded = jnp.floor((cnt + (block_rows - 1)) / block_rows) * block_rows
    hi = jnp.floor(padded / 256.0)
    lo = padded - hi * 256.0
    below = (lax.broadcasted_iota(I32, (n_exp, n_exp), 1) < lax.broadcasted_iota(I32, (n_exp, n_exp), 0))
    tri = jnp.where(below, 1.0, 0.0).astype(BF16)
    start = 256.0 * _dot(tri, hi.astype(BF16)) + _dot(tri, lo.astype(BF16))
    end = start + padded
    iota_e = lax.broadcasted_iota(I32, (n_exp, tm), 0)
    start_col = start[:, 0:1]
    rows = []
    for k in range(TOP_K):
        hit = iota_e == eidx_ref[k:k + 1, :]
        rows.append(jnp.sum(jnp.where(hit, start_col, 0.0), axis=0, keepdims=True))
    pos_ref[...] = jnp.concatenate(rows, axis=0).astype(I32) + rank_ref[...]

    @pl.when(pl.program_id(0) == 0)
    def _():
        bstart_ref[0:n_exp, :] = (start / block_rows).astype(I32)
        bstart_ref[n_exp:, :] = jnp.broadcast_to(
            (end[n_exp - 1:n_exp, :] / block_rows).astype(I32), (SUBLANES, LANES))


def _slots(eidx, rank, cnt, block_rows):
    t_all = eidx.shape[1]
    n_exp = cnt.shape[0]
    tm = min(TOKEN_TILE, t_all)
    return pl.pallas_call(
        functools.partial(_slots_body, block_rows=block_rows),
        grid=(t_all // tm,),
        in_specs=[pl.BlockSpec((TOP_K, tm), lambda i: (0, i)),
                  pl.BlockSpec((TOP_K, tm), lambda i: (0, i)),
                  pl.BlockSpec((n_exp, LANES), lambda i: (0, 0))],
        out_specs=[pl.BlockSpec((TOP_K, tm), lambda i: (0, i)),
                   pl.BlockSpec((n_exp + SUBLANES, LANES), lambda i: (0, 0))],
        out_shape=[jax.ShapeDtypeStruct((TOP_K, t_all), I32),
                   jax.ShapeDtypeStruct((n_exp + SUBLANES, LANES), I32)],
        compiler_params=_params("arbitrary"),
        name="slot_positions",
    )(eidx, rank, cnt)


def _expert_body(bstart_ref, xs_hbm, wg_ref, wu_ref, wd_ref, y_hbm,
                 xbuf, ybuf, wgu, wdn, sem_in, sem_out, *, block_rows):
    e = pl.program_id(0)
    n_exp = pl.num_programs(0)
    first, last, total = bstart_ref[e], bstart_ref[e + 1], bstart_ref[n_exp]
    f = wg_ref.shape[1]
    ahead = EXPERT_RING - EXPERT_GROUP

    def rows(i):
        return pl.ds(pl.multiple_of(i * block_rows, block_rows), block_rows)

    def slot_of(i):
        return i & (EXPERT_RING - 1)

    def in_copy(i):
        return pltpu.make_async_copy(xs_hbm.at[rows(i)], xbuf.at[slot_of(i)], sem_in.at[slot_of(i)])

    def out_copy(i):
        return pltpu.make_async_copy(ybuf.at[slot_of(i)], y_hbm.at[rows(i)], sem_out.at[slot_of(i)])

    def start_in(i):
        @pl.when(i < total)
        def _():
            in_copy(i).start()

    def wait_out(i):
        @pl.when(i >= 0)
        def _():
            out_copy(i).wait()

    @pl.when(e == 0)
    def _():
        for j in range(ahead):
            start_in(jnp.int32(j))

    wgu[:, :f] = wg_ref[...].astype(BF16)
    wgu[:, f:] = wu_ref[...].astype(BF16)
    wdn[...] = wd_ref[...].astype(BF16)

    def ffn(i):
        lo, hi = _unpack_rows(xbuf[slot_of(i)])
        x = jnp.concatenate([lo, hi], axis=1).astype(BF16)
        gu = _dot(x, wgu[...])
        return _pack_rows(_dot((_silu(gu[:, :f]) * gu[:, f:]).astype(BF16), wdn[...]))

    def run_blocks(i, n):
        for j in range(n):
            in_copy(i + j).wait()
        for j in range(n):
            start_in(i + ahead + j)
        res = [ffn(i + j) for j in range(n)]
        for j in range(n):
            wait_out(i + j - EXPERT_RING)
        for j in range(n):
            ybuf[slot_of(i + j)] = res[j]
        for j in range(n):
            out_copy(i + j).start()

    n_blk = last - first
    n_groups = n_blk // EXPERT_GROUP

    def group(p, carry):
        run_blocks(first + EXPERT_GROUP * p, EXPERT_GROUP)
        return carry

    lax.fori_loop(0, n_groups, group, 0)

    @pl.when(n_blk - n_groups * EXPERT_GROUP == 1)
    def _():
        run_blocks(last - 1, 1)

    @pl.when(e == n_exp - 1)
    def _():
        for j in range(EXPERT_RING, 0, -1):
            wait_out(total - j)


def _experts(xs, bstart, w_gate, w_up, w_down, block_rows):
    n_slots, dw = xs.shape
    n_exp, d, f = w_gate.shape
    expert = lambda e, bs: (e, 0, 0)
    grid_spec = pltpu.PrefetchScalarGridSpec(
        num_scalar_prefetch=1,
        grid=(n_exp,),
        in_specs=[
            pl.BlockSpec(memory_space=pl.ANY),
            pl.BlockSpec((None, d, f), expert),
            pl.BlockSpec((None, d, f), expert),
            pl.BlockSpec((None, f, d), expert),
        ],
        out_specs=pl.BlockSpec(memory_space=pl.ANY),
        scratch_shapes=[
            pltpu.VMEM((EXPERT_RING, block_rows, dw), U32),
            pltpu.VMEM((EXPERT_RING, block_rows, dw), U32),
            pltpu.VMEM((d, 2 * f), BF16),
            pltpu.VMEM((f, d), BF16),
            pltpu.SemaphoreType.DMA((EXPERT_RING,)),
            pltpu.SemaphoreType.DMA((EXPERT_RING,)),
        ],
    )
    return pl.pallas_call(
        functools.partial(_expert_body, block_rows=block_rows),
        grid_spec=grid_spec,
        out_shape=jax.ShapeDtypeStruct((n_slots, dw), U32),
        compiler_params=_params("arbitrary"),
        name="expert_ffn",
    )(bstart, xs, w_gate, w_up, w_down)


SC_ROWS = 128


def _sc_workers():
    info = plsc.get_sparse_core_info()
    return info.num_cores, info.num_subcores


def _dispatch(h_rows, pos, n_slots):
    t_all, dw = h_rows.shape
    n_cores, n_sub = _sc_workers()
    per_worker = t_all // (n_cores * n_sub)
    assert per_worker * n_cores * n_sub == t_all and per_worker % SC_ROWS == 0
    mesh = plsc.VectorSubcoreMesh(core_axis_name="c", subcore_axis_name="s")

    @functools.partial(
        pl.kernel, mesh=mesh,
        out_type=jax.ShapeDtypeStruct((n_slots, dw), h_rows.dtype),
        scratch_types=[pltpu.VMEM((TOP_K, SC_ROWS), I32), pltpu.VMEM((SC_ROWS, dw), h_rows.dtype),
                       pltpu.SemaphoreType.DMA],
    )
    def body(h_hbm, pos_hbm, out_hbm, idx_v, rows_v, sem):
        base = (lax.axis_index("s") * n_cores + lax.axis_index("c")) * per_worker

        @pl.loop(0, per_worker // SC_ROWS)
        def _(j):
            t0 = pl.multiple_of(base + j * SC_ROWS, SC_ROWS)
            pltpu.sync_copy(pos_hbm.at[:, pl.ds(t0, SC_ROWS)], idx_v)
            pltpu.sync_copy(h_hbm.at[pl.ds(t0, SC_ROWS)], rows_v)
            copies = [pltpu.async_copy(rows_v, out_hbm.at[idx_v.at[k]], sem) for k in range(TOP_K)]
            for cp in copies:
                cp.wait()

    return body(h_rows, pos)


def _gather_back(y_rows, pos):
    top_k, t_all = pos.shape
    dw = y_rows.shape[1]
    n_cores, n_sub = _sc_workers()
    per_worker = t_all // (n_cores * n_sub)
    assert per_worker * n_cores * n_sub == t_all and per_worker % SC_ROWS == 0
    half = SC_ROWS // 2
    mesh = plsc.VectorSubcoreMesh(core_axis_name="c", subcore_axis_name="s")

    @functools.partial(
        pl.kernel, mesh=mesh,
        out_type=jax.ShapeDtypeStruct((top_k, t_all, dw), y_rows.dtype),
        scratch_types=[pltpu.VMEM((top_k, SC_ROWS), I32),
                       pltpu.VMEM((half, dw), y_rows.dtype), pltpu.VMEM((half, dw), y_rows.dtype),
                       pltpu.SemaphoreType.DMA, pltpu.SemaphoreType.DMA, pltpu.SemaphoreType.DMA],
    )
    def body(y_hbm, pos_hbm, out_hbm, idx_v, buf_a, buf_b, sem_g, sem_a, sem_b):
        base = (lax.axis_index("s") * n_cores + lax.axis_index("c")) * per_worker
        bufs, sems = (buf_a, buf_b), (sem_a, sem_b)

        @pl.loop(0, per_worker // SC_ROWS)
        def _(j):
            t0 = pl.multiple_of(base + j * SC_ROWS, SC_ROWS)
            pltpu.sync_copy(pos_hbm.at[:, pl.ds(t0, SC_ROWS)], idx_v)
            pending = [None, None]
            for step in range(2 * top_k):
                k, h = step // 2, step % 2
                slot = step % 2
                if pending[slot] is not None:
                    pending[slot].wait()
                pltpu.async_copy(y_hbm.at[idx_v.at[k, pl.ds(h * half, half)]], bufs[slot], sem_g).wait()
                pending[slot] = pltpu.async_copy(
                    bufs[slot], out_hbm.at[k, pl.ds(t0 + h * half, half)], sems[slot])
            for p in pending:
                p.wait()

    return body(y_rows, pos)


def _combine_body(yg_ref, w_ref, base_ref, mod_ref, fg_ref, o_ref):
    acc_lo = acc_hi = None
    for k in range(TOP_K):
        lo, hi = _unpack_rows(yg_ref[k])
        wk = w_ref[:, k:k + 1]
        acc_lo = lo * wk if acc_lo is None else acc_lo + lo * wk
        acc_hi = hi * wk if acc_hi is None else acc_hi + hi * wk
    acc = jnp.concatenate([acc_lo, acc_hi], axis=1)
    x2 = base_ref[...] + mod_ref[5:6, :] * acc
    ms = jnp.mean(x2 * x2, axis=-1, keepdims=True)
    o_ref[...] = (x2 * lax.rsqrt(ms + NORM_EPS)) * fg_ref[...]


def _combine(yg, wts_t, base, mod3, final_g):
    b, s, d = base.shape
    tm = min(256, s)
    n_t = s // tm
    return pl.pallas_call(
        _combine_body,
        grid=(b, n_t),
        in_specs=[pl.BlockSpec((TOP_K, tm, d // 2), lambda bi, i: (0, bi * n_t + i, 0)),
                  pl.BlockSpec((tm, TOP_K), lambda bi, i: (bi * n_t + i, 0)),
                  pl.BlockSpec((None, tm, d), lambda bi, i: (bi, i, 0)),
                  pl.BlockSpec((None, N_ADA, d), lambda bi, i: (bi, 0, 0)),
                  pl.BlockSpec((1, d), lambda bi, i: (0, 0))],
        out_specs=pl.BlockSpec((None, tm, d), lambda bi, i: (bi, i, 0)),
        out_shape=jax.ShapeDtypeStruct((b, s, d), F32),
        compiler_params=_params("parallel", "parallel"),
        name="combine_norm",
    )(yg, wts_t, base, mod3, final_g)


def _rope_tables(seq_len):
    rows = seq_len // GRID_W
    row = jnp.repeat(jnp.arange(rows), GRID_W).astype(F32)
    col = jnp.tile(jnp.arange(GRID_W), rows).astype(F32)
    n_freq = HEAD_DIM // 4
    inv = ROPE_THETA ** (-jnp.arange(n_freq, dtype=F32) / n_freq)
    ar, ac = row[:, None] * inv, col[:, None] * inv
    zeros = jnp.zeros_like(ar)
    reps = LANES // HEAD_DIM
    cos = jnp.tile(jnp.concatenate([jnp.cos(ar), jnp.cos(ar), jnp.cos(ac), jnp.cos(ac)], 1), (1, reps))
    sin_a = jnp.tile(jnp.concatenate([-jnp.sin(ar), zeros, -jnp.sin(ac), zeros], 1), (1, reps))
    sin_b = jnp.tile(jnp.concatenate([zeros, jnp.sin(ar), zeros, jnp.sin(ac)], 1), (1, reps))
    return cos, sin_a, sin_b


def kernel(x, c, ctx, c_ctx, w_ada, b_ada, norm1_g, w_in, attn_sink, w_pool, pool_scale,
           w_up_attn, w_up_pool, w_out, norm2_g, w_router, router_bias,
           w_exp_gate, w_exp_up, w_exp_down, w_sh_gate, w_sh_up, w_sh_down, final_g):
    b, s, d = x.shape
    assert w_ada.shape[0] == 1, "single-layer block"
    assert s % ATTN_BLOCK == 0 and s % GRID_W == 0 and d % LANES == 0
    pool_dim = w_up_pool.shape[1]
    n_exp = w_router.shape[-1]
    t_all = b * s

    pad_rows = -(-(b + 1) // SUBLANES) * SUBLANES
    c_rows = jnp.concatenate([c, c_ctx[None, :], jnp.zeros((pad_rows - b - 1, d), F32)], axis=0)
    mod3 = _ada(c_rows, w_ada[0], b_ada[0]).reshape(pad_rows, N_ADA, d)

    w_in_bf = w_in[0].astype(BF16)
    g1 = norm1_g[0].reshape(1, d)
    cos, sin_a, sin_b = _rope_tables(s)
    q, k, v, u, gates = _inproj(x, mod3, g1, w_in_bf, cos, sin_a, sin_b, pool_dim)
    kc, vc = _ctxkv(ctx, mod3[b], g1, w_in_bf[:, ATTN_DIM:ATTN_DIM + 2 * KV_DIM])
    attn = _attention(q, k, v, kc, vc, attn_sink[0])

    w_r_t = w_router[0].T
    w_r_hi = w_r_t.astype(BF16)
    w_r_lo = (w_r_t - w_r_hi.astype(F32)).astype(BF16)
    base, h2, eidx, wts, rank, cnt = _mixer(
        x, attn, u, gates, mod3, norm2_g[0].reshape(1, d), w_pool[0].astype(BF16),
        pool_scale[0].reshape(1, pool_dim), w_up_attn[0].astype(BF16), w_up_pool[0].astype(BF16),
        w_out[0].astype(BF16), w_r_hi, w_r_lo, router_bias[0].reshape(n_exp, 1),
        w_sh_gate[0].astype(BF16), w_sh_up[0].astype(BF16), w_sh_down[0].astype(BF16))

    br = EXPERT_BLOCK_ROWS
    n_blocks = (t_all * TOP_K + n_exp * (br - 1) + br - 1) // br
    n_slots = n_blocks * br
    pos, bstart = _slots(eidx, rank, cnt, br)
    xs = _dispatch(h2, pos, n_slots)
    y = _experts(xs, bstart[:n_exp + 1, 0], w_exp_gate[0], w_exp_up[0], w_exp_down[0], br)
    yg = _gather_back(y, pos)

    return _combine(yg, wts.T, base, mod3, final_g.reshape(1, d))
```

```python
import functools

import jax
import jax.numpy as jnp
from jax import lax
from jax.experimental import pallas as pl
from jax.experimental.pallas import tpu as pltpu
from jax.experimental.pallas import tpu_sc as plsc

F32 = jnp.float32
BF16 = jnp.bfloat16
I32 = jnp.int32
U32 = jnp.uint32

GRID_W = 64
HEAD_DIM = 64
N_Q_HEADS = 8
N_KV_HEADS = 2
Q_PER_KV = N_Q_HEADS // N_KV_HEADS
ATTN_DIM = N_Q_HEADS * HEAD_DIM
KV_DIM = N_KV_HEADS * HEAD_DIM
ATTN_BLOCK = 128
ATTN_SCALE = HEAD_DIM ** -0.5
ROPE_THETA = 10000.0
POOL_WINDOWS = (2, 4, 8, 16)
POOL_HALO = 8
N_EXPERT_GROUPS = 8
TOPK_GROUPS = 4
TOP_K = 8
ROUTED_SCALE = 2.5
N_ADA = 6
NORM_EPS = 1e-6
NEG_INF = -1e30
LANES = 128
SUBLANES = 8
ATTN_QBLOCKS = 4
TOKEN_TILE = 512
EXPERT_BLOCK_ROWS = 256
EXPERT_GROUP = 2
EXPERT_RING = 8
VMEM_LIMIT = 56 * 1024 * 1024


def _sigmoid(x):
    return 1.0 / (1.0 + jnp.exp(-x))


def _silu(x):
    return x * _sigmoid(x)


def _nt_dot(a, b):
    return lax.dot_general(a, b, (((1,), (1,)), ((), ())), preferred_element_type=F32)


def _dot(a, b):
    return jnp.dot(a, b, preferred_element_type=F32)


def _pack_rows(x):
    n = x.shape[1] // 2
    bits = lax.bitcast_convert_type(x.astype(BF16).astype(F32), U32)
    return (bits[:, :n] >> 16) | (bits[:, n:] & jnp.uint32(0xFFFF0000))


def _unpack_rows(w):
    lo = lax.bitcast_convert_type(w << 16, F32)
    hi = lax.bitcast_convert_type(w & jnp.uint32(0xFFFF0000), F32)
    return lo, hi


def _params(*sem):
    return pltpu.CompilerParams(dimension_semantics=sem, vmem_limit_bytes=VMEM_LIMIT)


def _ada_body(c_ref, w_ref, b_ref, o_ref):
    s = _silu(c_ref[...])
    o_ref[...] = jnp.dot(s, w_ref[...], preferred_element_type=F32,
                         precision=lax.Precision.HIGHEST) + b_ref[...]


def _ada(c_rows, w_ada, b_ada):
    rows, d = c_rows.shape
    n = w_ada.shape[1]
    bn = d
    return pl.pallas_call(
        _ada_body,
        grid=(n // bn,),
        in_specs=[pl.BlockSpec((rows, d), lambda j: (0, 0)),
                  pl.BlockSpec((d, bn), lambda j: (0, j)),
                  pl.BlockSpec((1, bn), lambda j: (0, j))],
        out_specs=pl.BlockSpec((rows, bn), lambda j: (0, j)),
        out_shape=jax.ShapeDtypeStruct((rows, n), F32),
        compiler_params=_params("arbitrary"),
        name="ada_mod",
    )(c_rows, w_ada, b_ada.reshape(1, n))


def _norm_mod(x, g, shift, scale):
    ms = jnp.mean(x * x, axis=-1, keepdims=True)
    y = x * lax.rsqrt(ms + NORM_EPS)
    return (y * g) * (1.0 + scale) + shift


def _rope(t, cos, sin_a, sin_b):
    return (t * cos + pltpu.roll(t, LANES - HEAD_DIM // 4, 1) * sin_a
            + pltpu.roll(t, HEAD_DIM // 4, 1) * sin_b)


def _inproj_body(x_ref, mod_ref, g_ref, w_ref, cos_ref, sa_ref, sb_ref,
                 q_ref, k_ref, v_ref, u_ref, gate_ref):
    h = _norm_mod(x_ref[...], g_ref[...], mod_ref[0:1, :], mod_ref[1:2, :]).astype(BF16)
    cos, sa, sb = cos_ref[...], sa_ref[...], sb_ref[...]
    heads_per_chunk = LANES // HEAD_DIM
    for j in range(ATTN_DIM // LANES):
        t = _rope(_dot(h, w_ref[:, j * LANES:(j + 1) * LANES]), cos, sa, sb) * ATTN_SCALE
        t = t.astype(BF16)
        for i in range(heads_per_chunk):
            q_ref[heads_per_chunk * j + i] = t[:, i * HEAD_DIM:(i + 1) * HEAD_DIM]
    k_off = ATTN_DIM
    t = _rope(_dot(h, w_ref[:, k_off:k_off + KV_DIM]), cos, sa, sb).astype(BF16)
    for i in range(N_KV_HEADS):
        k_ref[i] = t[:, i * HEAD_DIM:(i + 1) * HEAD_DIM]
    v_off = k_off + KV_DIM
    t = _dot(h, w_ref[:, v_off:v_off + KV_DIM]).astype(BF16)
    for i in range(N_KV_HEADS):
        v_ref[i] = t[:, i * HEAD_DIM:(i + 1) * HEAD_DIM]
    p_off = v_off + KV_DIM
    pool_dim = u_ref.shape[-1]
    u_ref[...] = _dot(h, w_ref[:, p_off:p_off + pool_dim])
    g_off = p_off + pool_dim
    gate_dim = gate_ref.shape[-1]
    chunk = 512
    for j in range(gate_dim // chunk):
        t = _dot(h, w_ref[:, g_off + j * chunk:g_off + (j + 1) * chunk])
        gate_ref[:, j * chunk:(j + 1) * chunk] = _sigmoid(t).astype(BF16)


def _inproj(x, mod3, g, w_in_bf, cos, sa, sb, pool_dim):
    b, s, d = x.shape
    tm = min(TOKEN_TILE, s)
    in_dim = w_in_bf.shape[1]
    gate_dim = in_dim - ATTN_DIM - 2 * KV_DIM - pool_dim
    grid = (b, s // tm)
    return pl.pallas_call(
        _inproj_body,
        grid=grid,
        in_specs=[
            pl.BlockSpec((None, tm, d), lambda bi, i: (bi, i, 0)),
            pl.BlockSpec((None, N_ADA, d), lambda bi, i: (bi, 0, 0)),
            pl.BlockSpec((1, d), lambda bi, i: (0, 0)),
            pl.BlockSpec((d, in_dim), lambda bi, i: (0, 0)),
            pl.BlockSpec((tm, LANES), lambda bi, i: (i, 0)),
            pl.BlockSpec((tm, LANES), lambda bi, i: (i, 0)),
            pl.BlockSpec((tm, LANES), lambda bi, i: (i, 0)),
        ],
        out_specs=[
            pl.BlockSpec((None, N_Q_HEADS, tm, HEAD_DIM), lambda bi, i: (bi, 0, i, 0)),
            pl.BlockSpec((None, N_KV_HEADS, tm, HEAD_DIM), lambda bi, i: (bi, 0, i, 0)),
            pl.BlockSpec((None, N_KV_HEADS, tm, HEAD_DIM), lambda bi, i: (bi, 0, i, 0)),
            pl.BlockSpec((None, tm, pool_dim), lambda bi, i: (bi, i, 0)),
            pl.BlockSpec((None, tm, gate_dim), lambda bi, i: (bi, i, 0)),
        ],
        out_shape=[
            jax.ShapeDtypeStruct((b, N_Q_HEADS, s, HEAD_DIM), BF16),
            jax.ShapeDtypeStruct((b, N_KV_HEADS, s, HEAD_DIM), BF16),
            jax.ShapeDtypeStruct((b, N_KV_HEADS, s, HEAD_DIM), BF16),
            jax.ShapeDtypeStruct((b, s, pool_dim), F32),
            jax.ShapeDtypeStruct((b, s, gate_dim), BF16),
        ],
        compiler_params=_params("parallel", "parallel"),
        name="in_proj",
    )(x, mod3, g, w_in_bf, cos, sa, sb)


def _ctxkv_body(ctx_ref, mod_ref, g_ref, w_ref, kc_ref, vc_ref):
    h = _norm_mod(ctx_ref[...], g_ref[...], mod_ref[0:1, :], mod_ref[1:2, :]).astype(BF16)
    t = _dot(h, w_ref[...]).astype(BF16)
    for i in range(N_KV_HEADS):
        kc_ref[i] = t[:, i * HEAD_DIM:(i + 1) * HEAD_DIM]
        vc_ref[i] = t[:, KV_DIM + i * HEAD_DIM:KV_DIM + (i + 1) * HEAD_DIM]


def _ctxkv(ctx, mod_c, g, w_kv_bf):
    b, c, d = ctx.shape
    out = jax.ShapeDtypeStruct((b, N_KV_HEADS, c, HEAD_DIM), BF16)
    spec = pl.BlockSpec((None, N_KV_HEADS, c, HEAD_DIM), lambda bi: (bi, 0, 0, 0))
    return pl.pallas_call(
        _ctxkv_body,
        grid=(b,),
        in_specs=[pl.BlockSpec((None, c, d), lambda bi: (bi, 0, 0)),
                  pl.BlockSpec((N_ADA, d), lambda bi: (0, 0)),
                  pl.BlockSpec((1, d), lambda bi: (0, 0)),
                  pl.BlockSpec((d, 2 * KV_DIM), lambda bi: (0, 0))],
        out_specs=[spec, spec],
        out_shape=[out, out],
        compiler_params=_params("parallel"),
        name="ctx_kv",
    )(ctx, mod_c, g, w_kv_bf)


def _attn_body(sink_ref, q_ref, *refs):
    nkb = ATTN_QBLOCKS + 2
    k_refs, v_refs = refs[:nkb], refs[nkb:2 * nkb]
    kc_ref, vc_ref, lo_ref, hi_ref, o_ref = refs[2 * nkb:]
    n = pl.program_id(1)
    last = pl.num_programs(1) - 1
    rows = Q_PER_KV * ATTN_BLOCK
    row_head = lax.broadcasted_iota(I32, (rows, 1), 0) // ATTN_BLOCK
    for qb in range(ATTN_QBLOCKS):
        bias_lo, bias_hi = lo_ref[...], hi_ref[...]
        if qb == 0:
            bias_lo = bias_lo + jnp.where(n == 0, NEG_INF, 0.0)
        if qb == ATTN_QBLOCKS - 1:
            bias_hi = bias_hi + jnp.where(n == last, NEG_INF, 0.0)
        q_rows = slice(qb * ATTN_BLOCK, (qb + 1) * ATTN_BLOCK)
        outs = []
        for kk in range(N_KV_HEADS):
            qs = q_ref[kk * Q_PER_KV:(kk + 1) * Q_PER_KV, q_rows, :].reshape(rows, HEAD_DIM)
            s0 = _nt_dot(qs, k_refs[qb][kk]) + bias_lo
            s1 = _nt_dot(qs, k_refs[qb + 1][kk])
            s2 = _nt_dot(qs, k_refs[qb + 2][kk]) + bias_hi
            sc = _nt_dot(qs, kc_ref[kk])
            sink = jnp.zeros((rows, 1), F32)
            for g in range(Q_PER_KV):
                sink = jnp.where(row_head == g, sink_ref[kk * Q_PER_KV + g], sink)
            n_ctx = sc.shape[1] // ATTN_BLOCK
            folded = jnp.maximum(jnp.maximum(s0, s1), s2)
            for j in range(n_ctx):
                folded = jnp.maximum(folded, sc[:, j * ATTN_BLOCK:(j + 1) * ATTN_BLOCK])
            m = jnp.maximum(jnp.max(folded, axis=1, keepdims=True), sink)
            p0, p1, p2, pc = jnp.exp(s0 - m), jnp.exp(s1 - m), jnp.exp(s2 - m), jnp.exp(sc - m)
            folded = p0 + p1 + p2
            for j in range(n_ctx):
                folded = folded + pc[:, j * ATTN_BLOCK:(j + 1) * ATTN_BLOCK]
            denom = jnp.sum(folded, axis=1, keepdims=True) + jnp.exp(sink - m)
            o = (_dot(p0.astype(BF16), v_refs[qb][kk]) + _dot(p1.astype(BF16), v_refs[qb + 1][kk])
                 + _dot(p2.astype(BF16), v_refs[qb + 2][kk]) + _dot(pc.astype(BF16), vc_ref[kk]))
            o = o / denom
            outs += [o[g * ATTN_BLOCK:(g + 1) * ATTN_BLOCK] for g in range(Q_PER_KV)]
        o_ref[q_rows, :] = jnp.concatenate(outs, axis=1).astype(BF16)


def _attention(q, k, v, kc, vc, sink):
    b, _, s, _ = q.shape
    c = kc.shape[2]
    nb = s // ATTN_BLOCK
    assert nb % ATTN_QBLOCKS == 0 and c % ATTN_BLOCK == 0
    rows = Q_PER_KV * ATTN_BLOCK
    q_rows = ATTN_QBLOCKS * ATTN_BLOCK
    qi = jnp.tile(jnp.arange(ATTN_BLOCK), Q_PER_KV)[:, None]
    kj = jnp.arange(ATTN_BLOCK)[None, :]
    bias_lo = jnp.where(kj >= qi, 0.0, NEG_INF).astype(F32)
    bias_hi = jnp.where(kj <= qi, 0.0, NEG_INF).astype(F32)

    def kv_spec(i):
        return pl.BlockSpec(
            (None, N_KV_HEADS, ATTN_BLOCK, HEAD_DIM),
            lambda bi, n: (bi, 0, jnp.clip(ATTN_QBLOCKS * n - 1 + i, 0, nb - 1), 0))

    kv_specs = [kv_spec(i) for i in range(ATTN_QBLOCKS + 2)]
    ctx_spec = pl.BlockSpec((None, N_KV_HEADS, c, HEAD_DIM), lambda bi, n: (bi, 0, 0, 0))
    bias_spec = pl.BlockSpec((rows, ATTN_BLOCK), lambda bi, n: (0, 0))
    return pl.pallas_call(
        _attn_body,
        grid=(b, nb // ATTN_QBLOCKS),
        in_specs=[
            pl.BlockSpec(memory_space=pltpu.SMEM),
            pl.BlockSpec((None, N_Q_HEADS, q_rows, HEAD_DIM), lambda bi, n: (bi, 0, n, 0)),
            *kv_specs, *kv_specs, ctx_spec, ctx_spec, bias_spec, bias_spec,
        ],
        out_specs=pl.BlockSpec((None, q_rows, ATTN_DIM), lambda bi, n: (bi, n, 0)),
        out_shape=jax.ShapeDtypeStruct((b, s, ATTN_DIM), BF16),
        compiler_params=_params("parallel", "parallel"),
        name="window_attn",
    )(sink, q, *([k] * (ATTN_QBLOCKS + 2)), *([v] * (ATTN_QBLOCKS + 2)), kc, vc, bias_lo, bias_hi)


def _pool_delta(ue, inv_cnt):
    n_ext = ue.shape[0]
    tm = n_ext - 2 * POOL_HALO
    group_dim = ue.shape[1] // len(POOL_WINDOWS)
    outs = []
    for g, w in enumerate(POOL_WINDOWS):
        xs = ue[:, g * group_dim:(g + 1) * group_dim]
        acc = xs + pltpu.roll(xs, 1, 0)
        step = 1
        while 2 * step < w:
            acc = pltpu.roll(acc, step, 0) + pltpu.roll(acc, n_ext - step, 0)
            step *= 2
        core = slice(POOL_HALO, POOL_HALO + tm)
        outs.append(acc[core] * inv_cnt[:, g:g + 1] - xs[core])
    return outs


def _route_steps(s, sel, iota_ref, out):
    n_exp = s.shape[0]
    per_group = n_exp // N_EXPERT_GROUPS
    neg = float("-inf")
    iota_g = iota_ref[0:per_group, :]
    scores = []
    for g in range(N_EXPERT_GROUPS):
        blk = sel[g * per_group:(g + 1) * per_group]
        m1 = jnp.max(blk, axis=0, keepdims=True)
        first = jnp.min(jnp.where(blk == m1, iota_g, float(per_group)), axis=0, keepdims=True)
        m2 = jnp.max(jnp.where(iota_g == first, neg, blk), axis=0, keepdims=True)
        scores.append(m1 + m2)
        if g % 2 == 1:
            yield
    gs = jnp.concatenate(scores, axis=0)
    iota_ng = iota_ref[0:N_EXPERT_GROUPS, :]
    gsel = jnp.zeros(iota_ng.shape, F32)
    for _ in range(TOPK_GROUPS):
        m = jnp.max(gs, axis=0, keepdims=True)
        first = jnp.min(jnp.where(gs == m, iota_ng, float(N_EXPERT_GROUPS)), axis=0, keepdims=True)
        hit = iota_ng == first
        gsel = jnp.where(hit, 1.0, gsel)
        gs = jnp.where(hit, neg, gs)
    cur = jnp.concatenate(
        [jnp.where(gsel[g:g + 1] > 0.0, sel[g * per_group:(g + 1) * per_group], NEG_INF)
         for g in range(N_EXPERT_GROUPS)], axis=0)
    yield
    chosen = jnp.zeros(s.shape, F32)
    ids, aff = [], []
    for _ in range(TOP_K):
        m = jnp.max(cur, axis=0, keepdims=True)
        first = jnp.min(jnp.where(cur == m, iota_ref[...], float(n_exp)), axis=0, keepdims=True)
        hit = iota_ref[...] == first
        ids.append(first)
        aff.append(jnp.sum(jnp.where(hit, s, 0.0), axis=0, keepdims=True))
        chosen = jnp.where(hit, 1.0, chosen)
        cur = jnp.where(hit, neg, cur)
        yield
    out.update(ids=ids, aff=aff, chosen=chosen)


def _mixer_body(x_ref, attn_ref, up_ref, u_ref, un_ref, gate_ref, mod_ref, icnt_ref, iota_ref, n2g_ref,
                wpool_ref, pscale_ref, wua_ref, wup_ref, wout_ref, wrh_ref, wrl_ref, rbias_ref,
                wsg_ref, wsu_ref, wsd_ref,
                base_ref, h2_ref, eidx_ref, wts_ref, rank_ref, cnt_ref, score_buf, *, n_t):
    g = pl.program_id(0)
    n_tiles = pl.num_programs(0) - 1
    ti = lax.rem(jnp.minimum(g, n_tiles - 1), n_t)
    slot = g & 1
    tm, d = x_ref.shape
    pool_dim = u_ref.shape[-1]
    n_exp = score_buf.shape[1]

    @pl.when(g == 0)
    def _():
        cnt_ref[...] = jnp.zeros_like(cnt_ref)
        score_buf[1] = jnp.zeros((n_exp, tm), F32)

    def routing():
        s = score_buf[1 - slot]
        sel = s + rbias_ref[...]
        res = {}
        yield from _route_steps(s, sel, iota_ref, res)
        ids, aff, chosen = res["ids"], res["aff"], res["chosen"]
        total = aff[0]
        for a in aff[1:]:
            total = total + a
        wts_ref[...] = jnp.concatenate([a / total * ROUTED_SCALE for a in aff], axis=0)
        eidx_ref[...] = jnp.concatenate(ids, axis=0).astype(I32)
        yield
        before = (lax.broadcasted_iota(I32, (tm, tm), 0) < lax.broadcasted_iota(I32, (tm, tm), 1))
        prefix = _dot(chosen.astype(BF16), jnp.where(before, 1.0, 0.0).astype(BF16))
        rank_dense = prefix + cnt_ref[:, 0:1]
        rank_ref[...] = jnp.concatenate(
            [jnp.sum(jnp.where(iota_ref[...] == i, rank_dense, 0.0), axis=0, keepdims=True) for i in ids],
            axis=0).astype(I32)
        real = jnp.where(g > 0, 1.0, 0.0)
        cnt_ref[...] = cnt_ref[...] + real * jnp.sum(chosen, axis=1, keepdims=True)

    def chain():
        zero_halo = jnp.zeros((POOL_HALO, pool_dim), F32)
        u_prev = jnp.where(ti == 0, zero_halo, up_ref[...])
        u_next = jnp.where(ti == n_t - 1, zero_halo, un_ref[...])
        ue = jnp.concatenate([u_prev, u_ref[...], u_next], axis=0)
        deltas = _pool_delta(ue, icnt_ref[...])
        attn_up = _dot(attn_ref[...], wua_ref[...])
        yield
        pool = jnp.concatenate(
            [_dot(dl.astype(BF16), wpool_ref[i]) for i, dl in enumerate(deltas)], axis=1)
        pool = (pool * pscale_ref[...]).astype(BF16)
        yield
        pool_up = _dot(pool, wup_ref[...])
        yield
        y = gate_ref[:, :d] * attn_up.astype(BF16) + gate_ref[:, d:] * pool_up.astype(BF16)
        mix = _dot(y, wout_ref[...])
        yield
        x1 = x_ref[...] + mod_ref[2:3, :] * mix
        h2 = _norm_mod(x1, n2g_ref[...], mod_ref[3:4, :], mod_ref[4:5, :])
        h2_hi = h2.astype(BF16)
        h2_ref[...] = _pack_rows(h2)
        gate = _dot(h2_hi, wsg_ref[...])
        yield
        up = _dot(h2_hi, wsu_ref[...])
        yield
        shared = _dot((_silu(gate) * up).astype(BF16), wsd_ref[...])
        yield
        base_ref[...] = x1 + mod_ref[5:6, :] * shared
        h2_lo = (h2 - h2_hi.astype(F32)).astype(BF16)
        logits = _nt_dot(wrh_ref[...], h2_hi)
        yield
        logits = logits + _nt_dot(wrh_ref[...], h2_lo)
        yield
        logits = logits + _nt_dot(wrl_ref[...], h2_hi)
        score_buf[slot] = _sigmoid(logits)

    stages = [chain(), routing()]
    while stages:
        for gen in list(stages):
            try:
                next(gen)
            except StopIteration:
                stages.remove(gen)


def _mixer(x, attn, u, gates, mod3, n2g, wpool_bf, pscale, wua_bf, wup_bf, wout_bf,
           wr_hi, wr_lo, rbias, wsg_bf, wsu_bf, wsd_bf):
    b, s, d = x.shape
    tm = min(TOKEN_TILE, s)
    n_t = s // tm
    n_tiles = b * n_t
    t_all = b * s
    pool_dim = u.shape[-1]
    n_exp = wr_hi.shape[0]
    sh = wsg_bf.shape[1]
    halo_blocks = tm // POOL_HALO
    n_halo = s // POOL_HALO
    full = lambda shape: pl.BlockSpec(shape, lambda g: (0,) * len(shape))

    def tile(g):
        tg = jnp.minimum(g, n_tiles - 1)
        return tg // n_t, lax.rem(tg, n_t)

    def seq_block(g):
        bi, i = tile(g)
        return bi, i, 0

    def halo_prev(g):
        bi, i = tile(g)
        return bi, jnp.maximum(i * halo_blocks - 1, 0), 0

    def halo_next(g):
        bi, i = tile(g)
        return bi, jnp.minimum((i + 1) * halo_blocks, n_halo - 1), 0

    routed = lambda g: (0, jnp.maximum(g - 1, 0))
    t = jnp.arange(s)
    inv_cnt = jnp.stack(
        [1.0 / (jnp.minimum(t + w // 2, s) - jnp.maximum(t - w // 2, 0)).astype(F32) for w in POOL_WINDOWS], axis=1)
    expert_iota = jnp.broadcast_to(jnp.arange(n_exp, dtype=F32)[:, None], (n_exp, tm))
    return pl.pallas_call(
        functools.partial(_mixer_body, n_t=n_t),
        grid=(n_tiles + 1,),
        in_specs=[
            pl.BlockSpec((None, tm, d), seq_block),
            pl.BlockSpec((None, tm, ATTN_DIM), seq_block),
            pl.BlockSpec((None, POOL_HALO, pool_dim), halo_prev),
            pl.BlockSpec((None, tm, pool_dim), seq_block),
            pl.BlockSpec((None, POOL_HALO, pool_dim), halo_next),
            pl.BlockSpec((None, tm, 2 * d), seq_block),
            pl.BlockSpec((None, N_ADA, d), lambda g: (tile(g)[0], 0, 0)),
            pl.BlockSpec((tm, len(POOL_WINDOWS)), lambda g: (tile(g)[1], 0)),
            full((n_exp, tm)),
            full((1, d)),
            full((len(POOL_WINDOWS), pool_dim // len(POOL_WINDOWS), pool_dim // len(POOL_WINDOWS))),
            full((1, pool_dim)),
            full((ATTN_DIM, d)), full((pool_dim, d)), full((d, d)),
            full((n_exp, d)), full((n_exp, d)), full((n_exp, 1)),
            full((d, sh)), full((d, sh)), full((sh, d)),
        ],
        out_specs=[
            pl.BlockSpec((None, tm, d), seq_block),
            pl.BlockSpec((tm, d // 2), lambda g: (jnp.minimum(g, n_tiles - 1), 0)),
            pl.BlockSpec((TOP_K, tm), routed),
            pl.BlockSpec((TOP_K, tm), routed),
            pl.BlockSpec((TOP_K, tm), routed),
            pl.BlockSpec((n_exp, LANES), lambda g: (0, 0)),
        ],
        out_shape=[
            jax.ShapeDtypeStruct((b, s, d), F32),
            jax.ShapeDtypeStruct((t_all, d // 2), U32),
            jax.ShapeDtypeStruct((TOP_K, t_all), I32),
            jax.ShapeDtypeStruct((TOP_K, t_all), F32),
            jax.ShapeDtypeStruct((TOP_K, t_all), I32),
            jax.ShapeDtypeStruct((n_exp, LANES), F32),
        ],
        scratch_shapes=[pltpu.VMEM((2, n_exp, tm), F32)],
        compiler_params=_params("arbitrary"),
        name="mixer_router",
    )(x, attn, u, u, u, gates, mod3, inv_cnt, expert_iota, n2g, wpool_bf, pscale, wua_bf, wup_bf, wout_bf,
      wr_hi, wr_lo, rbias, wsg_bf, wsu_bf, wsd_bf)


def _slots_body(eidx_ref, rank_ref, cnt_ref, pos_ref, bstart_ref, *, block_rows):
    n_exp = cnt_ref.shape[0]
    tm = eidx_ref.shape[1]
    cnt = cnt_ref[...]
    padded = jnp.floor((cnt + (block_rows - 1)) / block_rows) * block_rows
    hi = jnp.floor(padded / 256.0)
    lo = padded - hi * 256.0
    below = (lax.broadcasted_iota(I32, (n_exp, n_exp), 1) < lax.broadcasted_iota(I32, (n_exp, n_exp), 0))
    tri = jnp.where(below, 1.0, 0.0).astype(BF16)
    start = 256.0 * _dot(tri, hi.astype(BF16)) + _dot(tri, lo.astype(BF16))
    end = start + padded
    iota_e = lax.broadcasted_iota(I32, (n_exp, tm), 0)
    start_col = start[:, 0:1]
    rows = []
    for k in range(TOP_K):
        hit = iota_e == eidx_ref[k:k + 1, :]
        rows.append(jnp.sum(jnp.where(hit, start_col, 0.0), axis=0, keepdims=True))
    pos_ref[...] = jnp.concatenate(rows, axis=0).astype(I32) + rank_ref[...]

    @pl.when(pl.program_id(0) == 0)
    def _():
        bstart_ref[0:n_exp, :] = (start / block_rows).astype(I32)
        bstart_ref[n_exp:, :] = jnp.broadcast_to(
            (end[n_exp - 1:n_exp, :] / block_rows).astype(I32), (SUBLANES, LANES))


def _slots(eidx, rank, cnt, block_rows):
    t_all = eidx.shape[1]
    n_exp = cnt.shape[0]
    tm = min(TOKEN_TILE, t_all)
    return pl.pallas_call(
        functools.partial(_slots_body, block_rows=block_rows),
        grid=(t_all // tm,),
        in_specs=[pl.BlockSpec((TOP_K, tm), lambda i: (0, i)),
                  pl.BlockSpec((TOP_K, tm), lambda i: (0, i)),
                  pl.BlockSpec((n_exp, LANES), lambda i: (0, 0))],
        out_specs=[pl.BlockSpec((TOP_K, tm), lambda i: (0, i)),
                   pl.BlockSpec((n_exp + SUBLANES, LANES), lambda i: (0, 0))],
        out_shape=[jax.ShapeDtypeStruct((TOP_K, t_all), I32),
                   jax.ShapeDtypeStruct((n_exp + SUBLANES, LANES), I32)],
        compiler_params=_params("arbitrary"),
        name="slot_positions",
    )(eidx, rank, cnt)


def _expert_body(bstart_ref, xs_hbm, wg_ref, wu_ref, wd_ref, y_hbm,
                 xbuf, ybuf, wgu, wdn, sem_in, sem_out, *, block_rows):
    e = pl.program_id(0)
    n_exp = pl.num_programs(0)
    first, last, total = bstart_ref[e], bstart_ref[e + 1], bstart_ref[n_exp]
    f = wg_ref.shape[1]
    ahead = EXPERT_RING - EXPERT_GROUP

    def rows(i):
        return pl.ds(pl.multiple_of(i * block_rows, block_rows), block_rows)

    def slot_of(i):
        return i & (EXPERT_RING - 1)

    def in_copy(i):
        return pltpu.make_async_copy(xs_hbm.at[rows(i)], xbuf.at[slot_of(i)], sem_in.at[slot_of(i)])

    def out_copy(i):
        return pltpu.make_async_copy(ybuf.at[slot_of(i)], y_hbm.at[rows(i)], sem_out.at[slot_of(i)])

    def start_in(i):
        @pl.when(i < total)
        def _():
            in_copy(i).start()

    def wait_out(i):
        @pl.when(i >= 0)
        def _():
            out_copy(i).wait()

    @pl.when(e == 0)
    def _():
        for j in range(ahead):
            start_in(jnp.int32(j))

    wgu[:, :f] = wg_ref[...].astype(BF16)
    wgu[:, f:] = wu_ref[...].astype(BF16)
    wdn[...] = wd_ref[...].astype(BF16)

    def ffn(i):
        lo, hi = _unpack_rows(xbuf[slot_of(i)])
        x = jnp.concatenate([lo, hi], axis=1).astype(BF16)
        gu = _dot(x, wgu[...])
        return _pack_rows(_dot((_silu(gu[:, :f]) * gu[:, f:]).astype(BF16), wdn[...]))

    def run_blocks(i, n):
        for j in range(n):
            in_copy(i + j).wait()
        for j in range(n):
            start_in(i + ahead + j)
        res = [ffn(i + j) for j in range(n)]
        for j in range(n):
            wait_out(i + j - EXPERT_RING)
        for j in range(n):
            ybuf[slot_of(i + j)] = res[j]
        for j in range(n):
            out_copy(i + j).start()

    n_blk = last - first
    n_groups = n_blk // EXPERT_GROUP

    def group(p, carry):
        run_blocks(first + EXPERT_GROUP * p, EXPERT_GROUP)
        return carry

    lax.fori_loop(0, n_groups, group, 0)

    @pl.when(n_blk - n_groups * EXPERT_GROUP == 1)
    def _():
        run_blocks(last - 1, 1)

    @pl.when(e == n_exp - 1)
    def _():
        for j in range(EXPERT_RING, 0, -1):
            wait_out(total - j)


def _experts(xs, bstart, w_gate, w_up, w_down, block_rows):
    n_slots, dw = xs.shape
    n_exp, d, f = w_gate.shape
    expert = lambda e, bs: (e, 0, 0)
    grid_spec = pltpu.PrefetchScalarGridSpec(
        num_scalar_prefetch=1,
        grid=(n_exp,),
        in_specs=[
            pl.BlockSpec(memory_space=pl.ANY),
            pl.BlockSpec((None, d, f), expert),
            pl.BlockSpec((None, d, f), expert),
            pl.BlockSpec((None, f, d), expert),
        ],
        out_specs=pl.BlockSpec(memory_space=pl.ANY),
        scratch_shapes=[
            pltpu.VMEM((EXPERT_RING, block_rows, dw), U32),
            pltpu.VMEM((EXPERT_RING, block_rows, dw), U32),
            pltpu.VMEM((d, 2 * f), BF16),
            pltpu.VMEM((f, d), BF16),
            pltpu.SemaphoreType.DMA((EXPERT_RING,)),
            pltpu.SemaphoreType.DMA((EXPERT_RING,)),
        ],
    )
    return pl.pallas_call(
        functools.partial(_expert_body, block_rows=block_rows),
        grid_spec=grid_spec,
        out_shape=jax.ShapeDtypeStruct((n_slots, dw), U32),
        compiler_params=_params("arbitrary"),
        name="expert_ffn",
    )(bstart, xs, w_gate, w_up, w_down)


SC_ROWS = 128


def _sc_workers():
    info = plsc.get_sparse_core_info()
    return info.num_cores, info.num_subcores


def _dispatch(h_rows, pos, n_slots):
    t_all, dw = h_rows.shape
    n_cores, n_sub = _sc_workers()
    per_worker = t_all // (n_cores * n_sub)
    assert per_worker * n_cores * n_sub == t_all and per_worker % SC_ROWS == 0
    mesh = plsc.VectorSubcoreMesh(core_axis_name="c", subcore_axis_name="s")

    @functools.partial(
        pl.kernel, mesh=mesh,
        out_type=jax.ShapeDtypeStruct((n_slots, dw), h_rows.dtype),
        scratch_types=[pltpu.VMEM((TOP_K, SC_ROWS), I32), pltpu.VMEM((SC_ROWS, dw), h_rows.dtype),
                       pltpu.SemaphoreType.DMA],
    )
    def body(h_hbm, pos_hbm, out_hbm, idx_v, rows_v, sem):
        base = (lax.axis_index("s") * n_cores + lax.axis_index("c")) * per_worker

        @pl.loop(0, per_worker // SC_ROWS)
        def _(j):
            t0 = pl.multiple_of(base + j * SC_ROWS, SC_ROWS)
            pltpu.sync_copy(pos_hbm.at[:, pl.ds(t0, SC_ROWS)], idx_v)
            pltpu.sync_copy(h_hbm.at[pl.ds(t0, SC_ROWS)], rows_v)
            copies = [pltpu.async_copy(rows_v, out_hbm.at[idx_v.at[k]], sem) for k in range(TOP_K)]
            for cp in copies:
                cp.wait()

    return body(h_rows, pos)


def _gather_back(y_rows, pos):
    top_k, t_all = pos.shape
    dw = y_rows.shape[1]
    n_cores, n_sub = _sc_workers()
    per_worker = t_all // (n_cores * n_sub)
    assert per_worker * n_cores * n_sub == t_all and per_worker % SC_ROWS == 0
    half = SC_ROWS // 2
    mesh = plsc.VectorSubcoreMesh(core_axis_name="c", subcore_axis_name="s")

    @functools.partial(
        pl.kernel, mesh=mesh,
        out_type=jax.ShapeDtypeStruct((top_k, t_all, dw), y_rows.dtype),
        scratch_types=[pltpu.VMEM((top_k, SC_ROWS), I32),
                       pltpu.VMEM((half, dw), y_rows.dtype), pltpu.VMEM((half, dw), y_rows.dtype),
                       pltpu.SemaphoreType.DMA, pltpu.SemaphoreType.DMA, pltpu.SemaphoreType.DMA],
    )
    def body(y_hbm, pos_hbm, out_hbm, idx_v, buf_a, buf_b, sem_g, sem_a, sem_b):
        base = (lax.axis_index("s") * n_cores + lax.axis_index("c")) * per_worker
        bufs, sems = (buf_a, buf_b), (sem_a, sem_b)

        @pl.loop(0, per_worker // SC_ROWS)
        def _(j):
            t0 = pl.multiple_of(base + j * SC_ROWS, SC_ROWS)
            pltpu.sync_copy(pos_hbm.at[:, pl.ds(t0, SC_ROWS)], idx_v)
            pending = [None, None]
            for step in range(2 * top_k):
                k, h = step // 2, step % 2
                slot = step % 2
                if pending[slot] is not None:
                    pending[slot].wait()
                pltpu.async_copy(y_hbm.at[idx_v.at[k, pl.ds(h * half, half)]], bufs[slot], sem_g).wait()
                pending[slot] = pltpu.async_copy(
                    bufs[slot], out_hbm.at[k, pl.ds(t0 + h * half, half)], sems[slot])
            for p in pending:
                p.wait()

    return body(y_rows, pos)


def _combine_body(yg_ref, w_ref, base_ref, mod_ref, fg_ref, o_ref):
    acc_lo = acc_hi = None
    for k in range(TOP_K):
        lo, hi = _unpack_rows(yg_ref[k])
        wk = w_ref[:, k:k + 1]
        acc_lo = lo * wk if acc_lo is None else acc_lo + lo * wk
        acc_hi = hi * wk if acc_hi is None else acc_hi + hi * wk
    acc = jnp.concatenate([acc_lo, acc_hi], axis=1)
    x2 = base_ref[...] + mod_ref[5:6, :] * acc
    ms = jnp.mean(x2 * x2, axis=-1, keepdims=True)
    o_ref[...] = (x2 * lax.rsqrt(ms + NORM_EPS)) * fg_ref[...]


def _combine(yg, wts_t, base, mod3, final_g):
    b, s, d = base.shape
    tm = min(256, s)
    n_t = s // tm
    return pl.pallas_call(
        _combine_body,
        grid=(b, n_t),
        in_specs=[pl.BlockSpec((TOP_K, tm, d // 2), lambda bi, i: (0, bi * n_t + i, 0)),
                  pl.BlockSpec((tm, TOP_K), lambda bi, i: (bi * n_t + i, 0)),
                  pl.BlockSpec((None, tm, d), lambda bi, i: (bi, i, 0)),
                  pl.BlockSpec((None, N_ADA, d), lambda bi, i: (bi, 0, 0)),
                  pl.BlockSpec((1, d), lambda bi, i: (0, 0))],
        out_specs=pl.BlockSpec((None, tm, d), lambda bi, i: (bi, i, 0)),
        out_shape=jax.ShapeDtypeStruct((b, s, d), F32),
        compiler_params=_params("parallel", "parallel"),
        name="combine_norm",
    )(yg, wts_t, base, mod3, final_g)


def _rope_tables(seq_len):
    rows = seq_len // GRID_W
    row = jnp.repeat(jnp.arange(rows), GRID_W).astype(F32)
    col = jnp.tile(jnp.arange(GRID_W), rows).astype(F32)
    n_freq = HEAD_DIM // 4
    inv = ROPE_THETA ** (-jnp.arange(n_freq, dtype=F32) / n_freq)
    ar, ac = row[:, None] * inv, col[:, None] * inv
    zeros = jnp.zeros_like(ar)
    reps = LANES // HEAD_DIM
    cos = jnp.tile(jnp.concatenate([jnp.cos(ar), jnp.cos(ar), jnp.cos(ac), jnp.cos(ac)], 1), (1, reps))
    sin_a = jnp.tile(jnp.concatenate([-jnp.sin(ar), zeros, -jnp.sin(ac), zeros], 1), (1, reps))
    sin_b = jnp.tile(jnp.concatenate([zeros, jnp.sin(ar), zeros, jnp.sin(ac)], 1), (1, reps))
    return cos, sin_a, sin_b


def kernel(x, c, ctx, c_ctx, w_ada, b_ada, norm1_g, w_in, attn_sink, w_pool, pool_scale,
           w_up_attn, w_up_pool, w_out, norm2_g, w_router, router_bias,
           w_exp_gate, w_exp_up, w_exp_down, w_sh_gate, w_sh_up, w_sh_down, final_g):
    b, s, d = x.shape
    assert w_ada.shape[0] == 1, "single-layer block"
    assert s % ATTN_BLOCK == 0 and s % GRID_W == 0 and d % LANES == 0
    pool_dim = w_up_pool.shape[1]
    n_exp = w_router.shape[-1]
    t_all = b * s

    pad_rows = -(-(b + 1) // SUBLANES) * SUBLANES
    c_rows = jnp.concatenate([c, c_ctx[None, :], jnp.zeros((pad_rows - b - 1, d), F32)], axis=0)
    mod3 = _ada(c_rows, w_ada[0], b_ada[0]).reshape(pad_rows, N_ADA, d)

    w_in_bf = w_in[0].astype(BF16)
    g1 = norm1_g[0].reshape(1, d)
    cos, sin_a, sin_b = _rope_tables(s)
    q, k, v, u, gates = _inproj(x, mod3, g1, w_in_bf, cos, sin_a, sin_b, pool_dim)
    kc, vc = _ctxkv(ctx, mod3[b], g1, w_in_bf[:, ATTN_DIM:ATTN_DIM + 2 * KV_DIM])
    attn = _attention(q, k, v, kc, vc, attn_sink[0])

    w_r_t = w_router[0].T
    w_r_hi = w_r_t.astype(BF16)
    w_r_lo = (w_r_t - w_r_hi.astype(F32)).astype(BF16)
    base, h2, eidx, wts, rank, cnt = _mixer(
        x, attn, u, gates, mod3, norm2_g[0].reshape(1, d), w_pool[0].astype(BF16),
        pool_scale[0].reshape(1, pool_dim), w_up_attn[0].astype(BF16), w_up_pool[0].astype(BF16),
        w_out[0].astype(BF16), w_r_hi, w_r_lo, router_bias[0].reshape(n_exp, 1),
        w_sh_gate[0].astype(BF16), w_sh_up[0].astype(BF16), w_sh_down[0].astype(BF16))

    br = EXPERT_BLOCK_ROWS
    n_blocks = (t_all * TOP_K + n_exp * (br - 1) + br - 1) // br
    n_slots = n_blocks * br
    pos, bstart = _slots(eidx, rank, cnt, br)
    xs = _dispatch(h2, pos, n_slots)
    y = _experts(xs, bstart[:n_exp + 1, 0], w_exp_gate[0], w_exp_up[0], w_exp_down[0], br)
    yg = _gather_back(y, pos)

    return _combine(yg, wts.T, base, mod3, final_g.reshape(1, d))
```

```python
import functools

import jax
import jax.numpy as jnp
from jax import lax
from jax.experimental import pallas as pl
from jax.experimental.pallas import tpu as pltpu
from jax.experimental.pallas import tpu_sc as plsc

F32 = jnp.float32
BF16 = jnp.bfloat16
I32 = jnp.int32
U32 = jnp.uint32

GRID_W = 64
HEAD_DIM = 64
N_Q_HEADS = 8
N_KV_HEADS = 2
Q_PER_KV = N_Q_HEADS // N_KV_HEADS
ATTN_DIM = N_Q_HEADS * HEAD_DIM
KV_DIM = N_KV_HEADS * HEAD_DIM
ATTN_BLOCK = 128
ATTN_SCALE = HEAD_DIM ** -0.5
ROPE_THETA = 10000.0
POOL_WINDOWS = (2, 4, 8, 16)
POOL_HALO = 8
N_EXPERT_GROUPS = 8
TOPK_GROUPS = 4
TOP_K = 8
ROUTED_SCALE = 2.5
N_ADA = 6
NORM_EPS = 1e-6
NEG_INF = -1e30
LANES = 128
SUBLANES = 8
ATTN_QBLOCKS = 4
TOKEN_TILE = 512
EXPERT_BLOCK_ROWS = 256
EXPERT_GROUP = 4
EXPERT_RING = 16
VMEM_LIMIT = 56 * 1024 * 1024


def _sigmoid(x):
    return 1.0 / (1.0 + jnp.exp(-x))


def _silu(x):
    return x * _sigmoid(x)


def _nt_dot(a, b):
    return lax.dot_general(a, b, (((1,), (1,)), ((), ())), preferred_element_type=F32)


def _dot(a, b):
    return jnp.dot(a, b, preferred_element_type=F32)


def _pack_rows(x):
    n = x.shape[1] // 2
    bits = lax.bitcast_convert_type(x.astype(BF16).astype(F32), U32)
    return (bits[:, :n] >> 16) | (bits[:, n:] & jnp.uint32(0xFFFF0000))


def _unpack_rows(w):
    lo = lax.bitcast_convert_type(w << 16, F32)
    hi = lax.bitcast_convert_type(w & jnp.uint32(0xFFFF0000), F32)
    return lo, hi


def _params(*sem):
    return pltpu.CompilerParams(dimension_semantics=sem, vmem_limit_bytes=VMEM_LIMIT)


def _ada_body(c_ref, w_ref, b_ref, o_ref):
    s = _silu(c_ref[...])
    o_ref[...] = jnp.dot(s, w_ref[...], preferred_element_type=F32,
                         precision=lax.Precision.HIGHEST) + b_ref[...]


def _ada(c_rows, w_ada, b_ada):
    rows, d = c_rows.shape
    n = w_ada.shape[1]
    bn = d
    return pl.pallas_call(
        _ada_body,
        grid=(n // bn,),
        in_specs=[pl.BlockSpec((rows, d), lambda j: (0, 0)),
                  pl.BlockSpec((d, bn), lambda j: (0, j)),
                  pl.BlockSpec((1, bn), lambda j: (0, j))],
        out_specs=pl.BlockSpec((rows, bn), lambda j: (0, j)),
        out_shape=jax.ShapeDtypeStruct((rows, n), F32),
        compiler_params=_params("arbitrary"),
        name="ada_mod",
    )(c_rows, w_ada, b_ada.reshape(1, n))


def _norm_mod(x, g, shift, scale):
    ms = jnp.mean(x * x, axis=-1, keepdims=True)
    y = x * lax.rsqrt(ms + NORM_EPS)
    return (y * g) * (1.0 + scale) + shift


def _rope(t, cos, sin_a, sin_b):
    return (t * cos + pltpu.roll(t, LANES - HEAD_DIM // 4, 1) * sin_a
            + pltpu.roll(t, HEAD_DIM // 4, 1) * sin_b)


def _inproj_body(x_ref, mod_ref, g_ref, w_ref, cos_ref, sa_ref, sb_ref,
                 q_ref, k_ref, v_ref, u_ref, gate_ref):
    h = _norm_mod(x_ref[...], g_ref[...], mod_ref[0:1, :], mod_ref[1:2, :]).astype(BF16)
    cos, sa, sb = cos_ref[...], sa_ref[...], sb_ref[...]
    heads_per_chunk = LANES // HEAD_DIM
    for j in range(ATTN_DIM // LANES):
        t = _rope(_dot(h, w_ref[:, j * LANES:(j + 1) * LANES]), cos, sa, sb) * ATTN_SCALE
        t = t.astype(BF16)
        for i in range(heads_per_chunk):
            q_ref[heads_per_chunk * j + i] = t[:, i * HEAD_DIM:(i + 1) * HEAD_DIM]
    k_off = ATTN_DIM
    t = _rope(_dot(h, w_ref[:, k_off:k_off + KV_DIM]), cos, sa, sb).astype(BF16)
    for i in range(N_KV_HEADS):
        k_ref[i] = t[:, i * HEAD_DIM:(i + 1) * HEAD_DIM]
    v_off = k_off + KV_DIM
    t = _dot(h, w_ref[:, v_off:v_off + KV_DIM]).astype(BF16)
    for i in range(N_KV_HEADS):
        v_ref[i] = t[:, i * HEAD_DIM:(i + 1) * HEAD_DIM]
    p_off = v_off + KV_DIM
    pool_dim = u_ref.shape[-1]
    u_ref[...] = _dot(h, w_ref[:, p_off:p_off + pool_dim])
    g_off = p_off + pool_dim
    gate_dim = gate_ref.shape[-1]
    chunk = 512
    for j in range(gate_dim // chunk):
        t = _dot(h, w_ref[:, g_off + j * chunk:g_off + (j + 1) * chunk])
        gate_ref[:, j * chunk:(j + 1) * chunk] = _sigmoid(t).astype(BF16)


def _inproj(x, mod3, g, w_in_bf, cos, sa, sb, pool_dim):
    b, s, d = x.shape
    tm = min(TOKEN_TILE, s)
    in_dim = w_in_bf.shape[1]
    gate_dim = in_dim - ATTN_DIM - 2 * KV_DIM - pool_dim
    grid = (b, s // tm)
    return pl.pallas_call(
        _inproj_body,
        grid=grid,
        in_specs=[
            pl.BlockSpec((None, tm, d), lambda bi, i: (bi, i, 0)),
            pl.BlockSpec((None, N_ADA, d), lambda bi, i: (bi, 0, 0)),
            pl.BlockSpec((1, d), lambda bi, i: (0, 0)),
            pl.BlockSpec((d, in_dim), lambda bi, i: (0, 0)),
            pl.BlockSpec((tm, LANES), lambda bi, i: (i, 0)),
            pl.BlockSpec((tm, LANES), lambda bi, i: (i, 0)),
            pl.BlockSpec((tm, LANES), lambda bi, i: (i, 0)),
        ],
        out_specs=[
            pl.BlockSpec((None, N_Q_HEADS, tm, HEAD_DIM), lambda bi, i: (bi, 0, i, 0)),
            pl.BlockSpec((None, N_KV_HEADS, tm, HEAD_DIM), lambda bi, i: (bi, 0, i, 0)),
            pl.BlockSpec((None, N_KV_HEADS, tm, HEAD_DIM), lambda bi, i: (bi, 0, i, 0)),
            pl.BlockSpec((None, tm, pool_dim), lambda bi, i: (bi, i, 0)),
            pl.BlockSpec((None, tm, gate_dim), lambda bi, i: (bi, i, 0)),
        ],
        out_shape=[
            jax.ShapeDtypeStruct((b, N_Q_HEADS, s, HEAD_DIM), BF16),
            jax.ShapeDtypeStruct((b, N_KV_HEADS, s, HEAD_DIM), BF16),
            jax.ShapeDtypeStruct((b, N_KV_HEADS, s, HEAD_DIM), BF16),
            jax.ShapeDtypeStruct((b, s, pool_dim), F32),
            jax.ShapeDtypeStruct((b, s, gate_dim), BF16),
        ],
        compiler_params=_params("parallel", "parallel"),
        name="in_proj",
    )(x, mod3, g, w_in_bf, cos, sa, sb)


def _ctxkv_body(ctx_ref, mod_ref, g_ref, w_ref, kc_ref, vc_ref):
    h = _norm_mod(ctx_ref[...], g_ref[...], mod_ref[0:1, :], mod_ref[1:2, :]).astype(BF16)
    t = _dot(h, w_ref[...]).astype(BF16)
    for i in range(N_KV_HEADS):
        kc_ref[i] = t[:, i * HEAD_DIM:(i + 1) * HEAD_DIM]
        vc_ref[i] = t[:, KV_DIM + i * HEAD_DIM:KV_DIM + (i + 1) * HEAD_DIM]


def _ctxkv(ctx, mod_c, g, w_kv_bf):
    b, c, d = ctx.shape
    out = jax.ShapeDtypeStruct((b, N_KV_HEADS, c, HEAD_DIM), BF16)
    spec = pl.BlockSpec((None, N_KV_HEADS, c, HEAD_DIM), lambda bi: (bi, 0, 0, 0))
    return pl.pallas_call(
        _ctxkv_body,
        grid=(b,),
        in_specs=[pl.BlockSpec((None, c, d), lambda bi: (bi, 0, 0)),
                  pl.BlockSpec((N_ADA, d), lambda bi: (0, 0)),
                  pl.BlockSpec((1, d), lambda bi: (0, 0)),
                  pl.BlockSpec((d, 2 * KV_DIM), lambda bi: (0, 0))],
        out_specs=[spec, spec],
        out_shape=[out, out],
        compiler_params=_params("parallel"),
        name="ctx_kv",
    )(ctx, mod_c, g, w_kv_bf)


def _attn_body(sink_ref, q_ref, *refs):
    nkb = ATTN_QBLOCKS + 2
    k_refs, v_refs = refs[:nkb], refs[nkb:2 * nkb]
    kc_ref, vc_ref, lo_ref, hi_ref, o_ref = refs[2 * nkb:]
    n = pl.program_id(1)
    last = pl.num_programs(1) - 1
    rows = Q_PER_KV * ATTN_BLOCK
    row_head = lax.broadcasted_iota(I32, (rows, 1), 0) // ATTN_BLOCK
    for qb in range(ATTN_QBLOCKS):
        bias_lo, bias_hi = lo_ref[...], hi_ref[...]
        if qb == 0:
            bias_lo = bias_lo + jnp.where(n == 0, NEG_INF, 0.0)
        if qb == ATTN_QBLOCKS - 1:
            bias_hi = bias_hi + jnp.where(n == last, NEG_INF, 0.0)
        q_rows = slice(qb * ATTN_BLOCK, (qb + 1) * ATTN_BLOCK)
        outs = []
        for kk in range(N_KV_HEADS):
            qs = q_ref[kk * Q_PER_KV:(kk + 1) * Q_PER_KV, q_rows, :].reshape(rows, HEAD_DIM)
            s0 = _nt_dot(qs, k_refs[qb][kk]) + bias_lo
            s1 = _nt_dot(qs, k_refs[qb + 1][kk])
            s2 = _nt_dot(qs, k_refs[qb + 2][kk]) + bias_hi
            sc = _nt_dot(qs, kc_ref[kk])
            sink = jnp.zeros((rows, 1), F32)
            for g in range(Q_PER_KV):
                sink = jnp.where(row_head == g, sink_ref[kk * Q_PER_KV + g], sink)
            n_ctx = sc.shape[1] // ATTN_BLOCK
            folded = jnp.maximum(jnp.maximum(s0, s1), s2)
            for j in range(n_ctx):
                folded = jnp.maximum(folded, sc[:, j * ATTN_BLOCK:(j + 1) * ATTN_BLOCK])
            m = jnp.maximum(jnp.max(folded, axis=1, keepdims=True), sink)
            p0, p1, p2, pc = jnp.exp(s0 - m), jnp.exp(s1 - m), jnp.exp(s2 - m), jnp.exp(sc - m)
            folded = p0 + p1 + p2
            for j in range(n_ctx):
                folded = folded + pc[:, j * ATTN_BLOCK:(j + 1) * ATTN_BLOCK]
            denom = jnp.sum(folded, axis=1, keepdims=True) + jnp.exp(sink - m)
            o = (_dot(p0.astype(BF16), v_refs[qb][kk]) + _dot(p1.astype(BF16), v_refs[qb + 1][kk])
                 + _dot(p2.astype(BF16), v_refs[qb + 2][kk]) + _dot(pc.astype(BF16), vc_ref[kk]))
            o = o / denom
            outs += [o[g * ATTN_BLOCK:(g + 1) * ATTN_BLOCK] for g in range(Q_PER_KV)]
        o_ref[q_rows, :] = jnp.concatenate(outs, axis=1).astype(BF16)


def _attention(q, k, v, kc, vc, sink):
    b, _, s, _ = q.shape
    c = kc.shape[2]
    nb = s // ATTN_BLOCK
    assert nb % ATTN_QBLOCKS == 0 and c % ATTN_BLOCK == 0
    rows = Q_PER_KV * ATTN_BLOCK
    q_rows = ATTN_QBLOCKS * ATTN_BLOCK
    qi = jnp.tile(jnp.arange(ATTN_BLOCK), Q_PER_KV)[:, None]
    kj = jnp.arange(ATTN_BLOCK)[None, :]
    bias_lo = jnp.where(kj >= qi, 0.0, NEG_INF).astype(F32)
    bias_hi = jnp.where(kj <= qi, 0.0, NEG_INF).astype(F32)

    def kv_spec(i):
        return pl.BlockSpec(
            (None, N_KV_HEADS, ATTN_BLOCK, HEAD_DIM),
            lambda bi, n: (bi, 0, jnp.clip(ATTN_QBLOCKS * n - 1 + i, 0, nb - 1), 0))

    kv_specs = [kv_spec(i) for i in range(ATTN_QBLOCKS + 2)]
    ctx_spec = pl.BlockSpec((None, N_KV_HEADS, c, HEAD_DIM), lambda bi, n: (bi, 0, 0, 0))
    bias_spec = pl.BlockSpec((rows, ATTN_BLOCK), lambda bi, n: (0, 0))
    return pl.pallas_call(
        _attn_body,
        grid=(b, nb // ATTN_QBLOCKS),
        in_specs=[
            pl.BlockSpec(memory_space=pltpu.SMEM),
            pl.BlockSpec((None, N_Q_HEADS, q_rows, HEAD_DIM), lambda bi, n: (bi, 0, n, 0)),
            *kv_specs, *kv_specs, ctx_spec, ctx_spec, bias_spec, bias_spec,
        ],
        out_specs=pl.BlockSpec((None, q_rows, ATTN_DIM), lambda bi, n: (bi, n, 0)),
        out_shape=jax.ShapeDtypeStruct((b, s, ATTN_DIM), BF16),
        compiler_params=_params("parallel", "parallel"),
        name="window_attn",
    )(sink, q, *([k] * (ATTN_QBLOCKS + 2)), *([v] * (ATTN_QBLOCKS + 2)), kc, vc, bias_lo, bias_hi)


def _pool_delta(ue, inv_cnt):
    n_ext = ue.shape[0]
    tm = n_ext - 2 * POOL_HALO
    group_dim = ue.shape[1] // len(POOL_WINDOWS)
    outs = []
    for g, w in enumerate(POOL_WINDOWS):
        xs = ue[:, g * group_dim:(g + 1) * group_dim]
        acc = xs + pltpu.roll(xs, 1, 0)
        step = 1
        while 2 * step < w:
            acc = pltpu.roll(acc, step, 0) + pltpu.roll(acc, n_ext - step, 0)
            step *= 2
        core = slice(POOL_HALO, POOL_HALO + tm)
        outs.append(acc[core] * inv_cnt[:, g:g + 1] - xs[core])
    return outs


def _route_steps(s, sel, iota_ref, out):
    n_exp = s.shape[0]
    per_group = n_exp // N_EXPERT_GROUPS
    neg = float("-inf")
    iota_g = iota_ref[0:per_group, :]
    scores = []
    for g in range(N_EXPERT_GROUPS):
        blk = sel[g * per_group:(g + 1) * per_group]
        m1 = jnp.max(blk, axis=0, keepdims=True)
        first = jnp.min(jnp.where(blk == m1, iota_g, float(per_group)), axis=0, keepdims=True)
        m2 = jnp.max(jnp.where(iota_g == first, neg, blk), axis=0, keepdims=True)
        scores.append(m1 + m2)
        if g % 2 == 1:
            yield
    gs = jnp.concatenate(scores, axis=0)
    iota_ng = iota_ref[0:N_EXPERT_GROUPS, :]
    gsel = jnp.zeros(iota_ng.shape, F32)
    for _ in range(TOPK_GROUPS):
        m = jnp.max(gs, axis=0, keepdims=True)
        first = jnp.min(jnp.where(gs == m, iota_ng, float(N_EXPERT_GROUPS)), axis=0, keepdims=True)
        hit = iota_ng == first
        gsel = jnp.where(hit, 1.0, gsel)
        gs = jnp.where(hit, neg, gs)
    cur = jnp.concatenate(
        [jnp.where(gsel[g:g + 1] > 0.0, sel[g * per_group:(g + 1) * per_group], NEG_INF)
         for g in range(N_EXPERT_GROUPS)], axis=0)
    yield
    chosen = jnp.zeros(s.shape, F32)
    ids, aff = [], []
    for _ in range(TOP_K):
        m = jnp.max(cur, axis=0, keepdims=True)
        first = jnp.min(jnp.where(cur == m, iota_ref[...], float(n_exp)), axis=0, keepdims=True)
        hit = iota_ref[...] == first
        ids.append(first)
        aff.append(jnp.sum(jnp.where(hit, s, 0.0), axis=0, keepdims=True))
        chosen = jnp.where(hit, 1.0, chosen)
        cur = jnp.where(hit, neg, cur)
        yield
    out.update(ids=ids, aff=aff, chosen=chosen)


def _mixer_body(x_ref, attn_ref, up_ref, u_ref, un_ref, gate_ref, mod_ref, icnt_ref, iota_ref, n2g_ref,
                wpool_ref, pscale_ref, wua_ref, wup_ref, wout_ref, wrh_ref, wrl_ref, rbias_ref,
                wsg_ref, wsu_ref, wsd_ref,
                base_ref, h2_ref, eidx_ref, wts_ref, rank_ref, cnt_ref, score_buf, *, n_t):
    g = pl.program_id(0)
    n_tiles = pl.num_programs(0) - 1
    ti = lax.rem(jnp.minimum(g, n_tiles - 1), n_t)
    slot = g & 1
    tm, d = x_ref.shape
    pool_dim = u_ref.shape[-1]
    n_exp = score_buf.shape[1]

    @pl.when(g == 0)
    def _():
        cnt_ref[...] = jnp.zeros_like(cnt_ref)
        score_buf[1] = jnp.zeros((n_exp, tm), F32)

    def routing():
        s = score_buf[1 - slot]
        sel = s + rbias_ref[...]
        res = {}
        yield from _route_steps(s, sel, iota_ref, res)
        ids, aff, chosen = res["ids"], res["aff"], res["chosen"]
        total = aff[0]
        for a in aff[1:]:
            total = total + a
        wts_ref[...] = jnp.concatenate([a / total * ROUTED_SCALE for a in aff], axis=0)
        eidx_ref[...] = jnp.concatenate(ids, axis=0).astype(I32)
        yield
        before = (lax.broadcasted_iota(I32, (tm, tm), 0) < lax.broadcasted_iota(I32, (tm, tm), 1))
        prefix = _dot(chosen.astype(BF16), jnp.where(before, 1.0, 0.0).astype(BF16))
        rank_dense = prefix + cnt_ref[:, 0:1]
        rank_ref[...] = jnp.concatenate(
            [jnp.sum(jnp.where(iota_ref[...] == i, rank_dense, 0.0), axis=0, keepdims=True) for i in ids],
            axis=0).astype(I32)
        real = jnp.where(g > 0, 1.0, 0.0)
        cnt_ref[...] = cnt_ref[...] + real * jnp.sum(chosen, axis=1, keepdims=True)

    def chain():
        zero_halo = jnp.zeros((POOL_HALO, pool_dim), F32)
        u_prev = jnp.where(ti == 0, zero_halo, up_ref[...])
        u_next = jnp.where(ti == n_t - 1, zero_halo, un_ref[...])
        ue = jnp.concatenate([u_prev, u_ref[...], u_next], axis=0)
        deltas = _pool_delta(ue, icnt_ref[...])
        attn_up = _dot(attn_ref[...], wua_ref[...])
        yield
        pool = jnp.concatenate(
            [_dot(dl.astype(BF16), wpool_ref[i]) for i, dl in enumerate(deltas)], axis=1)
        pool = (pool * pscale_ref[...]).astype(BF16)
        yield
        pool_up = _dot(pool, wup_ref[...])
        yield
        y = gate_ref[:, :d] * attn_up.astype(BF16) + gate_ref[:, d:] * pool_up.astype(BF16)
        mix = _dot(y, wout_ref[...])
        yield
        x1 = x_ref[...] + mod_ref[2:3, :] * mix
        h2 = _norm_mod(x1, n2g_ref[...], mod_ref[3:4, :], mod_ref[4:5, :])
        h2_hi = h2.astype(BF16)
        h2_ref[...] = _pack_rows(h2)
        gate = _dot(h2_hi, wsg_ref[...])
        yield
        up = _dot(h2_hi, wsu_ref[...])
        yield
        shared = _dot((_silu(gate) * up).astype(BF16), wsd_ref[...])
        yield
        base_ref[...] = x1 + mod_ref[5:6, :] * shared
        h2_lo = (h2 - h2_hi.astype(F32)).astype(BF16)
        logits = _nt_dot(wrh_ref[...], h2_hi)
        yield
        logits = logits + _nt_dot(wrh_ref[...], h2_lo)
        yield
        logits = logits + _nt_dot(wrl_ref[...], h2_hi)
        score_buf[slot] = _sigmoid(logits)

    stages = [chain(), routing()]
    while stages:
        for gen in list(stages):
            try:
                next(gen)
            except StopIteration:
                stages.remove(gen)


def _mixer(x, attn, u, gates, mod3, n2g, wpool_bf, pscale, wua_bf, wup_bf, wout_bf,
           wr_hi, wr_lo, rbias, wsg_bf, wsu_bf, wsd_bf):
    b, s, d = x.shape
    tm = min(TOKEN_TILE, s)
    n_t = s // tm
    n_tiles = b * n_t
    t_all = b * s
    pool_dim = u.shape[-1]
    n_exp = wr_hi.shape[0]
    sh = wsg_bf.shape[1]
    halo_blocks = tm // POOL_HALO
    n_halo = s // POOL_HALO
    full = lambda shape: pl.BlockSpec(shape, lambda g: (0,) * len(shape))

    def tile(g):
        tg = jnp.minimum(g, n_tiles - 1)
        return tg // n_t, lax.rem(tg, n_t)

    def seq_block(g):
        bi, i = tile(g)
        return bi, i, 0

    def halo_prev(g):
        bi, i = tile(g)
        return bi, jnp.maximum(i * halo_blocks - 1, 0), 0

    def halo_next(g):
        bi, i = tile(g)
        return bi, jnp.minimum((i + 1) * halo_blocks, n_halo - 1), 0

    routed = lambda g: (0, jnp.maximum(g - 1, 0))
    t = jnp.arange(s)
    inv_cnt = jnp.stack(
        [1.0 / (jnp.minimum(t + w // 2, s) - jnp.maximum(t - w // 2, 0)).astype(F32) for w in POOL_WINDOWS], axis=1)
    expert_iota = jnp.broadcast_to(jnp.arange(n_exp, dtype=F32)[:, None], (n_exp, tm))
    return pl.pallas_call(
        functools.partial(_mixer_body, n_t=n_t),
        grid=(n_tiles + 1,),
        in_specs=[
            pl.BlockSpec((None, tm, d), seq_block),
            pl.BlockSpec((None, tm, ATTN_DIM), seq_block),
            pl.BlockSpec((None, POOL_HALO, pool_dim), halo_prev),
            pl.BlockSpec((None, tm, pool_dim), seq_block),
            pl.BlockSpec((None, POOL_HALO, pool_dim), halo_next),
            pl.BlockSpec((None, tm, 2 * d), seq_block),
            pl.BlockSpec((None, N_ADA, d), lambda g: (tile(g)[0], 0, 0)),
            pl.BlockSpec((tm, len(POOL_WINDOWS)), lambda g: (tile(g)[1], 0)),
            full((n_exp, tm)),
            full((1, d)),
            full((len(POOL_WINDOWS), pool_dim // len(POOL_WINDOWS), pool_dim // len(POOL_WINDOWS))),
            full((1, pool_dim)),
            full((ATTN_DIM, d)), full((pool_dim, d)), full((d, d)),
            full((n_exp, d)), full((n_exp, d)), full((n_exp, 1)),
            full((d, sh)), full((d, sh)), full((sh, d)),
        ],
        out_specs=[
            pl.BlockSpec((None, tm, d), seq_block),
            pl.BlockSpec((tm, d // 2), lambda g: (jnp.minimum(g, n_tiles - 1), 0)),
            pl.BlockSpec((TOP_K, tm), routed),
            pl.BlockSpec((TOP_K, tm), routed),
            pl.BlockSpec((TOP_K, tm), routed),
            pl.BlockSpec((n_exp, LANES), lambda g: (0, 0)),
        ],
        out_shape=[
            jax.ShapeDtypeStruct((b, s, d), F32),
            jax.ShapeDtypeStruct((t_all, d // 2), U32),
            jax.ShapeDtypeStruct((TOP_K, t_all), I32),
            jax.ShapeDtypeStruct((TOP_K, t_all), F32),
            jax.ShapeDtypeStruct((TOP_K, t_all), I32),
            jax.ShapeDtypeStruct((n_exp, LANES), F32),
        ],
        scratch_shapes=[pltpu.VMEM((2, n_exp, tm), F32)],
        compiler_params=_params("arbitrary"),
        name="mixer_router",
    )(x, attn, u, u, u, gates, mod3, inv_cnt, expert_iota, n2g, wpool_bf, pscale, wua_bf, wup_bf, wout_bf,
      wr_hi, wr_lo, rbias, wsg_bf, wsu_bf, wsd_bf)


def _slots_body(eidx_ref, rank_ref, cnt_ref, pos_ref, bstart_ref, *, block_rows):
    n_exp = cnt_ref.shape[0]
    tm = eidx_ref.shape[1]
    cnt = cnt_ref[...]
    padded = jnp.floor((cnt + (block_rows - 1)) / block_rows) * block_rows
    hi = jnp.floor(padded / 256.0)
    lo = padded - hi * 256.0
    below = (lax.broadcasted_iota(I32, (n_exp, n_exp), 1) < lax.broadcasted_iota(I32, (n_exp, n_exp), 0))
    tri = jnp.where(below, 1.0, 0.0).astype(BF16)
    start = 256.0 * _dot(tri, hi.astype(BF16)) + _dot(tri, lo.astype(BF16))
    end = start + padded
    iota_e = lax.broadcasted_iota(I32, (n_exp, tm), 0)
    start_col = start[:, 0:1]
    rows = []
    for k in range(TOP_K):
        hit = iota_e == eidx_ref[k:k + 1, :]
        rows.append(jnp.sum(jnp.where(hit, start_col, 0.0), axis=0, keepdims=True))
    pos_ref[...] = jnp.concatenate(rows, axis=0).astype(I32) + rank_ref[...]

    @pl.when(pl.program_id(0) == 0)
    def _():
        bstart_ref[0:n_exp, :] = (start / block_rows).astype(I32)
        bstart_ref[n_exp:, :] = jnp.broadcast_to(
            (end[n_exp - 1:n_exp, :] / block_rows).astype(I32), (SUBLANES, LANES))


def _slots(eidx, rank, cnt, block_rows):
    t_all = eidx.shape[1]
    n_exp = cnt.shape[0]
    tm = min(TOKEN_TILE, t_all)
    return pl.pallas_call(
        functools.partial(_slots_body, block_rows=block_rows),
        grid=(t_all // tm,),
        in_specs=[pl.BlockSpec((TOP_K, tm), lambda i: (0, i)),
                  pl.BlockSpec((TOP_K, tm), lambda i: (0, i)),
                  pl.BlockSpec((n_exp, LANES), lambda i: (0, 0))],
        out_specs=[pl.BlockSpec((TOP_K, tm), lambda i: (0, i)),
                   pl.BlockSpec((n_exp + SUBLANES, LANES), lambda i: (0, 0))],
        out_shape=[jax.ShapeDtypeStruct((TOP_K, t_all), I32),
                   jax.ShapeDtypeStruct((n_exp + SUBLANES, LANES), I32)],
        compiler_params=_params("arbitrary"),
        name="slot_positions",
    )(eidx, rank, cnt)


def _expert_body(bstart_ref, xs_hbm, wg_ref, wu_ref, wd_ref, y_hbm,
                 xbuf, ybuf, wgu, wdn, sem_in, sem_out, *, block_rows):
    e = pl.program_id(0)
    n_exp = pl.num_programs(0)
    first, last, total = bstart_ref[e], bstart_ref[e + 1], bstart_ref[n_exp]
    f = wg_ref.shape[1]
    ahead = EXPERT_RING - EXPERT_GROUP

    def rows(i):
        return pl.ds(pl.multiple_of(i * block_rows, block_rows), block_rows)

    def slot_of(i):
        return i & (EXPERT_RING - 1)

    def in_copy(i):
        return pltpu.make_async_copy(xs_hbm.at[rows(i)], xbuf.at[slot_of(i)], sem_in.at[slot_of(i)])

    def out_copy(i):
        return pltpu.make_async_copy(ybuf.at[slot_of(i)], y_hbm.at[rows(i)], sem_out.at[slot_of(i)])

    def start_in(i):
        @pl.when(i < total)
        def _():
            in_copy(i).start()

    def wait_out(i):
        @pl.when(i >= 0)
        def _():
            out_copy(i).wait()

    @pl.when(e == 0)
    def _():
        for j in range(ahead):
            start_in(jnp.int32(j))

    wgu[:, :f] = wg_ref[...].astype(BF16)
    wgu[:, f:] = wu_ref[...].astype(BF16)
    wdn[...] = wd_ref[...].astype(BF16)

    def ffn(i, n):
        parts = []
        for j in range(n):
            lo, hi = _unpack_rows(xbuf[slot_of(i + j)])
            parts.append(jnp.concatenate([lo, hi], axis=1).astype(BF16))
        x = jnp.concatenate(parts, axis=0)
        gu = _dot(x, wgu[...])
        y = _pack_rows(_dot((_silu(gu[:, :f]) * gu[:, f:]).astype(BF16), wdn[...]))
        return [y[j * block_rows:(j + 1) * block_rows] for j in range(n)]

    def run_blocks(i, n):
        for j in range(n):
            in_copy(i + j).wait()
        for j in range(n):
            start_in(i + ahead + j)
        res = ffn(i, n)
        for j in range(n):
            wait_out(i + j - EXPERT_RING)
        for j in range(n):
            ybuf[slot_of(i + j)] = res[j]
        for j in range(n):
            out_copy(i + j).start()

    n_blk = last - first
    n_groups = n_blk // EXPERT_GROUP

    def group(p, carry):
        run_blocks(first + EXPERT_GROUP * p, EXPERT_GROUP)
        return carry

    lax.fori_loop(0, n_groups, group, 0)

    done = first + n_groups * EXPERT_GROUP
    size = EXPERT_GROUP // 2
    while size >= 1:
        @pl.when((n_blk & size) != 0)
        def _(done=done, size=size):
            run_blocks(done, size)
        done = done + (n_blk & size)
        size //= 2

    @pl.when(e == n_exp - 1)
    def _():
        for j in range(EXPERT_RING, 0, -1):
            wait_out(total - j)


def _experts(xs, bstart, w_gate, w_up, w_down, block_rows):
    n_slots, dw = xs.shape
    n_exp, d, f = w_gate.shape
    expert = lambda e, bs: (e, 0, 0)
    grid_spec = pltpu.PrefetchScalarGridSpec(
        num_scalar_prefetch=1,
        grid=(n_exp,),
        in_specs=[
            pl.BlockSpec(memory_space=pl.ANY),
            pl.BlockSpec((None, d, f), expert),
            pl.BlockSpec((None, d, f), expert),
            pl.BlockSpec((None, f, d), expert),
        ],
        out_specs=pl.BlockSpec(memory_space=pl.ANY),
        scratch_shapes=[
            pltpu.VMEM((EXPERT_RING, block_rows, dw), U32),
            pltpu.VMEM((EXPERT_RING, block_rows, dw), U32),
            pltpu.VMEM((d, 2 * f), BF16),
            pltpu.VMEM((f, d), BF16),
            pltpu.SemaphoreType.DMA((EXPERT_RING,)),
            pltpu.SemaphoreType.DMA((EXPERT_RING,)),
        ],
    )
    return pl.pallas_call(
        functools.partial(_expert_body, block_rows=block_rows),
        grid_spec=grid_spec,
        out_shape=jax.ShapeDtypeStruct((n_slots, dw), U32),
        compiler_params=_params("arbitrary"),
        name="expert_ffn",
    )(bstart, xs, w_gate, w_up, w_down)


SC_ROWS = 128


def _sc_workers():
    info = plsc.get_sparse_core_info()
    return info.num_cores, info.num_subcores


def _dispatch(h_rows, pos, n_slots):
    t_all, dw = h_rows.shape
    n_cores, n_sub = _sc_workers()
    per_worker = t_all // (n_cores * n_sub)
    assert per_worker * n_cores * n_sub == t_all and per_worker % SC_ROWS == 0
    mesh = plsc.VectorSubcoreMesh(core_axis_name="c", subcore_axis_name="s")

    @functools.partial(
        pl.kernel, mesh=mesh,
        out_type=jax.ShapeDtypeStruct((n_slots, dw), h_rows.dtype),
        scratch_types=[pltpu.VMEM((TOP_K, SC_ROWS), I32), pltpu.VMEM((SC_ROWS, dw), h_rows.dtype),
                       pltpu.SemaphoreType.DMA],
    )
    def body(h_hbm, pos_hbm, out_hbm, idx_v, rows_v, sem):
        base = (lax.axis_index("s") * n_cores + lax.axis_index("c")) * per_worker

        @pl.loop(0, per_worker // SC_ROWS)
        def _(j):
            t0 = pl.multiple_of(base + j * SC_ROWS, SC_ROWS)
            pltpu.sync_copy(pos_hbm.at[:, pl.ds(t0, SC_ROWS)], idx_v)
            pltpu.sync_copy(h_hbm.at[pl.ds(t0, SC_ROWS)], rows_v)
            copies = [pltpu.async_copy(rows_v, out_hbm.at[idx_v.at[k]], sem) for k in range(TOP_K)]
            for cp in copies:
                cp.wait()

    return body(h_rows, pos)


def _gather_back(y_rows, pos):
    top_k, t_all = pos.shape
    dw = y_rows.shape[1]
    n_cores, n_sub = _sc_workers()
    per_worker = t_all // (n_cores * n_sub)
    assert per_worker * n_cores * n_sub == t_all and per_worker % SC_ROWS == 0
    half = SC_ROWS // 2
    mesh = plsc.VectorSubcoreMesh(core_axis_name="c", subcore_axis_name="s")

    @functools.partial(
        pl.kernel, mesh=mesh,
        out_type=jax.ShapeDtypeStruct((top_k, t_all, dw), y_rows.dtype),
        scratch_types=[pltpu.VMEM((top_k, SC_ROWS), I32),
                       pltpu.VMEM((half, dw), y_rows.dtype), pltpu.VMEM((half, dw), y_rows.dtype),
                       pltpu.SemaphoreType.DMA, pltpu.SemaphoreType.DMA, pltpu.SemaphoreType.DMA],
    )
    def body(y_hbm, pos_hbm, out_hbm, idx_v, buf_a, buf_b, sem_g, sem_a, sem_b):
        base = (lax.axis_index("s") * n_cores + lax.axis_index("c")) * per_worker
        bufs, sems = (buf_a, buf_b), (sem_a, sem_b)

        @pl.loop(0, per_worker // SC_ROWS)
        def _(j):
            t0 = pl.multiple_of(base + j * SC_ROWS, SC_ROWS)
            pltpu.sync_copy(pos_hbm.at[:, pl.ds(t0, SC_ROWS)], idx_v)
            pending = [None, None]
            for step in range(2 * top_k):
                k, h = step // 2, step % 2
                slot = step % 2
                if pending[slot] is not None:
                    pending[slot].wait()
                pltpu.async_copy(y_hbm.at[idx_v.at[k, pl.ds(h * half, half)]], bufs[slot], sem_g).wait()
                pending[slot] = pltpu.async_copy(
                    bufs[slot], out_hbm.at[k, pl.ds(t0 + h * half, half)], sems[slot])
            for p in pending:
                p.wait()

    return body(y_rows, pos)


def _combine_body(yg_ref, w_ref, base_ref, mod_ref, fg_ref, o_ref):
    acc_lo = acc_hi = None
    for k in range(TOP_K):
        lo, hi = _unpack_rows(yg_ref[k])
        wk = w_ref[:, k:k + 1]
        acc_lo = lo * wk if acc_lo is None else acc_lo + lo * wk
        acc_hi = hi * wk if acc_hi is None else acc_hi + hi * wk
    acc = jnp.concatenate([acc_lo, acc_hi], axis=1)
    x2 = base_ref[...] + mod_ref[5:6, :] * acc
    ms = jnp.mean(x2 * x2, axis=-1, keepdims=True)
    o_ref[...] = (x2 * lax.rsqrt(ms + NORM_EPS)) * fg_ref[...]


def _combine(yg, wts_t, base, mod3, final_g):
    b, s, d = base.shape
    tm = min(256, s)
    n_t = s // tm
    return pl.pallas_call(
        _combine_body,
        grid=(b, n_t),
        in_specs=[pl.BlockSpec((TOP_K, tm, d // 2), lambda bi, i: (0, bi * n_t + i, 0)),
                  pl.BlockSpec((tm, TOP_K), lambda bi, i: (bi * n_t + i, 0)),
                  pl.BlockSpec((None, tm, d), lambda bi, i: (bi, i, 0)),
                  pl.BlockSpec((None, N_ADA, d), lambda bi, i: (bi, 0, 0)),
                  pl.BlockSpec((1, d), lambda bi, i: (0, 0))],
        out_specs=pl.BlockSpec((None, tm, d), lambda bi, i: (bi, i, 0)),
        out_shape=jax.ShapeDtypeStruct((b, s, d), F32),
        compiler_params=_params("parallel", "parallel"),
        name="combine_norm",
    )(yg, wts_t, base, mod3, final_g)


def _rope_tables(seq_len):
    rows = seq_len // GRID_W
    row = jnp.repeat(jnp.arange(rows), GRID_W).astype(F32)
    col = jnp.tile(jnp.arange(GRID_W), rows).astype(F32)
    n_freq = HEAD_DIM // 4
    inv = ROPE_THETA ** (-jnp.arange(n_freq, dtype=F32) / n_freq)
    ar, ac = row[:, None] * inv, col[:, None] * inv
    zeros = jnp.zeros_like(ar)
    reps = LANES // HEAD_DIM
    cos = jnp.tile(jnp.concatenate([jnp.cos(ar), jnp.cos(ar), jnp.cos(ac), jnp.cos(ac)], 1), (1, reps))
    sin_a = jnp.tile(jnp.concatenate([-jnp.sin(ar), zeros, -jnp.sin(ac), zeros], 1), (1, reps))
    sin_b = jnp.tile(jnp.concatenate([zeros, jnp.sin(ar), zeros, jnp.sin(ac)], 1), (1, reps))
    return cos, sin_a, sin_b


def kernel(x, c, ctx, c_ctx, w_ada, b_ada, norm1_g, w_in, attn_sink, w_pool, pool_scale,
           w_up_attn, w_up_pool, w_out, norm2_g, w_router, router_bias,
           w_exp_gate, w_exp_up, w_exp_down, w_sh_gate, w_sh_up, w_sh_down, final_g):
    b, s, d = x.shape
    assert w_ada.shape[0] == 1, "single-layer block"
    assert s % ATTN_BLOCK == 0 and s % GRID_W == 0 and d % LANES == 0
    pool_dim = w_up_pool.shape[1]
    n_exp = w_router.shape[-1]
    t_all = b * s

    pad_rows = -(-(b + 1) // SUBLANES) * SUBLANES
    c_rows = jnp.concatenate([c, c_ctx[None, :], jnp.zeros((pad_rows - b - 1, d), F32)], axis=0)
    mod3 = _ada(c_rows, w_ada[0], b_ada[0]).reshape(pad_rows, N_ADA, d)

    w_in_bf = w_in[0].astype(BF16)
    g1 = norm1_g[0].reshape(1, d)
    cos, sin_a, sin_b = _rope_tables(s)
    q, k, v, u, gates = _inproj(x, mod3, g1, w_in_bf, cos, sin_a, sin_b, pool_dim)
    kc, vc = _ctxkv(ctx, mod3[b], g1, w_in_bf[:, ATTN_DIM:ATTN_DIM + 2 * KV_DIM])
    attn = _attention(q, k, v, kc, vc, attn_sink[0])

    w_r_t = w_router[0].T
    w_r_hi = w_r_t.astype(BF16)
    w_r_lo = (w_r_t - w_r_hi.astype(F32)).astype(BF16)
    base, h2, eidx, wts, rank, cnt = _mixer(
        x, attn, u, gates, mod3, norm2_g[0].reshape(1, d), w_pool[0].astype(BF16),
        pool_scale[0].reshape(1, pool_dim), w_up_attn[0].astype(BF16), w_up_pool[0].astype(BF16),
        w_out[0].astype(BF16), w_r_hi, w_r_lo, router_bias[0].reshape(n_exp, 1),
        w_sh_gate[0].astype(BF16), w_sh_up[0].astype(BF16), w_sh_down[0].astype(BF16))

    br = EXPERT_BLOCK_ROWS
    n_blocks = (t_all * TOP_K + n_exp * (br - 1) + br - 1) // br
    n_slots = n_blocks * br
    pos, bstart = _slots(eidx, rank, cnt, br)
    xs = _dispatch(h2, pos, n_slots)
    y = _experts(xs, bstart[:n_exp + 1, 0], w_exp_gate[0], w_exp_up[0], w_exp_down[0], br)
    yg = _gather_back(y, pos)

    return _combine(yg, wts.T, base, mod3, final_g.reshape(1, d))
```

```python
import functools

import jax
import jax.numpy as jnp
from jax import lax
from jax.experimental import pallas as pl
from jax.experimental.pallas import tpu as pltpu
from jax.experimental.pallas import tpu_sc as plsc

F32 = jnp.float32
BF16 = jnp.bfloat16
I32 = jnp.int32
U32 = jnp.uint32

GRID_W = 64
HEAD_DIM = 64
N_Q_HEADS = 8
N_KV_HEADS = 2
Q_PER_KV = N_Q_HEADS // N_KV_HEADS
ATTN_DIM = N_Q_HEADS * HEAD_DIM
KV_DIM = N_KV_HEADS * HEAD_DIM
ATTN_BLOCK = 128
ATTN_SCALE = HEAD_DIM ** -0.5
ROPE_THETA = 10000.0
POOL_WINDOWS = (2, 4, 8, 16)
POOL_HALO = 8
N_EXPERT_GROUPS = 8
TOPK_GROUPS = 4
TOP_K = 8
ROUTED_SCALE = 2.5
N_ADA = 6
NORM_EPS = 1e-6
NEG_INF = -1e30
LANES = 128
SUBLANES = 8
ATTN_QBLOCKS = 4
TOKEN_TILE = 512
PROJ_TILE = 1024
EXPERT_BLOCK_ROWS = 256
EXPERT_GROUP = 4
EXPERT_RING = 16
VMEM_LIMIT = 56 * 1024 * 1024


def _sigmoid(x):
    return 1.0 / (1.0 + jnp.exp(-x))


def _silu(x):
    return x * _sigmoid(x)


def _nt_dot(a, b):
    return lax.dot_general(a, b, (((1,), (1,)), ((), ())), preferred_element_type=F32)


def _dot(a, b):
    return jnp.dot(a, b, preferred_element_type=F32)


def _pack_rows(x):
    n = x.shape[1] // 2
    bits = lax.bitcast_convert_type(x.astype(BF16).astype(F32), U32)
    return (bits[:, :n] >> 16) | (bits[:, n:] & jnp.uint32(0xFFFF0000))


def _unpack_rows(w):
    lo = lax.bitcast_convert_type(w << 16, F32)
    hi = lax.bitcast_convert_type(w & jnp.uint32(0xFFFF0000), F32)
    return lo, hi


def _params(*sem):
    return pltpu.CompilerParams(dimension_semantics=sem, vmem_limit_bytes=VMEM_LIMIT)


def _ada_body(c_ref, w_ref, b_ref, o_ref):
    s = _silu(c_ref[...])
    o_ref[...] = jnp.dot(s, w_ref[...], preferred_element_type=F32,
                         precision=lax.Precision.HIGHEST) + b_ref[...]


def _ada(c_rows, w_ada, b_ada):
    rows, d = c_rows.shape
    n = w_ada.shape[1]
    bn = d
    return pl.pallas_call(
        _ada_body,
        grid=(n // bn,),
        in_specs=[pl.BlockSpec((rows, d), lambda j: (0, 0)),
                  pl.BlockSpec((d, bn), lambda j: (0, j)),
                  pl.BlockSpec((1, bn), lambda j: (0, j))],
        out_specs=pl.BlockSpec((rows, bn), lambda j: (0, j)),
        out_shape=jax.ShapeDtypeStruct((rows, n), F32),
        compiler_params=_params("arbitrary"),
        name="ada_mod",
    )(c_rows, w_ada, b_ada.reshape(1, n))


def _norm_mod(x, g, shift, scale):
    ms = jnp.mean(x * x, axis=-1, keepdims=True)
    y = x * lax.rsqrt(ms + NORM_EPS)
    return (y * g) * (1.0 + scale) + shift


def _rope(t, cos, sin_a, sin_b):
    return (t * cos + pltpu.roll(t, LANES - HEAD_DIM // 4, 1) * sin_a
            + pltpu.roll(t, HEAD_DIM // 4, 1) * sin_b)


def _inproj_body(x_ref, mod_ref, g_ref, w_ref, cos_ref, sa_ref, sb_ref,
                 q_ref, k_ref, v_ref, u_ref, gate_ref):
    h = _norm_mod(x_ref[...], g_ref[...], mod_ref[0:1, :], mod_ref[1:2, :]).astype(BF16)
    cos, sa, sb = cos_ref[...], sa_ref[...], sb_ref[...]
    heads_per_chunk = LANES // HEAD_DIM
    wide = 2 * LANES
    for j in range(ATTN_DIM // wide):
        t2 = _dot(h, w_ref[:, j * wide:(j + 1) * wide])
        for c in range(wide // LANES):
            t = (_rope(t2[:, c * LANES:(c + 1) * LANES], cos, sa, sb) * ATTN_SCALE).astype(BF16)
            for i in range(heads_per_chunk):
                head = (j * (wide // LANES) + c) * heads_per_chunk + i
                q_ref[head] = t[:, i * HEAD_DIM:(i + 1) * HEAD_DIM]
    k_off = ATTN_DIM
    v_off = k_off + KV_DIM
    kv = _dot(h, w_ref[:, k_off:v_off + KV_DIM])
    t = _rope(kv[:, :KV_DIM], cos, sa, sb).astype(BF16)
    for i in range(N_KV_HEADS):
        k_ref[i] = t[:, i * HEAD_DIM:(i + 1) * HEAD_DIM]
    t = kv[:, KV_DIM:].astype(BF16)
    for i in range(N_KV_HEADS):
        v_ref[i] = t[:, i * HEAD_DIM:(i + 1) * HEAD_DIM]
    p_off = v_off + KV_DIM
    pool_dim = u_ref.shape[-1]
    u_ref[...] = _dot(h, w_ref[:, p_off:p_off + pool_dim])
    g_off = p_off + pool_dim
    gate_dim = gate_ref.shape[-1]
    chunk = 512
    for j in range(gate_dim // chunk):
        t = _dot(h, w_ref[:, g_off + j * chunk:g_off + (j + 1) * chunk])
        gate_ref[:, j * chunk:(j + 1) * chunk] = _sigmoid(t).astype(BF16)


def _inproj(x, mod3, g, w_in_bf, cos, sa, sb, pool_dim):
    b, s, d = x.shape
    tm = min(PROJ_TILE, s)
    in_dim = w_in_bf.shape[1]
    gate_dim = in_dim - ATTN_DIM - 2 * KV_DIM - pool_dim
    grid = (b, s // tm)
    return pl.pallas_call(
        _inproj_body,
        grid=grid,
        in_specs=[
            pl.BlockSpec((None, tm, d), lambda bi, i: (bi, i, 0)),
            pl.BlockSpec((None, N_ADA, d), lambda bi, i: (bi, 0, 0)),
            pl.BlockSpec((1, d), lambda bi, i: (0, 0)),
            pl.BlockSpec((d, in_dim), lambda bi, i: (0, 0)),
            pl.BlockSpec((tm, LANES), lambda bi, i: (i, 0)),
            pl.BlockSpec((tm, LANES), lambda bi, i: (i, 0)),
            pl.BlockSpec((tm, LANES), lambda bi, i: (i, 0)),
        ],
        out_specs=[
            pl.BlockSpec((None, N_Q_HEADS, tm, HEAD_DIM), lambda bi, i: (bi, 0, i, 0)),
            pl.BlockSpec((None, N_KV_HEADS, tm, HEAD_DIM), lambda bi, i: (bi, 0, i, 0)),
            pl.BlockSpec((None, N_KV_HEADS, tm, HEAD_DIM), lambda bi, i: (bi, 0, i, 0)),
            pl.BlockSpec((None, tm, pool_dim), lambda bi, i: (bi, i, 0)),
            pl.BlockSpec((None, tm, gate_dim), lambda bi, i: (bi, i, 0)),
        ],
        out_shape=[
            jax.ShapeDtypeStruct((b, N_Q_HEADS, s, HEAD_DIM), BF16),
            jax.ShapeDtypeStruct((b, N_KV_HEADS, s, HEAD_DIM), BF16),
            jax.ShapeDtypeStruct((b, N_KV_HEADS, s, HEAD_DIM), BF16),
            jax.ShapeDtypeStruct((b, s, pool_dim), F32),
            jax.ShapeDtypeStruct((b, s, gate_dim), BF16),
        ],
        compiler_params=_params("parallel", "parallel"),
        name="in_proj",
    )(x, mod3, g, w_in_bf, cos, sa, sb)


def _ctxkv_body(ctx_ref, mod_ref, g_ref, w_ref, kc_ref, vc_ref):
    h = _norm_mod(ctx_ref[...], g_ref[...], mod_ref[0:1, :], mod_ref[1:2, :]).astype(BF16)
    t = _dot(h, w_ref[...]).astype(BF16)
    for i in range(N_KV_HEADS):
        kc_ref[i] = t[:, i * HEAD_DIM:(i + 1) * HEAD_DIM]
        vc_ref[i] = t[:, KV_DIM + i * HEAD_DIM:KV_DIM + (i + 1) * HEAD_DIM]


def _ctxkv(ctx, mod_c, g, w_kv_bf):
    b, c, d = ctx.shape
    out = jax.ShapeDtypeStruct((b, N_KV_HEADS, c, HEAD_DIM), BF16)
    spec = pl.BlockSpec((None, N_KV_HEADS, c, HEAD_DIM), lambda bi: (bi, 0, 0, 0))
    return pl.pallas_call(
        _ctxkv_body,
        grid=(b,),
        in_specs=[pl.BlockSpec((None, c, d), lambda bi: (bi, 0, 0)),
                  pl.BlockSpec((N_ADA, d), lambda bi: (0, 0)),
                  pl.BlockSpec((1, d), lambda bi: (0, 0)),
                  pl.BlockSpec((d, 2 * KV_DIM), lambda bi: (0, 0))],
        out_specs=[spec, spec],
        out_shape=[out, out],
        compiler_params=_params("parallel"),
        name="ctx_kv",
    )(ctx, mod_c, g, w_kv_bf)


def _attn_body(sink_ref, q_ref, *refs):
    nkb = ATTN_QBLOCKS + 2
    k_refs, v_refs = refs[:nkb], refs[nkb:2 * nkb]
    kc_ref, vc_ref, lo_ref, hi_ref, o_ref = refs[2 * nkb:]
    n = pl.program_id(1)
    last = pl.num_programs(1) - 1
    rows = Q_PER_KV * ATTN_BLOCK
    row_head = lax.broadcasted_iota(I32, (rows, 1), 0) // ATTN_BLOCK
    for qb in range(ATTN_QBLOCKS):
        bias_lo, bias_hi = lo_ref[...], hi_ref[...]
        if qb == 0:
            bias_lo = bias_lo + jnp.where(n == 0, NEG_INF, 0.0)
        if qb == ATTN_QBLOCKS - 1:
            bias_hi = bias_hi + jnp.where(n == last, NEG_INF, 0.0)
        q_rows = slice(qb * ATTN_BLOCK, (qb + 1) * ATTN_BLOCK)
        outs = []
        for kk in range(N_KV_HEADS):
            qs = q_ref[kk * Q_PER_KV:(kk + 1) * Q_PER_KV, q_rows, :].reshape(rows, HEAD_DIM)
            k_loc = jnp.concatenate([k_refs[qb + j][kk] for j in range(3)], axis=0)
            v_loc = jnp.concatenate([v_refs[qb + j][kk] for j in range(3)], axis=0)
            s_loc = _nt_dot(qs, k_loc)
            s0 = s_loc[:, :ATTN_BLOCK] + bias_lo
            s1 = s_loc[:, ATTN_BLOCK:2 * ATTN_BLOCK]
            s2 = s_loc[:, 2 * ATTN_BLOCK:] + bias_hi
            sc = _nt_dot(qs, kc_ref[kk])
            sink = jnp.zeros((rows, 1), F32)
            for g in range(Q_PER_KV):
                sink = jnp.where(row_head == g, sink_ref[kk * Q_PER_KV + g], sink)
            n_ctx = sc.shape[1] // ATTN_BLOCK
            folded = jnp.maximum(jnp.maximum(s0, s1), s2)
            for j in range(n_ctx):
                folded = jnp.maximum(folded, sc[:, j * ATTN_BLOCK:(j + 1) * ATTN_BLOCK])
            m = jnp.maximum(jnp.max(folded, axis=1, keepdims=True), sink)
            p0, p1, p2, pc = jnp.exp(s0 - m), jnp.exp(s1 - m), jnp.exp(s2 - m), jnp.exp(sc - m)
            folded = p0 + p1 + p2
            for j in range(n_ctx):
                folded = folded + pc[:, j * ATTN_BLOCK:(j + 1) * ATTN_BLOCK]
            denom = jnp.sum(folded, axis=1, keepdims=True) + jnp.exp(sink - m)
            p_loc = jnp.concatenate([p0, p1, p2], axis=1).astype(BF16)
            o = _dot(p_loc, v_loc) + _dot(pc.astype(BF16), vc_ref[kk])
            o = o / denom
            outs += [o[g * ATTN_BLOCK:(g + 1) * ATTN_BLOCK] for g in range(Q_PER_KV)]
        o_ref[q_rows, :] = jnp.concatenate(outs, axis=1).astype(BF16)


def _attention(q, k, v, kc, vc, sink):
    b, _, s, _ = q.shape
    c = kc.shape[2]
    nb = s // ATTN_BLOCK
    assert nb % ATTN_QBLOCKS == 0 and c % ATTN_BLOCK == 0
    rows = Q_PER_KV * ATTN_BLOCK
    q_rows = ATTN_QBLOCKS * ATTN_BLOCK
    qi = jnp.tile(jnp.arange(ATTN_BLOCK), Q_PER_KV)[:, None]
    kj = jnp.arange(ATTN_BLOCK)[None, :]
    bias_lo = jnp.where(kj >= qi, 0.0, NEG_INF).astype(F32)
    bias_hi = jnp.where(kj <= qi, 0.0, NEG_INF).astype(F32)

    def kv_spec(i):
        return pl.BlockSpec(
            (None, N_KV_HEADS, ATTN_BLOCK, HEAD_DIM),
            lambda bi, n: (bi, 0, jnp.clip(ATTN_QBLOCKS * n - 1 + i, 0, nb - 1), 0))

    kv_specs = [kv_spec(i) for i in range(ATTN_QBLOCKS + 2)]
    ctx_spec = pl.BlockSpec((None, N_KV_HEADS, c, HEAD_DIM), lambda bi, n: (bi, 0, 0, 0))
    bias_spec = pl.BlockSpec((rows, ATTN_BLOCK), lambda bi, n: (0, 0))
    return pl.pallas_call(
        _attn_body,
        grid=(b, nb // ATTN_QBLOCKS),
        in_specs=[
            pl.BlockSpec(memory_space=pltpu.SMEM),
            pl.BlockSpec((None, N_Q_HEADS, q_rows, HEAD_DIM), lambda bi, n: (bi, 0, n, 0)),
            *kv_specs, *kv_specs, ctx_spec, ctx_spec, bias_spec, bias_spec,
        ],
        out_specs=pl.BlockSpec((None, q_rows, ATTN_DIM), lambda bi, n: (bi, n, 0)),
        out_shape=jax.ShapeDtypeStruct((b, s, ATTN_DIM), BF16),
        compiler_params=_params("parallel", "parallel"),
        name="window_attn",
    )(sink, q, *([k] * (ATTN_QBLOCKS + 2)), *([v] * (ATTN_QBLOCKS + 2)), kc, vc, bias_lo, bias_hi)


def _pool_delta(ue, inv_cnt):
    n_ext = ue.shape[0]
    tm = n_ext - 2 * POOL_HALO
    group_dim = ue.shape[1] // len(POOL_WINDOWS)
    outs = []
    for g, w in enumerate(POOL_WINDOWS):
        xs = ue[:, g * group_dim:(g + 1) * group_dim]
        acc = xs + pltpu.roll(xs, 1, 0)
        step = 1
        while 2 * step < w:
            acc = pltpu.roll(acc, step, 0) + pltpu.roll(acc, n_ext - step, 0)
            step *= 2
        core = slice(POOL_HALO, POOL_HALO + tm)
        outs.append(acc[core] * inv_cnt[:, g:g + 1] - xs[core])
    return outs


def _route_steps(s, sel, iota_ref, out):
    n_exp = s.shape[0]
    per_group = n_exp // N_EXPERT_GROUPS
    neg = float("-inf")
    iota_g = iota_ref[0:per_group, :]
    scores = []
    for g in range(N_EXPERT_GROUPS):
        blk = sel[g * per_group:(g + 1) * per_group]
        m1 = jnp.max(blk, axis=0, keepdims=True)
        first = jnp.min(jnp.where(blk == m1, iota_g, float(per_group)), axis=0, keepdims=True)
        m2 = jnp.max(jnp.where(iota_g == first, neg, blk), axis=0, keepdims=True)
        scores.append(m1 + m2)
        if g % 2 == 1:
            yield
    gs = jnp.concatenate(scores, axis=0)
    iota_ng = iota_ref[0:N_EXPERT_GROUPS, :]
    gsel = jnp.zeros(iota_ng.shape, F32)
    for _ in range(TOPK_GROUPS):
        m = jnp.max(gs, axis=0, keepdims=True)
        first = jnp.min(jnp.where(gs == m, iota_ng, float(N_EXPERT_GROUPS)), axis=0, keepdims=True)
        hit = iota_ng == first
        gsel = jnp.where(hit, 1.0, gsel)
        gs = jnp.where(hit, neg, gs)
    cur = jnp.concatenate(
        [jnp.where(gsel[g:g + 1] > 0.0, sel[g * per_group:(g + 1) * per_group], NEG_INF)
         for g in range(N_EXPERT_GROUPS)], axis=0)
    yield
    chosen = jnp.zeros(s.shape, F32)
    ids, aff = [], []
    for _ in range(TOP_K):
        m = jnp.max(cur, axis=0, keepdims=True)
        first = jnp.min(jnp.where(cur == m, iota_ref[...], float(n_exp)), axis=0, keepdims=True)
        hit = iota_ref[...] == first
        ids.append(first)
        aff.append(jnp.sum(jnp.where(hit, s, 0.0), axis=0, keepdims=True))
        chosen = jnp.where(hit, 1.0, chosen)
        cur = jnp.where(hit, neg, cur)
        yield
    out.update(ids=ids, aff=aff, chosen=chosen)


def _mixer_body(x_ref, attn_ref, up_ref, u_ref, un_ref, gate_ref, mod_ref, icnt_ref, iota_ref, n2g_ref,
                wpool_ref, pscale_ref, wua_ref, wup_ref, wout_ref, wrh_ref, wrl_ref, rbias_ref,
                wsg_ref, wsu_ref, wsd_ref,
                base_ref, h2_ref, eidx_ref, wts_ref, rank_ref, cnt_ref, score_buf, *, n_t):
    g = pl.program_id(0)
    n_tiles = pl.num_programs(0) - 1
    ti = lax.rem(jnp.minimum(g, n_tiles - 1), n_t)
    slot = g & 1
    tm, d = x_ref.shape
    pool_dim = u_ref.shape[-1]
    n_exp = score_buf.shape[1]

    @pl.when(g == 0)
    def _():
        cnt_ref[...] = jnp.zeros_like(cnt_ref)
        score_buf[1] = jnp.zeros((n_exp, tm), F32)

    def routing():
        s = score_buf[1 - slot]
        sel = s + rbias_ref[...]
        res = {}
        yield from _route_steps(s, sel, iota_ref, res)
        ids, aff, chosen = res["ids"], res["aff"], res["chosen"]
        total = aff[0]
        for a in aff[1:]:
            total = total + a
        wts_ref[...] = jnp.concatenate([a / total * ROUTED_SCALE for a in aff], axis=0)
        eidx_ref[...] = jnp.concatenate(ids, axis=0).astype(I32)
        yield
        before = (lax.broadcasted_iota(I32, (tm, tm), 0) < lax.broadcasted_iota(I32, (tm, tm), 1))
        prefix = _dot(chosen.astype(BF16), jnp.where(before, 1.0, 0.0).astype(BF16))
        rank_dense = prefix + cnt_ref[:, 0:1]
        rank_ref[...] = jnp.concatenate(
            [jnp.sum(jnp.where(iota_ref[...] == i, rank_dense, 0.0), axis=0, keepdims=True) for i in ids],
            axis=0).astype(I32)
        real = jnp.where(g > 0, 1.0, 0.0)
        cnt_ref[...] = cnt_ref[...] + real * jnp.sum(chosen, axis=1, keepdims=True)

    def chain():
        zero_halo = jnp.zeros((POOL_HALO, pool_dim), F32)
        u_prev = jnp.where(ti == 0, zero_halo, up_ref[...])
        u_next = jnp.where(ti == n_t - 1, zero_halo, un_ref[...])
        ue = jnp.concatenate([u_prev, u_ref[...], u_next], axis=0)
        deltas = _pool_delta(ue, icnt_ref[...])
        attn_up = _dot(attn_ref[...], wua_ref[...])
        yield
        pool = jnp.concatenate(
            [_dot(dl.astype(BF16), wpool_ref[i]) for i, dl in enumerate(deltas)], axis=1)
        pool = (pool * pscale_ref[...]).astype(BF16)
        yield
        pool_up = _dot(pool, wup_ref[...])
        yield
        y = gate_ref[:, :d] * attn_up.astype(BF16) + gate_ref[:, d:] * pool_up.astype(BF16)
        mix = _dot(y, wout_ref[...])
        yield
        x1 = x_ref[...] + mod_ref[2:3, :] * mix
        h2 = _norm_mod(x1, n2g_ref[...], mod_ref[3:4, :], mod_ref[4:5, :])
        h2_hi = h2.astype(BF16)
        h2_ref[...] = _pack_rows(h2)
        gate = _dot(h2_hi, wsg_ref[...])
        yield
        up = _dot(h2_hi, wsu_ref[...])
        yield
        shared = _dot((_silu(gate) * up).astype(BF16), wsd_ref[...])
        yield
        base_ref[...] = x1 + mod_ref[5:6, :] * shared
        h2_lo = (h2 - h2_hi.astype(F32)).astype(BF16)
        logits = _nt_dot(wrh_ref[...], h2_hi)
        yield
        logits = logits + _nt_dot(wrh_ref[...], h2_lo)
        yield
        logits = logits + _nt_dot(wrl_ref[...], h2_hi)
        score_buf[slot] = _sigmoid(logits)

    stages = [chain(), routing()]
    while stages:
        for gen in list(stages):
            try:
                next(gen)
            except StopIteration:
                stages.remove(gen)


def _mixer(x, attn, u, gates, mod3, n2g, wpool_bf, pscale, wua_bf, wup_bf, wout_bf,
           wr_hi, wr_lo, rbias, wsg_bf, wsu_bf, wsd_bf):
    b, s, d = x.shape
    tm = min(TOKEN_TILE, s)
    n_t = s // tm
    n_tiles = b * n_t
    t_all = b * s
    pool_dim = u.shape[-1]
    n_exp = wr_hi.shape[0]
    sh = wsg_bf.shape[1]
    halo_blocks = tm // POOL_HALO
    n_halo = s // POOL_HALO
    full = lambda shape: pl.BlockSpec(shape, lambda g: (0,) * len(shape))

    def tile(g):
        tg = jnp.minimum(g, n_tiles - 1)
        return tg // n_t, lax.rem(tg, n_t)

    def seq_block(g):
        bi, i = tile(g)
        return bi, i, 0

    def halo_prev(g):
        bi, i = tile(g)
        return bi, jnp.maximum(i * halo_blocks - 1, 0), 0

    def halo_next(g):
        bi, i = tile(g)
        return bi, jnp.minimum((i + 1) * halo_blocks, n_halo - 1), 0

    routed = lambda g: (0, jnp.maximum(g - 1, 0))
    t = jnp.arange(s)
    inv_cnt = jnp.stack(
        [1.0 / (jnp.minimum(t + w // 2, s) - jnp.maximum(t - w // 2, 0)).astype(F32) for w in POOL_WINDOWS], axis=1)
    expert_iota = jnp.broadcast_to(jnp.arange(n_exp, dtype=F32)[:, None], (n_exp, tm))
    return pl.pallas_call(
        functools.partial(_mixer_body, n_t=n_t),
        grid=(n_tiles + 1,),
        in_specs=[
            pl.BlockSpec((None, tm, d), seq_block),
            pl.BlockSpec((None, tm, ATTN_DIM), seq_block),
            pl.BlockSpec((None, POOL_HALO, pool_dim), halo_prev),
            pl.BlockSpec((None, tm, pool_dim), seq_block),
            pl.BlockSpec((None, POOL_HALO, pool_dim), halo_next),
            pl.BlockSpec((None, tm, 2 * d), seq_block),
            pl.BlockSpec((None, N_ADA, d), lambda g: (tile(g)[0], 0, 0)),
            pl.BlockSpec((tm, len(POOL_WINDOWS)), lambda g: (tile(g)[1], 0)),
            full((n_exp, tm)),
            full((1, d)),
            full((len(POOL_WINDOWS), pool_dim // len(POOL_WINDOWS), pool_dim // len(POOL_WINDOWS))),
            full((1, pool_dim)),
            full((ATTN_DIM, d)), full((pool_dim, d)), full((d, d)),
            full((n_exp, d)), full((n_exp, d)), full((n_exp, 1)),
            full((d, sh)), full((d, sh)), full((sh, d)),
        ],
        out_specs=[
            pl.BlockSpec((None, tm, d), seq_block),
            pl.BlockSpec((tm, d // 2), lambda g: (jnp.minimum(g, n_tiles - 1), 0)),
            pl.BlockSpec((TOP_K, tm), routed),
            pl.BlockSpec((TOP_K, tm), routed),
            pl.BlockSpec((TOP_K, tm), routed),
            pl.BlockSpec((n_exp, LANES), lambda g: (0, 0)),
        ],
        out_shape=[
            jax.ShapeDtypeStruct((b, s, d), F32),
            jax.ShapeDtypeStruct((t_all, d // 2), U32),
            jax.ShapeDtypeStruct((TOP_K, t_all), I32),
            jax.ShapeDtypeStruct((TOP_K, t_all), F32),
            jax.ShapeDtypeStruct((TOP_K, t_all), I32),
            jax.ShapeDtypeStruct((n_exp, LANES), F32),
        ],
        scratch_shapes=[pltpu.VMEM((2, n_exp, tm), F32)],
        compiler_params=_params("arbitrary"),
        name="mixer_router",
    )(x, attn, u, u, u, gates, mod3, inv_cnt, expert_iota, n2g, wpool_bf, pscale, wua_bf, wup_bf, wout_bf,
      wr_hi, wr_lo, rbias, wsg_bf, wsu_bf, wsd_bf)


def _slots_body(eidx_ref, rank_ref, cnt_ref, pos_ref, bstart_ref, *, block_rows):
    n_exp = cnt_ref.shape[0]
    tm = eidx_ref.shape[1]
    cnt = cnt_ref[...]
    padded = jnp.floor((cnt + (block_rows - 1)) / block_rows) * block_rows
    hi = jnp.floor(padded / 256.0)
    lo = padded - hi * 256.0
    below = (lax.broadcasted_iota(I32, (n_exp, n_exp), 1) < lax.broadcasted_iota(I32, (n_exp, n_exp), 0))
    tri = jnp.where(below, 1.0, 0.0).astype(BF16)
    start = 256.0 * _dot(tri, hi.astype(BF16)) + _dot(tri, lo.astype(BF16))
    end = start + padded
    iota_e = lax.broadcasted_iota(I32, (n_exp, tm), 0)
    start_col = start[:, 0:1]
    rows = []
    for k in range(TOP_K):
        hit = iota_e == eidx_ref[k:k + 1, :]
        rows.append(jnp.sum(jnp.where(hit, start_col, 0.0), axis=0, keepdims=True))
    pos_ref[...] = jnp.concatenate(rows, axis=0).astype(I32) + rank_ref[...]

    @pl.when(pl.program_id(0) == 0)
    def _():
        bstart_ref[0:n_exp, :] = (start / block_rows).astype(I32)
        bstart_ref[n_exp:, :] = jnp.broadcast_to(
            (end[n_exp - 1:n_exp, :] / block_rows).astype(I32), (SUBLANES, LANES))


def _slots(eidx, rank, cnt, block_rows):
    t_all = eidx.shape[1]
    n_exp = cnt.shape[0]
    tm = min(TOKEN_TILE, t_all)
    return pl.pallas_call(
        functools.partial(_slots_body, block_rows=block_rows),
        grid=(t_all // tm,),
        in_specs=[pl.BlockSpec((TOP_K, tm), lambda i: (0, i)),
                  pl.BlockSpec((TOP_K, tm), lambda i: (0, i)),
                  pl.BlockSpec((n_exp, LANES), lambda i: (0, 0))],
        out_specs=[pl.BlockSpec((TOP_K, tm), lambda i: (0, i)),
                   pl.BlockSpec((n_exp + SUBLANES, LANES), lambda i: (0, 0))],
        out_shape=[jax.ShapeDtypeStruct((TOP_K, t_all), I32),
                   jax.ShapeDtypeStruct((n_exp + SUBLANES, LANES), I32)],
        compiler_params=_params("arbitrary"),
        name="slot_positions",
    )(eidx, rank, cnt)


def _expert_body(bstart_ref, xs_hbm, wg_ref, wu_ref, wd_ref, y_hbm,
                 xbuf, ybuf, wgu, wdn, sem_in, sem_out, *, block_rows):
    e = pl.program_id(0)
    n_exp = pl.num_programs(0)
    first, last, total = bstart_ref[e], bstart_ref[e + 1], bstart_ref[n_exp]
    f = wg_ref.shape[1]
    ahead = EXPERT_RING - EXPERT_GROUP

    def rows(i):
        return pl.ds(pl.multiple_of(i * block_rows, block_rows), block_rows)

    def slot_of(i):
        return i & (EXPERT_RING - 1)

    def in_copy(i):
        return pltpu.make_async_copy(xs_hbm.at[rows(i)], xbuf.at[slot_of(i)], sem_in.at[slot_of(i)])

    def out_copy(i):
        return pltpu.make_async_copy(ybuf.at[slot_of(i)], y_hbm.at[rows(i)], sem_out.at[slot_of(i)])

    def start_in(i):
        @pl.when(i < total)
        def _():
            in_copy(i).start()

    def wait_out(i):
        @pl.when(i >= 0)
        def _():
            out_copy(i).wait()

    @pl.when(e == 0)
    def _():
        for j in range(ahead):
            start_in(jnp.int32(j))

    wgu[:, :f] = wg_ref[...].astype(BF16)
    wgu[:, f:] = wu_ref[...].astype(BF16)
    wdn[...] = wd_ref[...].astype(BF16)

    def ffn(i, n):
        parts = []
        for j in range(n):
            lo, hi = _unpack_rows(xbuf[slot_of(i + j)])
            parts.append(jnp.concatenate([lo, hi], axis=1).astype(BF16))
        x = jnp.concatenate(parts, axis=0)
        gu = _dot(x, wgu[...])
        y = _pack_rows(_dot((_silu(gu[:, :f]) * gu[:, f:]).astype(BF16), wdn[...]))
        return [y[j * block_rows:(j + 1) * block_rows] for j in range(n)]

    def run_blocks(i, n):
        for j in range(n):
            in_copy(i + j).wait()
        for j in range(n):
            start_in(i + ahead + j)
        res = ffn(i, n)
        for j in range(n):
            wait_out(i + j - EXPERT_RING)
        for j in range(n):
            ybuf[slot_of(i + j)] = res[j]
        for j in range(n):
            out_copy(i + j).start()

    n_blk = last - first
    n_groups = n_blk // EXPERT_GROUP

    def group(p, carry):
        run_blocks(first + EXPERT_GROUP * p, EXPERT_GROUP)
        return carry

    lax.fori_loop(0, n_groups, group, 0)

    done = first + n_groups * EXPERT_GROUP
    size = EXPERT_GROUP // 2
    while size >= 1:
        @pl.when((n_blk & size) != 0)
        def _(done=done, size=size):
            run_blocks(done, size)
        done = done + (n_blk & size)
        size //= 2

    @pl.when(e == n_exp - 1)
    def _():
        for j in range(EXPERT_RING, 0, -1):
            wait_out(total - j)


def _experts(xs, bstart, w_gate, w_up, w_down, block_rows):
    n_slots, dw = xs.shape
    n_exp, d, f = w_gate.shape
    expert = lambda e, bs: (e, 0, 0)
    grid_spec = pltpu.PrefetchScalarGridSpec(
        num_scalar_prefetch=1,
        grid=(n_exp,),
        in_specs=[
            pl.BlockSpec(memory_space=pl.ANY),
            pl.BlockSpec((None, d, f), expert),
            pl.BlockSpec((None, d, f), expert),
            pl.BlockSpec((None, f, d), expert),
        ],
        out_specs=pl.BlockSpec(memory_space=pl.ANY),
        scratch_shapes=[
            pltpu.VMEM((EXPERT_RING, block_rows, dw), U32),
            pltpu.VMEM((EXPERT_RING, block_rows, dw), U32),
            pltpu.VMEM((d, 2 * f), BF16),
            pltpu.VMEM((f, d), BF16),
            pltpu.SemaphoreType.DMA((EXPERT_RING,)),
            pltpu.SemaphoreType.DMA((EXPERT_RING,)),
        ],
    )
    return pl.pallas_call(
        functools.partial(_expert_body, block_rows=block_rows),
        grid_spec=grid_spec,
        out_shape=jax.ShapeDtypeStruct((n_slots, dw), U32),
        compiler_params=_params("arbitrary"),
        name="expert_ffn",
    )(bstart, xs, w_gate, w_up, w_down)


SC_ROWS = 128


def _sc_workers():
    info = plsc.get_sparse_core_info()
    return info.num_cores, info.num_subcores


def _dispatch(h_rows, pos, n_slots):
    t_all, dw = h_rows.shape
    n_cores, n_sub = _sc_workers()
    per_worker = t_all // (n_cores * n_sub)
    assert per_worker * n_cores * n_sub == t_all and per_worker % SC_ROWS == 0
    mesh = plsc.VectorSubcoreMesh(core_axis_name="c", subcore_axis_name="s")

    @functools.partial(
        pl.kernel, mesh=mesh,
        out_type=jax.ShapeDtypeStruct((n_slots, dw), h_rows.dtype),
        scratch_types=[pltpu.VMEM((TOP_K, SC_ROWS), I32), pltpu.VMEM((SC_ROWS, dw), h_rows.dtype),
                       pltpu.SemaphoreType.DMA],
    )
    def body(h_hbm, pos_hbm, out_hbm, idx_v, rows_v, sem):
        base = (lax.axis_index("s") * n_cores + lax.axis_index("c")) * per_worker

        @pl.loop(0, per_worker // SC_ROWS)
        def _(j):
            t0 = pl.multiple_of(base + j * SC_ROWS, SC_ROWS)
            pltpu.sync_copy(pos_hbm.at[:, pl.ds(t0, SC_ROWS)], idx_v)
            pltpu.sync_copy(h_hbm.at[pl.ds(t0, SC_ROWS)], rows_v)
            copies = [pltpu.async_copy(rows_v, out_hbm.at[idx_v.at[k]], sem) for k in range(TOP_K)]
            for cp in copies:
                cp.wait()

    return body(h_rows, pos)


def _gather_back(y_rows, pos):
    top_k, t_all = pos.shape
    dw = y_rows.shape[1]
    n_cores, n_sub = _sc_workers()
    per_worker = t_all // (n_cores * n_sub)
    assert per_worker * n_cores * n_sub == t_all and per_worker % SC_ROWS == 0
    half = SC_ROWS // 2
    mesh = plsc.VectorSubcoreMesh(core_axis_name="c", subcore_axis_name="s")

    @functools.partial(
        pl.kernel, mesh=mesh,
        out_type=jax.ShapeDtypeStruct((top_k, t_all, dw), y_rows.dtype),
        scratch_types=[pltpu.VMEM((top_k, SC_ROWS), I32),
                       pltpu.VMEM((half, dw), y_rows.dtype), pltpu.VMEM((half, dw), y_rows.dtype),
                       pltpu.SemaphoreType.DMA, pltpu.SemaphoreType.DMA, pltpu.SemaphoreType.DMA],
    )
    def body(y_hbm, pos_hbm, out_hbm, idx_v, buf_a, buf_b, sem_g, sem_a, sem_b):
        base = (lax.axis_index("s") * n_cores + lax.axis_index("c")) * per_worker
        bufs, sems = (buf_a, buf_b), (sem_a, sem_b)

        @pl.loop(0, per_worker // SC_ROWS)
        def _(j):
            t0 = pl.multiple_of(base + j * SC_ROWS, SC_ROWS)
            pltpu.sync_copy(pos_hbm.at[:, pl.ds(t0, SC_ROWS)], idx_v)
            pending = [None, None]
            for step in range(2 * top_k):
                k, h = step // 2, step % 2
                slot = step % 2
                if pending[slot] is not None:
                    pending[slot].wait()
                pltpu.async_copy(y_hbm.at[idx_v.at[k, pl.ds(h * half, half)]], bufs[slot], sem_g).wait()
                pending[slot] = pltpu.async_copy(
                    bufs[slot], out_hbm.at[k, pl.ds(t0 + h * half, half)], sems[slot])
            for p in pending:
                p.wait()

    return body(y_rows, pos)


def _combine_body(yg_ref, w_ref, base_ref, mod_ref, fg_ref, o_ref):
    acc_lo = acc_hi = None
    for k in range(TOP_K):
        lo, hi = _unpack_rows(yg_ref[k])
        wk = w_ref[:, k:k + 1]
        acc_lo = lo * wk if acc_lo is None else acc_lo + lo * wk
        acc_hi = hi * wk if acc_hi is None else acc_hi + hi * wk
    acc = jnp.concatenate([acc_lo, acc_hi], axis=1)
    x2 = base_ref[...] + mod_ref[5:6, :] * acc
    ms = jnp.mean(x2 * x2, axis=-1, keepdims=True)
    o_ref[...] = (x2 * lax.rsqrt(ms + NORM_EPS)) * fg_ref[...]


def _combine(yg, wts_t, base, mod3, final_g):
    b, s, d = base.shape
    tm = min(256, s)
    n_t = s // tm
    return pl.pallas_call(
        _combine_body,
        grid=(b, n_t),
        in_specs=[pl.BlockSpec((TOP_K, tm, d // 2), lambda bi, i: (0, bi * n_t + i, 0)),
                  pl.BlockSpec((tm, TOP_K), lambda bi, i: (bi * n_t + i, 0)),
                  pl.BlockSpec((None, tm, d), lambda bi, i: (bi, i, 0)),
                  pl.BlockSpec((None, N_ADA, d), lambda bi, i: (bi, 0, 0)),
                  pl.BlockSpec((1, d), lambda bi, i: (0, 0))],
        out_specs=pl.BlockSpec((None, tm, d), lambda bi, i: (bi, i, 0)),
        out_shape=jax.ShapeDtypeStruct((b, s, d), F32),
        compiler_params=_params("parallel", "parallel"),
        name="combine_norm",
    )(yg, wts_t, base, mod3, final_g)


def _rope_tables(seq_len):
    rows = seq_len // GRID_W
    row = jnp.repeat(jnp.arange(rows), GRID_W).astype(F32)
    col = jnp.tile(jnp.arange(GRID_W), rows).astype(F32)
    n_freq = HEAD_DIM // 4
    inv = ROPE_THETA ** (-jnp.arange(n_freq, dtype=F32) / n_freq)
    ar, ac = row[:, None] * inv, col[:, None] * inv
    zeros = jnp.zeros_like(ar)
    reps = LANES // HEAD_DIM
    cos = jnp.tile(jnp.concatenate([jnp.cos(ar), jnp.cos(ar), jnp.cos(ac), jnp.cos(ac)], 1), (1, reps))
    sin_a = jnp.tile(jnp.concatenate([-jnp.sin(ar), zeros, -jnp.sin(ac), zeros], 1), (1, reps))
    sin_b = jnp.tile(jnp.concatenate([zeros, jnp.sin(ar), zeros, jnp.sin(ac)], 1), (1, reps))
    return cos, sin_a, sin_b


def kernel(x, c, ctx, c_ctx, w_ada, b_ada, norm1_g, w_in, attn_sink, w_pool, pool_scale,
           w_up_attn, w_up_pool, w_out, norm2_g, w_router, router_bias,
           w_exp_gate, w_exp_up, w_exp_down, w_sh_gate, w_sh_up, w_sh_down, final_g):
    b, s, d = x.shape
    assert w_ada.shape[0] == 1, "single-layer block"
    assert s % ATTN_BLOCK == 0 and s % GRID_W == 0 and d % LANES == 0
    pool_dim = w_up_pool.shape[1]
    n_exp = w_router.shape[-1]
    t_all = b * s

    pad_rows = -(-(b + 1) // SUBLANES) * SUBLANES
    c_rows = jnp.concatenate([c, c_ctx[None, :], jnp.zeros((pad_rows - b - 1, d), F32)], axis=0)
    mod3 = _ada(c_rows, w_ada[0], b_ada[0]).reshape(pad_rows, N_ADA, d)

    w_in_bf = w_in[0].astype(BF16)
    g1 = norm1_g[0].reshape(1, d)
    cos, sin_a, sin_b = _rope_tables(s)
    q, k, v, u, gates = _inproj(x, mod3, g1, w_in_bf, cos, sin_a, sin_b, pool_dim)
    kc, vc = _ctxkv(ctx, mod3[b], g1, w_in_bf[:, ATTN_DIM:ATTN_DIM + 2 * KV_DIM])
    attn = _attention(q, k, v, kc, vc, attn_sink[0])

    w_r_t = w_router[0].T
    w_r_hi = w_r_t.astype(BF16)
    w_r_lo = (w_r_t - w_r_hi.astype(F32)).astype(BF16)
    base, h2, eidx, wts, rank, cnt = _mixer(
        x, attn, u, gates, mod3, norm2_g[0].reshape(1, d), w_pool[0].astype(BF16),
        pool_scale[0].reshape(1, pool_dim), w_up_attn[0].astype(BF16), w_up_pool[0].astype(BF16),
        w_out[0].astype(BF16), w_r_hi, w_r_lo, router_bias[0].reshape(n_exp, 1),
        w_sh_gate[0].astype(BF16), w_sh_up[0].astype(BF16), w_sh_down[0].astype(BF16))

    br = EXPERT_BLOCK_ROWS
    n_blocks = (t_all * TOP_K + n_exp * (br - 1) + br - 1) // br
    n_slots = n_blocks * br
    pos, bstart = _slots(eidx, rank, cnt, br)
    xs = _dispatch(h2, pos, n_slots)
    y = _experts(xs, bstart[:n_exp + 1, 0], w_exp_gate[0], w_exp_up[0], w_exp_down[0], br)
    yg = _gather_back(y, pos)

    return _combine(yg, wts.T, base, mod3, final_g.reshape(1, d))
```

```python
import functools

import jax
import jax.numpy as jnp
from jax import lax
from jax.experimental import pallas as pl
from jax.experimental.pallas import tpu as pltpu
from jax.experimental.pallas import tpu_sc as plsc

F32 = jnp.float32
BF16 = jnp.bfloat16
I32 = jnp.int32
U32 = jnp.uint32

GRID_W = 64
HEAD_DIM = 64
N_Q_HEADS = 8
N_KV_HEADS = 2
Q_PER_KV = N_Q_HEADS // N_KV_HEADS
ATTN_DIM = N_Q_HEADS * HEAD_DIM
KV_DIM = N_KV_HEADS * HEAD_DIM
ATTN_BLOCK = 128
ATTN_SCALE = HEAD_DIM ** -0.5
ROPE_THETA = 10000.0
POOL_WINDOWS = (2, 4, 8, 16)
POOL_HALO = 8
N_EXPERT_GROUPS = 8
TOPK_GROUPS = 4
TOP_K = 8
ROUTED_SCALE = 2.5
N_ADA = 6
NORM_EPS = 1e-6
NEG_INF = -1e30
LANES = 128
SUBLANES = 8
ATTN_QBLOCKS = 4
TOKEN_TILE = 512
PROJ_TILE = 1024
EXPERT_BLOCK_ROWS = 256
EXPERT_GROUP = 4
EXPERT_RING = 16
VMEM_LIMIT = 56 * 1024 * 1024


def _sigmoid(x):
    return 1.0 / (1.0 + jnp.exp(-x))


def _silu(x):
    return x * _sigmoid(x)


def _nt_dot(a, b):
    return lax.dot_general(a, b, (((1,), (1,)), ((), ())), preferred_element_type=F32)


def _dot(a, b):
    return jnp.dot(a, b, preferred_element_type=F32)


def _pack_rows(x):
    n = x.shape[1] // 2
    bits = lax.bitcast_convert_type(x.astype(BF16).astype(F32), U32)
    return (bits[:, :n] >> 16) | (bits[:, n:] & jnp.uint32(0xFFFF0000))


def _unpack_rows(w):
    lo = lax.bitcast_convert_type(w << 16, F32)
    hi = lax.bitcast_convert_type(w & jnp.uint32(0xFFFF0000), F32)
    return lo, hi


def _params(*sem):
    return pltpu.CompilerParams(dimension_semantics=sem, vmem_limit_bytes=VMEM_LIMIT)


def _ada_body(c_ref, w_ref, b_ref, o_ref):
    s = _silu(c_ref[...])
    o_ref[...] = jnp.dot(s, w_ref[...], preferred_element_type=F32,
                         precision=lax.Precision.HIGHEST) + b_ref[...]


def _ada(c_rows, w_ada, b_ada):
    rows, d = c_rows.shape
    n = w_ada.shape[1]
    bn = d
    return pl.pallas_call(
        _ada_body,
        grid=(n // bn,),
        in_specs=[pl.BlockSpec((rows, d), lambda j: (0, 0)),
                  pl.BlockSpec((d, bn), lambda j: (0, j)),
                  pl.BlockSpec((1, bn), lambda j: (0, j))],
        out_specs=pl.BlockSpec((rows, bn), lambda j: (0, j)),
        out_shape=jax.ShapeDtypeStruct((rows, n), F32),
        compiler_params=_params("arbitrary"),
        name="ada_mod",
    )(c_rows, w_ada, b_ada.reshape(1, n))


def _norm_mod(x, g, shift, scale):
    ms = jnp.mean(x * x, axis=-1, keepdims=True)
    y = x * lax.rsqrt(ms + NORM_EPS)
    return y * (g * (1.0 + scale)) + shift


def _rope(t, cos, sin_a, sin_b):
    return (t * cos + pltpu.roll(t, LANES - HEAD_DIM // 4, 1) * sin_a
            + pltpu.roll(t, HEAD_DIM // 4, 1) * sin_b)


def _inproj_body(x_ref, mod_ref, g_ref, w_ref, cos_ref, sa_ref, sb_ref,
                 q_ref, k_ref, v_ref, u_ref, gate_ref):
    h = _norm_mod(x_ref[...], g_ref[...], mod_ref[0:1, :], mod_ref[1:2, :]).astype(BF16)
    cos, sa, sb = cos_ref[...], sa_ref[...], sb_ref[...]
    heads_per_chunk = LANES // HEAD_DIM
    wide = 2 * LANES
    for j in range(ATTN_DIM // wide):
        t2 = _dot(h, w_ref[:, j * wide:(j + 1) * wide])
        for c in range(wide // LANES):
            t = (_rope(t2[:, c * LANES:(c + 1) * LANES], cos, sa, sb) * ATTN_SCALE).astype(BF16)
            for i in range(heads_per_chunk):
                head = (j * (wide // LANES) + c) * heads_per_chunk + i
                q_ref[head] = t[:, i * HEAD_DIM:(i + 1) * HEAD_DIM]
    k_off = ATTN_DIM
    v_off = k_off + KV_DIM
    kv = _dot(h, w_ref[:, k_off:v_off + KV_DIM])
    t = _rope(kv[:, :KV_DIM], cos, sa, sb).astype(BF16)
    for i in range(N_KV_HEADS):
        k_ref[i] = t[:, i * HEAD_DIM:(i + 1) * HEAD_DIM]
    t = kv[:, KV_DIM:].astype(BF16)
    for i in range(N_KV_HEADS):
        v_ref[i] = t[:, i * HEAD_DIM:(i + 1) * HEAD_DIM]
    p_off = v_off + KV_DIM
    pool_dim = u_ref.shape[-1]
    u_ref[...] = _dot(h, w_ref[:, p_off:p_off + pool_dim])
    g_off = p_off + pool_dim
    gate_dim = gate_ref.shape[-1]
    chunk = 512
    for j in range(gate_dim // chunk):
        t = _dot(h, w_ref[:, g_off + j * chunk:g_off + (j + 1) * chunk])
        gate_ref[:, j * chunk:(j + 1) * chunk] = _sigmoid(t).astype(BF16)


def _inproj(x, mod3, g, w_in_bf, cos, sa, sb, pool_dim):
    b, s, d = x.shape
    tm = min(PROJ_TILE, s)
    in_dim = w_in_bf.shape[1]
    gate_dim = in_dim - ATTN_DIM - 2 * KV_DIM - pool_dim
    grid = (b, s // tm)
    return pl.pallas_call(
        _inproj_body,
        grid=grid,
        in_specs=[
            pl.BlockSpec((None, tm, d), lambda bi, i: (bi, i, 0)),
            pl.BlockSpec((None, N_ADA, d), lambda bi, i: (bi, 0, 0)),
            pl.BlockSpec((1, d), lambda bi, i: (0, 0)),
            pl.BlockSpec((d, in_dim), lambda bi, i: (0, 0)),
            pl.BlockSpec((tm, LANES), lambda bi, i: (i, 0)),
            pl.BlockSpec((tm, LANES), lambda bi, i: (i, 0)),
            pl.BlockSpec((tm, LANES), lambda bi, i: (i, 0)),
        ],
        out_specs=[
            pl.BlockSpec((None, N_Q_HEADS, tm, HEAD_DIM), lambda bi, i: (bi, 0, i, 0)),
            pl.BlockSpec((None, N_KV_HEADS, tm, HEAD_DIM), lambda bi, i: (bi, 0, i, 0)),
            pl.BlockSpec((None, N_KV_HEADS, tm, HEAD_DIM), lambda bi, i: (bi, 0, i, 0)),
            pl.BlockSpec((None, tm, pool_dim), lambda bi, i: (bi, i, 0)),
            pl.BlockSpec((None, tm, gate_dim), lambda bi, i: (bi, i, 0)),
        ],
        out_shape=[
            jax.ShapeDtypeStruct((b, N_Q_HEADS, s, HEAD_DIM), BF16),
            jax.ShapeDtypeStruct((b, N_KV_HEADS, s, HEAD_DIM), BF16),
            jax.ShapeDtypeStruct((b, N_KV_HEADS, s, HEAD_DIM), BF16),
            jax.ShapeDtypeStruct((b, s, pool_dim), F32),
            jax.ShapeDtypeStruct((b, s, gate_dim), BF16),
        ],
        compiler_params=_params("parallel", "parallel"),
        name="in_proj",
    )(x, mod3, g, w_in_bf, cos, sa, sb)


def _ctxkv_body(ctx_ref, mod_ref, g_ref, w_ref, kc_ref, vc_ref):
    h = _norm_mod(ctx_ref[...], g_ref[...], mod_ref[0:1, :], mod_ref[1:2, :]).astype(BF16)
    t = _dot(h, w_ref[...]).astype(BF16)
    for i in range(N_KV_HEADS):
        kc_ref[i] = t[:, i * HEAD_DIM:(i + 1) * HEAD_DIM]
        vc_ref[i] = t[:, KV_DIM + i * HEAD_DIM:KV_DIM + (i + 1) * HEAD_DIM]


def _ctxkv(ctx, mod_c, g, w_kv_bf):
    b, c, d = ctx.shape
    out = jax.ShapeDtypeStruct((b, N_KV_HEADS, c, HEAD_DIM), BF16)
    spec = pl.BlockSpec((None, N_KV_HEADS, c, HEAD_DIM), lambda bi: (bi, 0, 0, 0))
    return pl.pallas_call(
        _ctxkv_body,
        grid=(b,),
        in_specs=[pl.BlockSpec((None, c, d), lambda bi: (bi, 0, 0)),
                  pl.BlockSpec((N_ADA, d), lambda bi: (0, 0)),
                  pl.BlockSpec((1, d), lambda bi: (0, 0)),
                  pl.BlockSpec((d, 2 * KV_DIM), lambda bi: (0, 0))],
        out_specs=[spec, spec],
        out_shape=[out, out],
        compiler_params=_params("parallel"),
        name="ctx_kv",
    )(ctx, mod_c, g, w_kv_bf)


def _attn_body(sink_ref, q_ref, *refs):
    nkb = ATTN_QBLOCKS + 2
    k_refs, v_refs = refs[:nkb], refs[nkb:2 * nkb]
    kc_ref, vc_ref, lo_ref, hi_ref, o_ref = refs[2 * nkb:]
    n = pl.program_id(1)
    last = pl.num_programs(1) - 1
    rows = Q_PER_KV * ATTN_BLOCK
    row_head = lax.broadcasted_iota(I32, (rows, 1), 0) // ATTN_BLOCK
    for qb in range(ATTN_QBLOCKS):
        bias_lo, bias_hi = lo_ref[...], hi_ref[...]
        if qb == 0:
            bias_lo = bias_lo + jnp.where(n == 0, NEG_INF, 0.0)
        if qb == ATTN_QBLOCKS - 1:
            bias_hi = bias_hi + jnp.where(n == last, NEG_INF, 0.0)
        q_rows = slice(qb * ATTN_BLOCK, (qb + 1) * ATTN_BLOCK)
        outs = []
        for kk in range(N_KV_HEADS):
            qs = q_ref[kk * Q_PER_KV:(kk + 1) * Q_PER_KV, q_rows, :].reshape(rows, HEAD_DIM)
            k_loc = jnp.concatenate([k_refs[qb + j][kk] for j in range(3)], axis=0)
            v_loc = jnp.concatenate([v_refs[qb + j][kk] for j in range(3)], axis=0)
            s_loc = _nt_dot(qs, k_loc)
            s0 = s_loc[:, :ATTN_BLOCK] + bias_lo
            s1 = s_loc[:, ATTN_BLOCK:2 * ATTN_BLOCK]
            s2 = s_loc[:, 2 * ATTN_BLOCK:] + bias_hi
            sc = _nt_dot(qs, kc_ref[kk])
            sink = jnp.zeros((rows, 1), F32)
            for g in range(Q_PER_KV):
                sink = jnp.where(row_head == g, sink_ref[kk * Q_PER_KV + g], sink)
            n_ctx = sc.shape[1] // ATTN_BLOCK
            folded = jnp.maximum(jnp.maximum(s0, s1), s2)
            for j in range(n_ctx):
                folded = jnp.maximum(folded, sc[:, j * ATTN_BLOCK:(j + 1) * ATTN_BLOCK])
            m = jnp.maximum(jnp.max(folded, axis=1, keepdims=True), sink)
            p0, p1, p2, pc = jnp.exp(s0 - m), jnp.exp(s1 - m), jnp.exp(s2 - m), jnp.exp(sc - m)
            folded = p0 + p1 + p2
            for j in range(n_ctx):
                folded = folded + pc[:, j * ATTN_BLOCK:(j + 1) * ATTN_BLOCK]
            denom = jnp.sum(folded, axis=1, keepdims=True) + jnp.exp(sink - m)
            p_loc = jnp.concatenate([p0, p1, p2], axis=1).astype(BF16)
            o = _dot(p_loc, v_loc) + _dot(pc.astype(BF16), vc_ref[kk])
            o = o / denom
            outs += [o[g * ATTN_BLOCK:(g + 1) * ATTN_BLOCK] for g in range(Q_PER_KV)]
        o_ref[q_rows, :] = jnp.concatenate(outs, axis=1).astype(BF16)


def _attention(q, k, v, kc, vc, sink):
    b, _, s, _ = q.shape
    c = kc.shape[2]
    nb = s // ATTN_BLOCK
    assert nb % ATTN_QBLOCKS == 0 and c % ATTN_BLOCK == 0
    rows = Q_PER_KV * ATTN_BLOCK
    q_rows = ATTN_QBLOCKS * ATTN_BLOCK
    qi = jnp.tile(jnp.arange(ATTN_BLOCK), Q_PER_KV)[:, None]
    kj = jnp.arange(ATTN_BLOCK)[None, :]
    bias_lo = jnp.where(kj >= qi, 0.0, NEG_INF).astype(F32)
    bias_hi = jnp.where(kj <= qi, 0.0, NEG_INF).astype(F32)

    def kv_spec(i):
        return pl.BlockSpec(
            (None, N_KV_HEADS, ATTN_BLOCK, HEAD_DIM),
            lambda bi, n: (bi, 0, jnp.clip(ATTN_QBLOCKS * n - 1 + i, 0, nb - 1), 0))

    kv_specs = [kv_spec(i) for i in range(ATTN_QBLOCKS + 2)]
    ctx_spec = pl.BlockSpec((None, N_KV_HEADS, c, HEAD_DIM), lambda bi, n: (bi, 0, 0, 0))
    bias_spec = pl.BlockSpec((rows, ATTN_BLOCK), lambda bi, n: (0, 0))
    return pl.pallas_call(
        _attn_body,
        grid=(b, nb // ATTN_QBLOCKS),
        in_specs=[
            pl.BlockSpec(memory_space=pltpu.SMEM),
            pl.BlockSpec((None, N_Q_HEADS, q_rows, HEAD_DIM), lambda bi, n: (bi, 0, n, 0)),
            *kv_specs, *kv_specs, ctx_spec, ctx_spec, bias_spec, bias_spec,
        ],
        out_specs=pl.BlockSpec((None, q_rows, ATTN_DIM), lambda bi, n: (bi, n, 0)),
        out_shape=jax.ShapeDtypeStruct((b, s, ATTN_DIM), BF16),
        compiler_params=_params("parallel", "parallel"),
        name="window_attn",
    )(sink, q, *([k] * (ATTN_QBLOCKS + 2)), *([v] * (ATTN_QBLOCKS + 2)), kc, vc, bias_lo, bias_hi)


def _pool_delta(u_ref, up, un, inv_cnt, bufs):
    tm = u_ref.shape[0]
    n_ext = tm + 2 * POOL_HALO
    pad = POOL_HALO
    group_dim = u_ref.shape[1] // len(POOL_WINDOWS)
    outs = []
    for g, w in enumerate(POOL_WINDOWS):
        lanes = slice(g * group_dim, (g + 1) * group_dim)
        level = bufs[g]
        level[0][pad:pad + POOL_HALO, :] = up[:, lanes]
        level[0][pad + POOL_HALO:pad + POOL_HALO + tm, :] = u_ref[:, lanes]
        level[0][pad + POOL_HALO + tm:pad + n_ext, :] = un[:, lanes]
        acc = level[0][pl.ds(pad - 1, n_ext), :] + level[0][pl.ds(pad, n_ext), :]
        step = 1
        for j in range(1, len(level)):
            level[j][pad:pad + n_ext, :] = acc
            acc = level[j][pl.ds(pad - step, n_ext), :] + level[j][pl.ds(pad + step, n_ext), :]
            step *= 2
        outs.append(acc[POOL_HALO:POOL_HALO + tm] * inv_cnt[:, g:g + 1] - u_ref[:, lanes])
    return outs


def _route_steps(s, sel, iota_ref, out):
    n_exp = s.shape[0]
    per_group = n_exp // N_EXPERT_GROUPS
    neg = float("-inf")
    iota_g = iota_ref[0:per_group, :]
    scores = []
    for g in range(N_EXPERT_GROUPS):
        blk = sel[g * per_group:(g + 1) * per_group]
        m1 = jnp.max(blk, axis=0, keepdims=True)
        first = jnp.min(jnp.where(blk == m1, iota_g, float(per_group)), axis=0, keepdims=True)
        m2 = jnp.max(jnp.where(iota_g == first, neg, blk), axis=0, keepdims=True)
        scores.append(m1 + m2)
        if g % 2 == 1:
            yield
    gs = jnp.concatenate(scores, axis=0)
    iota_ng = iota_ref[0:N_EXPERT_GROUPS, :]
    gsel = jnp.zeros(iota_ng.shape, F32)
    for _ in range(TOPK_GROUPS):
        m = jnp.max(gs, axis=0, keepdims=True)
        first = jnp.min(jnp.where(gs == m, iota_ng, float(N_EXPERT_GROUPS)), axis=0, keepdims=True)
        hit = iota_ng == first
        gsel = jnp.where(hit, 1.0, gsel)
        gs = jnp.where(hit, neg, gs)
    cur = jnp.concatenate(
        [jnp.where(gsel[g:g + 1] > 0.0, sel[g * per_group:(g + 1) * per_group], NEG_INF)
         for g in range(N_EXPERT_GROUPS)], axis=0)
    yield
    chosen = jnp.zeros(s.shape, F32)
    ids, aff = [], []
    for _ in range(TOP_K):
        m = jnp.max(cur, axis=0, keepdims=True)
        first = jnp.min(jnp.where(cur == m, iota_ref[...], float(n_exp)), axis=0, keepdims=True)
        hit = iota_ref[...] == first
        ids.append(first)
        aff.append(jnp.sum(jnp.where(hit, s, 0.0), axis=0, keepdims=True))
        chosen = jnp.where(hit, 1.0, chosen)
        cur = jnp.where(hit, neg, cur)
        yield
    out.update(ids=ids, aff=aff, chosen=chosen)


def _mixer_body(x_ref, attn_ref, up_ref, u_ref, un_ref, gate_ref, mod_ref, icnt_ref, iota_ref, n2g_ref,
                wpool_ref, pscale_ref, wua_ref, wup_ref, wout_ref, wrh_ref, wrl_ref, rbias_ref,
                wsg_ref, wsu_ref, wsd_ref,
                base_ref, h2_ref, eidx_ref, wts_ref, rank_ref, cnt_ref, score_buf, *pool_bufs, n_t):
    g = pl.program_id(0)
    n_tiles = pl.num_programs(0) - 1
    ti = lax.rem(jnp.minimum(g, n_tiles - 1), n_t)
    slot = g & 1
    tm, d = x_ref.shape
    pool_dim = u_ref.shape[-1]
    n_exp = score_buf.shape[1]

    @pl.when(g == 0)
    def _():
        cnt_ref[...] = jnp.zeros_like(cnt_ref)
        score_buf[1] = jnp.zeros((n_exp, tm), F32)
        for buf in pool_bufs:
            buf[0:POOL_HALO, :] = jnp.zeros((POOL_HALO, buf.shape[1]), F32)
            buf[buf.shape[0] - POOL_HALO:, :] = jnp.zeros((POOL_HALO, buf.shape[1]), F32)
    levels = [w.bit_length() - 1 for w in POOL_WINDOWS]
    group_bufs = [pool_bufs[sum(levels[:i]):sum(levels[:i + 1])] for i in range(len(levels))]

    def routing():
        s = score_buf[1 - slot]
        sel = s + rbias_ref[...]
        res = {}
        yield from _route_steps(s, sel, iota_ref, res)
        ids, aff, chosen = res["ids"], res["aff"], res["chosen"]
        total = aff[0]
        for a in aff[1:]:
            total = total + a
        wts_ref[...] = jnp.concatenate([a / total * ROUTED_SCALE for a in aff], axis=0)
        eidx_ref[...] = jnp.concatenate(ids, axis=0).astype(I32)
        yield
        before = (lax.broadcasted_iota(I32, (tm, tm), 0) < lax.broadcasted_iota(I32, (tm, tm), 1))
        prefix = _dot(chosen.astype(BF16), jnp.where(before, 1.0, 0.0).astype(BF16))
        rank_dense = prefix + cnt_ref[:, 0:1]
        rank_ref[...] = jnp.concatenate(
            [jnp.sum(jnp.where(iota_ref[...] == i, rank_dense, 0.0), axis=0, keepdims=True) for i in ids],
            axis=0).astype(I32)
        real = jnp.where(g > 0, 1.0, 0.0)
        cnt_ref[...] = cnt_ref[...] + real * jnp.sum(chosen, axis=1, keepdims=True)

    def chain():
        zero_halo = jnp.zeros((POOL_HALO, pool_dim), F32)
        u_prev = jnp.where(ti == 0, zero_halo, up_ref[...])
        u_next = jnp.where(ti == n_t - 1, zero_halo, un_ref[...])
        deltas = _pool_delta(u_ref, u_prev, u_next, icnt_ref[...], group_bufs)
        attn_up = _dot(attn_ref[...], wua_ref[...])
        yield
        pool = jnp.concatenate(
            [_dot(dl.astype(BF16), wpool_ref[i]) for i, dl in enumerate(deltas)], axis=1)
        pool = (pool * pscale_ref[...]).astype(BF16)
        yield
        pool_up = _dot(pool, wup_ref[...])
        yield
        y = gate_ref[:, :d] * attn_up.astype(BF16) + gate_ref[:, d:] * pool_up.astype(BF16)
        mix = _dot(y, wout_ref[...])
        yield
        x1 = x_ref[...] + mod_ref[2:3, :] * mix
        h2 = _norm_mod(x1, n2g_ref[...], mod_ref[3:4, :], mod_ref[4:5, :])
        h2_hi = h2.astype(BF16)
        h2_ref[...] = _pack_rows(h2)
        gate = _dot(h2_hi, wsg_ref[...])
        yield
        up = _dot(h2_hi, wsu_ref[...])
        yield
        shared = _dot((_silu(gate) * up).astype(BF16), wsd_ref[...])
        yield
        base_ref[...] = x1 + mod_ref[5:6, :] * shared
        h2_lo = (h2 - h2_hi.astype(F32)).astype(BF16)
        logits = _nt_dot(wrh_ref[...], h2_hi)
        yield
        logits = logits + _nt_dot(wrh_ref[...], h2_lo)
        yield
        logits = logits + _nt_dot(wrl_ref[...], h2_hi)
        score_buf[slot] = _sigmoid(logits)

    stages = [chain(), routing()]
    while stages:
        for gen in list(stages):
            try:
                next(gen)
            except StopIteration:
                stages.remove(gen)


def _mixer(x, attn, u, gates, mod3, n2g, wpool_bf, pscale, wua_bf, wup_bf, wout_bf,
           wr_hi, wr_lo, rbias, wsg_bf, wsu_bf, wsd_bf):
    b, s, d = x.shape
    tm = min(TOKEN_TILE, s)
    n_t = s // tm
    n_tiles = b * n_t
    t_all = b * s
    pool_dim = u.shape[-1]
    n_exp = wr_hi.shape[0]
    sh = wsg_bf.shape[1]
    halo_blocks = tm // POOL_HALO
    n_halo = s // POOL_HALO
    full = lambda shape: pl.BlockSpec(shape, lambda g: (0,) * len(shape))

    def tile(g):
        tg = jnp.minimum(g, n_tiles - 1)
        return tg // n_t, lax.rem(tg, n_t)

    def seq_block(g):
        bi, i = tile(g)
        return bi, i, 0

    def halo_prev(g):
        bi, i = tile(g)
        return bi, jnp.maximum(i * halo_blocks - 1, 0), 0

    def halo_next(g):
        bi, i = tile(g)
        return bi, jnp.minimum((i + 1) * halo_blocks, n_halo - 1), 0

    routed = lambda g: (0, jnp.maximum(g - 1, 0))
    t = jnp.arange(s)
    inv_cnt = jnp.stack(
        [1.0 / (jnp.minimum(t + w // 2, s) - jnp.maximum(t - w // 2, 0)).astype(F32) for w in POOL_WINDOWS], axis=1)
    expert_iota = jnp.broadcast_to(jnp.arange(n_exp, dtype=F32)[:, None], (n_exp, tm))
    return pl.pallas_call(
        functools.partial(_mixer_body, n_t=n_t),
        grid=(n_tiles + 1,),
        in_specs=[
            pl.BlockSpec((None, tm, d), seq_block),
            pl.BlockSpec((None, tm, ATTN_DIM), seq_block),
            pl.BlockSpec((None, POOL_HALO, pool_dim), halo_prev),
            pl.BlockSpec((None, tm, pool_dim), seq_block),
            pl.BlockSpec((None, POOL_HALO, pool_dim), halo_next),
            pl.BlockSpec((None, tm, 2 * d), seq_block),
            pl.BlockSpec((None, N_ADA, d), lambda g: (tile(g)[0], 0, 0)),
            pl.BlockSpec((tm, len(POOL_WINDOWS)), lambda g: (tile(g)[1], 0)),
            full((n_exp, tm)),
            full((1, d)),
            full((len(POOL_WINDOWS), pool_dim // len(POOL_WINDOWS), pool_dim // len(POOL_WINDOWS))),
            full((1, pool_dim)),
            full((ATTN_DIM, d)), full((pool_dim, d)), full((d, d)),
            full((n_exp, d)), full((n_exp, d)), full((n_exp, 1)),
            full((d, sh)), full((d, sh)), full((sh, d)),
        ],
        out_specs=[
            pl.BlockSpec((None, tm, d), seq_block),
            pl.BlockSpec((tm, d // 2), lambda g: (jnp.minimum(g, n_tiles - 1), 0)),
            pl.BlockSpec((TOP_K, tm), routed),
            pl.BlockSpec((TOP_K, tm), routed),
            pl.BlockSpec((TOP_K, tm), routed),
            pl.BlockSpec((n_exp, LANES), lambda g: (0, 0)),
        ],
        out_shape=[
            jax.ShapeDtypeStruct((b, s, d), F32),
            jax.ShapeDtypeStruct((t_all, d // 2), U32),
            jax.ShapeDtypeStruct((TOP_K, t_all), I32),
            jax.ShapeDtypeStruct((TOP_K, t_all), F32),
            jax.ShapeDtypeStruct((TOP_K, t_all), I32),
            jax.ShapeDtypeStruct((n_exp, LANES), F32),
        ],
        scratch_shapes=[pltpu.VMEM((2, n_exp, tm), F32)] + [
            pltpu.VMEM((tm + 4 * POOL_HALO, pool_dim // len(POOL_WINDOWS)), F32)
            for w in POOL_WINDOWS for _ in range(w.bit_length() - 1)],
        compiler_params=_params("arbitrary"),
        name="mixer_router",
    )(x, attn, u, u, u, gates, mod3, inv_cnt, expert_iota, n2g, wpool_bf, pscale, wua_bf, wup_bf, wout_bf,
      wr_hi, wr_lo, rbias, wsg_bf, wsu_bf, wsd_bf)


def _slots_body(eidx_ref, rank_ref, cnt_ref, pos_ref, bstart_ref, *, block_rows):
    n_exp = cnt_ref.shape[0]
    tm = eidx_ref.shape[1]
    cnt = cnt_ref[...]
    padded = jnp.floor((cnt + (block_rows - 1)) / block_rows) * block_rows
    hi = jnp.floor(padded / 256.0)
    lo = padded - hi * 256.0
    below = (lax.broadcasted_iota(I32, (n_exp, n_exp), 1) < lax.broadcasted_iota(I32, (n_exp, n_exp), 0))
    tri = jnp.where(below, 1.0, 0.0).astype(BF16)
    start = 256.0 * _dot(tri, hi.astype(BF16)) + _dot(tri, lo.astype(BF16))
    end = start + padded
    iota_e = lax.broadcasted_iota(I32, (n_exp, tm), 0)
    start_col = start[:, 0:1]
    rows = []
    for k in range(TOP_K):
        hit = iota_e == eidx_ref[k:k + 1, :]
        rows.append(jnp.sum(jnp.where(hit, start_col, 0.0), axis=0, keepdims=True))
    pos_ref[...] = jnp.concatenate(rows, axis=0).astype(I32) + rank_ref[...]

    @pl.when(pl.program_id(0) == 0)
    def _():
        bstart_ref[0:n_exp, :] = (start / block_rows).astype(I32)
        bstart_ref[n_exp:, :] = jnp.broadcast_to(
            (end[n_exp - 1:n_exp, :] / block_rows).astype(I32), (SUBLANES, LANES))


def _slots(eidx, rank, cnt, block_rows):
    t_all = eidx.shape[1]
    n_exp = cnt.shape[0]
    tm = min(TOKEN_TILE, t_all)
    return pl.pallas_call(
        functools.partial(_slots_body, block_rows=block_rows),
        grid=(t_all // tm,),
        in_specs=[pl.BlockSpec((TOP_K, tm), lambda i: (0, i)),
                  pl.BlockSpec((TOP_K, tm), lambda i: (0, i)),
                  pl.BlockSpec((n_exp, LANES), lambda i: (0, 0))],
        out_specs=[pl.BlockSpec((TOP_K, tm), lambda i: (0, i)),
                   pl.BlockSpec((n_exp + SUBLANES, LANES), lambda i: (0, 0))],
        out_shape=[jax.ShapeDtypeStruct((TOP_K, t_all), I32),
                   jax.ShapeDtypeStruct((n_exp + SUBLANES, LANES), I32)],
        compiler_params=_params("arbitrary"),
        name="slot_positions",
    )(eidx, rank, cnt)


def _expert_body(bstart_ref, xs_hbm, wg_ref, wu_ref, wd_ref, y_hbm,
                 xbuf, ybuf, wgu, wdn, sem_in, sem_out, *, block_rows):
    e = pl.program_id(0)
    n_exp = pl.num_programs(0)
    first, last, total = bstart_ref[e], bstart_ref[e + 1], bstart_ref[n_exp]
    f = wg_ref.shape[1]
    ahead = EXPERT_RING - EXPERT_GROUP

    def rows(i):
        return pl.ds(pl.multiple_of(i * block_rows, block_rows), block_rows)

    def slot_of(i):
        return i & (EXPERT_RING - 1)

    def in_copy(i):
        return pltpu.make_async_copy(xs_hbm.at[rows(i)], xbuf.at[slot_of(i)], sem_in.at[slot_of(i)])

    def out_copy(i):
        return pltpu.make_async_copy(ybuf.at[slot_of(i)], y_hbm.at[rows(i)], sem_out.at[slot_of(i)])

    def start_in(i):
        @pl.when(i < total)
        def _():
            in_copy(i).start()

    def wait_out(i):
        @pl.when(i >= 0)
        def _():
            out_copy(i).wait()

    @pl.when(e == 0)
    def _():
        for j in range(ahead):
            start_in(jnp.int32(j))

    wgu[:, :f] = wg_ref[...].astype(BF16)
    wgu[:, f:] = wu_ref[...].astype(BF16)
    wdn[...] = wd_ref[...].astype(BF16)

    def ffn(i, n):
        parts = []
        for j in range(n):
            lo, hi = _unpack_rows(xbuf[slot_of(i + j)])
            parts.append(jnp.concatenate([lo, hi], axis=1).astype(BF16))
        x = jnp.concatenate(parts, axis=0)
        gu = _dot(x, wgu[...])
        y = _pack_rows(_dot((_silu(gu[:, :f]) * gu[:, f:]).astype(BF16), wdn[...]))
        return [y[j * block_rows:(j + 1) * block_rows] for j in range(n)]

    def run_blocks(i, n):
        for j in range(n):
            in_copy(i + j).wait()
        for j in range(n):
            start_in(i + ahead + j)
        res = ffn(i, n)
        for j in range(n):
            wait_out(i + j - EXPERT_RING)
        for j in range(n):
            ybuf[slot_of(i + j)] = res[j]
        for j in range(n):
            out_copy(i + j).start()

    n_blk = last - first
    n_groups = n_blk // EXPERT_GROUP

    def group(p, carry):
        run_blocks(first + EXPERT_GROUP * p, EXPERT_GROUP)
        return carry

    lax.fori_loop(0, n_groups, group, 0)

    done = first + n_groups * EXPERT_GROUP
    size = EXPERT_GROUP // 2
    while size >= 1:
        @pl.when((n_blk & size) != 0)
        def _(done=done, size=size):
            run_blocks(done, size)
        done = done + (n_blk & size)
        size //= 2

    @pl.when(e == n_exp - 1)
    def _():
        for j in range(EXPERT_RING, 0, -1):
            wait_out(total - j)


def _experts(xs, bstart, w_gate, w_up, w_down, block_rows):
    n_slots, dw = xs.shape
    n_exp, d, f = w_gate.shape
    expert = lambda e, bs: (e, 0, 0)
    grid_spec = pltpu.PrefetchScalarGridSpec(
        num_scalar_prefetch=1,
        grid=(n_exp,),
        in_specs=[
            pl.BlockSpec(memory_space=pl.ANY),
            pl.BlockSpec((None, d, f), expert),
            pl.BlockSpec((None, d, f), expert),
            pl.BlockSpec((None, f, d), expert),
        ],
        out_specs=pl.BlockSpec(memory_space=pl.ANY),
        scratch_shapes=[
            pltpu.VMEM((EXPERT_RING, block_rows, dw), U32),
            pltpu.VMEM((EXPERT_RING, block_rows, dw), U32),
            pltpu.VMEM((d, 2 * f), BF16),
            pltpu.VMEM((f, d), BF16),
            pltpu.SemaphoreType.DMA((EXPERT_RING,)),
            pltpu.SemaphoreType.DMA((EXPERT_RING,)),
        ],
    )
    return pl.pallas_call(
        functools.partial(_expert_body, block_rows=block_rows),
        grid_spec=grid_spec,
        out_shape=jax.ShapeDtypeStruct((n_slots, dw), U32),
        compiler_params=_params("arbitrary"),
        name="expert_ffn",
    )(bstart, xs, w_gate, w_up, w_down)


SC_ROWS = 128


def _sc_workers():
    info = plsc.get_sparse_core_info()
    return info.num_cores, info.num_subcores


def _dispatch(h_rows, pos, n_slots):
    t_all, dw = h_rows.shape
    n_cores, n_sub = _sc_workers()
    per_worker = t_all // (n_cores * n_sub)
    assert per_worker * n_cores * n_sub == t_all and per_worker % SC_ROWS == 0
    mesh = plsc.VectorSubcoreMesh(core_axis_name="c", subcore_axis_name="s")

    @functools.partial(
        pl.kernel, mesh=mesh,
        out_type=jax.ShapeDtypeStruct((n_slots, dw), h_rows.dtype),
        scratch_types=[pltpu.VMEM((TOP_K, SC_ROWS), I32), pltpu.VMEM((SC_ROWS, dw), h_rows.dtype),
                       pltpu.SemaphoreType.DMA],
    )
    def body(h_hbm, pos_hbm, out_hbm, idx_v, rows_v, sem):
        base = (lax.axis_index("s") * n_cores + lax.axis_index("c")) * per_worker

        @pl.loop(0, per_worker // SC_ROWS)
        def _(j):
            t0 = pl.multiple_of(base + j * SC_ROWS, SC_ROWS)
            pltpu.sync_copy(pos_hbm.at[:, pl.ds(t0, SC_ROWS)], idx_v)
            pltpu.sync_copy(h_hbm.at[pl.ds(t0, SC_ROWS)], rows_v)
            copies = [pltpu.async_copy(rows_v, out_hbm.at[idx_v.at[k]], sem) for k in range(TOP_K)]
            for cp in copies:
                cp.wait()

    return body(h_rows, pos)


def _gather_back(y_rows, pos):
    top_k, t_all = pos.shape
    dw = y_rows.shape[1]
    n_cores, n_sub = _sc_workers()
    per_worker = t_all // (n_cores * n_sub)
    assert per_worker * n_cores * n_sub == t_all and per_worker % SC_ROWS == 0
    half = SC_ROWS // 2
    mesh = plsc.VectorSubcoreMesh(core_axis_name="c", subcore_axis_name="s")

    @functools.partial(
        pl.kernel, mesh=mesh,
        out_type=jax.ShapeDtypeStruct((top_k, t_all, dw), y_rows.dtype),
        scratch_types=[pltpu.VMEM((top_k, SC_ROWS), I32),
                       pltpu.VMEM((half, dw), y_rows.dtype), pltpu.VMEM((half, dw), y_rows.dtype),
                       pltpu.SemaphoreType.DMA, pltpu.SemaphoreType.DMA, pltpu.SemaphoreType.DMA],
    )
    def body(y_hbm, pos_hbm, out_hbm, idx_v, buf_a, buf_b, sem_g, sem_a, sem_b):
        base = (lax.axis_index("s") * n_cores + lax.axis_index("c")) * per_worker
        bufs, sems = (buf_a, buf_b), (sem_a, sem_b)

        @pl.loop(0, per_worker // SC_ROWS)
        def _(j):
            t0 = pl.multiple_of(base + j * SC_ROWS, SC_ROWS)
            pltpu.sync_copy(pos_hbm.at[:, pl.ds(t0, SC_ROWS)], idx_v)
            pending = [None, None]
            for step in range(2 * top_k):
                k, h = step // 2, step % 2
                slot = step % 2
                if pending[slot] is not None:
                    pending[slot].wait()
                pltpu.async_copy(y_hbm.at[idx_v.at[k, pl.ds(h * half, half)]], bufs[slot], sem_g).wait()
                pending[slot] = pltpu.async_copy(
                    bufs[slot], out_hbm.at[k, pl.ds(t0 + h * half, half)], sems[slot])
            for p in pending:
                p.wait()

    return body(y_rows, pos)


def _combine_body(yg_ref, w_ref, base_ref, mod_ref, fg_ref, o_ref):
    w_cols = jnp.transpose(w_ref[...])
    acc_lo = acc_hi = None
    for k in range(TOP_K):
        lo, hi = _unpack_rows(yg_ref[k])
        wk = w_cols[:, k:k + 1]
        acc_lo = lo * wk if acc_lo is None else acc_lo + lo * wk
        acc_hi = hi * wk if acc_hi is None else acc_hi + hi * wk
    acc = jnp.concatenate([acc_lo, acc_hi], axis=1)
    x2 = base_ref[...] + mod_ref[5:6, :] * acc
    ms = jnp.mean(x2 * x2, axis=-1, keepdims=True)
    o_ref[...] = (x2 * lax.rsqrt(ms + NORM_EPS)) * fg_ref[...]


def _combine(yg, wts, base, mod3, final_g):
    b, s, d = base.shape
    tm = min(256, s)
    n_t = s // tm
    return pl.pallas_call(
        _combine_body,
        grid=(b, n_t),
        in_specs=[pl.BlockSpec((TOP_K, tm, d // 2), lambda bi, i: (0, bi * n_t + i, 0)),
                  pl.BlockSpec((TOP_K, tm), lambda bi, i: (0, bi * n_t + i)),
                  pl.BlockSpec((None, tm, d), lambda bi, i: (bi, i, 0)),
                  pl.BlockSpec((None, N_ADA, d), lambda bi, i: (bi, 0, 0)),
                  pl.BlockSpec((1, d), lambda bi, i: (0, 0))],
        out_specs=pl.BlockSpec((None, tm, d), lambda bi, i: (bi, i, 0)),
        out_shape=jax.ShapeDtypeStruct((b, s, d), F32),
        compiler_params=_params("parallel", "parallel"),
        name="combine_norm",
    )(yg, wts, base, mod3, final_g)


def _rope_tables(seq_len):
    rows = seq_len // GRID_W
    row = jnp.repeat(jnp.arange(rows), GRID_W).astype(F32)
    col = jnp.tile(jnp.arange(GRID_W), rows).astype(F32)
    n_freq = HEAD_DIM // 4
    inv = ROPE_THETA ** (-jnp.arange(n_freq, dtype=F32) / n_freq)
    ar, ac = row[:, None] * inv, col[:, None] * inv
    zeros = jnp.zeros_like(ar)
    reps = LANES // HEAD_DIM
    cos = jnp.tile(jnp.concatenate([jnp.cos(ar), jnp.cos(ar), jnp.cos(ac), jnp.cos(ac)], 1), (1, reps))
    sin_a = jnp.tile(jnp.concatenate([-jnp.sin(ar), zeros, -jnp.sin(ac), zeros], 1), (1, reps))
    sin_b = jnp.tile(jnp.concatenate([zeros, jnp.sin(ar), zeros, jnp.sin(ac)], 1), (1, reps))
    return cos, sin_a, sin_b


def kernel(x, c, ctx, c_ctx, w_ada, b_ada, norm1_g, w_in, attn_sink, w_pool, pool_scale,
           w_up_attn, w_up_pool, w_out, norm2_g, w_router, router_bias,
           w_exp_gate, w_exp_up, w_exp_down, w_sh_gate, w_sh_up, w_sh_down, final_g):
    b, s, d = x.shape
    assert w_ada.shape[0] == 1, "single-layer block"
    assert s % ATTN_BLOCK == 0 and s % GRID_W == 0 and d % LANES == 0
    pool_dim = w_up_pool.shape[1]
    n_exp = w_router.shape[-1]
    t_all = b * s

    pad_rows = -(-(b + 1) // SUBLANES) * SUBLANES
    c_rows = jnp.concatenate([c, c_ctx[None, :], jnp.zeros((pad_rows - b - 1, d), F32)], axis=0)
    mod3 = _ada(c_rows, w_ada[0], b_ada[0]).reshape(pad_rows, N_ADA, d)

    w_in_bf = w_in[0].astype(BF16)
    g1 = norm1_g[0].reshape(1, d)
    cos, sin_a, sin_b = _rope_tables(s)
    q, k, v, u, gates = _inproj(x, mod3, g1, w_in_bf, cos, sin_a, sin_b, pool_dim)
    kc, vc = _ctxkv(ctx, mod3[b], g1, w_in_bf[:, ATTN_DIM:ATTN_DIM + 2 * KV_DIM])
    attn = _attention(q, k, v, kc, vc, attn_sink[0])

    w_r_t = w_router[0].T
    w_r_hi = w_r_t.astype(BF16)
    w_r_lo = (w_r_t - w_r_hi.astype(F32)).astype(BF16)
    base, h2, eidx, wts, rank, cnt = _mixer(
        x, attn, u, gates, mod3, norm2_g[0].reshape(1, d), w_pool[0].astype(BF16),
        pool_scale[0].reshape(1, pool_dim), w_up_attn[0].astype(BF16), w_up_pool[0].astype(BF16),
        w_out[0].astype(BF16), w_r_hi, w_r_lo, router_bias[0].reshape(n_exp, 1),
        w_sh_gate[0].astype(BF16), w_sh_up[0].astype(BF16), w_sh_down[0].astype(BF16))

    br = EXPERT_BLOCK_ROWS
    n_blocks = (t_all * TOP_K + n_exp * (br - 1) + br - 1) // br
    n_slots = n_blocks * br
    pos, bstart = _slots(eidx, rank, cnt, br)
    xs = _dispatch(h2, pos, n_slots)
    y = _experts(xs, bstart[:n_exp + 1, 0], w_exp_gate[0], w_exp_up[0], w_exp_down[0], br)
    yg = _gather_back(y, pos)

    return _combine(yg, wts, base, mod3, final_g.reshape(1, d))
```

```python
import functools

import jax
import jax.numpy as jnp
from jax import lax
from jax.experimental import pallas as pl
from jax.experimental.pallas import tpu as pltpu
from jax.experimental.pallas import tpu_sc as plsc

F32 = jnp.float32
BF16 = jnp.bfloat16
I32 = jnp.int32
U32 = jnp.uint32

GRID_W = 64
HEAD_DIM = 64
N_Q_HEADS = 8
N_KV_HEADS = 2
Q_PER_KV = N_Q_HEADS // N_KV_HEADS
ATTN_DIM = N_Q_HEADS * HEAD_DIM
KV_DIM = N_KV_HEADS * HEAD_DIM
ATTN_BLOCK = 128
ATTN_SCALE = HEAD_DIM ** -0.5
ROPE_THETA = 10000.0
POOL_WINDOWS = (2, 4, 8, 16)
POOL_HALO = 8
N_EXPERT_GROUPS = 8
TOPK_GROUPS = 4
TOP_K = 8
ROUTED_SCALE = 2.5
N_ADA = 6
NORM_EPS = 1e-6
NEG_INF = -1e30
LANES = 128
SUBLANES = 8
ATTN_QBLOCKS = 4
TOKEN_TILE = 512
SLOT_TILE = 2048
COMBINE_TILE = 512
PROJ_TILE = 1024
EXPERT_BLOCK_ROWS = 256
EXPERT_GROUP = 4
EXPERT_RING = 16
VMEM_LIMIT = 56 * 1024 * 1024


def _sigmoid(x):
    return 1.0 / (1.0 + jnp.exp(-x))


def _silu(x):
    return x * _sigmoid(x)


def _nt_dot(a, b):
    return lax.dot_general(a, b, (((1,), (1,)), ((), ())), preferred_element_type=F32)


def _dot(a, b):
    return jnp.dot(a, b, preferred_element_type=F32)


def _pack_rows(x):
    n = x.shape[1] // 2
    bits = lax.bitcast_convert_type(x.astype(BF16).astype(F32), U32)
    return (bits[:, :n] >> 16) | (bits[:, n:] & jnp.uint32(0xFFFF0000))


def _unpack_rows(w):
    lo = lax.bitcast_convert_type(w << 16, F32)
    hi = lax.bitcast_convert_type(w & jnp.uint32(0xFFFF0000), F32)
    return lo, hi


def _params(*sem):
    return pltpu.CompilerParams(dimension_semantics=sem, vmem_limit_bytes=VMEM_LIMIT)


def _ada_body(c_ref, w_ref, b_ref, o_ref):
    s = _silu(c_ref[...])
    o_ref[...] = jnp.dot(s, w_ref[...], preferred_element_type=F32,
                         precision=lax.Precision.HIGHEST) + b_ref[...]


def _ada(c_rows, w_ada, b_ada):
    rows, d = c_rows.shape
    n = w_ada.shape[1]
    bn = d
    return pl.pallas_call(
        _ada_body,
        grid=(n // bn,),
        in_specs=[pl.BlockSpec((rows, d), lambda j: (0, 0)),
                  pl.BlockSpec((d, bn), lambda j: (0, j)),
                  pl.BlockSpec((1, bn), lambda j: (0, j))],
        out_specs=pl.BlockSpec((rows, bn), lambda j: (0, j)),
        out_shape=jax.ShapeDtypeStruct((rows, n), F32),
        compiler_params=_params("arbitrary"),
        name="ada_mod",
    )(c_rows, w_ada, b_ada.reshape(1, n))


def _norm_mod(x, g, shift, scale):
    ms = jnp.mean(x * x, axis=-1, keepdims=True)
    y = x * lax.rsqrt(ms + NORM_EPS)
    return y * (g * (1.0 + scale)) + shift


def _rope(t, cos, sin_a, sin_b):
    return (t * cos + pltpu.roll(t, LANES - HEAD_DIM // 4, 1) * sin_a
            + pltpu.roll(t, HEAD_DIM // 4, 1) * sin_b)


def _inproj_body(x_ref, mod_ref, g_ref, w_ref, cos_ref, sa_ref, sb_ref,
                 q_ref, k_ref, v_ref, u_ref, gate_ref):
    h = _norm_mod(x_ref[...], g_ref[...], mod_ref[0:1, :], mod_ref[1:2, :]).astype(BF16)
    cos, sa, sb = cos_ref[...], sa_ref[...], sb_ref[...]
    heads_per_chunk = LANES // HEAD_DIM
    wide = 2 * LANES
    for j in range(ATTN_DIM // wide):
        t2 = _dot(h, w_ref[:, j * wide:(j + 1) * wide])
        for c in range(wide // LANES):
            t = (_rope(t2[:, c * LANES:(c + 1) * LANES], cos, sa, sb) * ATTN_SCALE).astype(BF16)
            for i in range(heads_per_chunk):
                head = (j * (wide // LANES) + c) * heads_per_chunk + i
                q_ref[head] = t[:, i * HEAD_DIM:(i + 1) * HEAD_DIM]
    k_off = ATTN_DIM
    v_off = k_off + KV_DIM
    kv = _dot(h, w_ref[:, k_off:v_off + KV_DIM])
    t = _rope(kv[:, :KV_DIM], cos, sa, sb).astype(BF16)
    for i in range(N_KV_HEADS):
        k_ref[i] = t[:, i * HEAD_DIM:(i + 1) * HEAD_DIM]
    t = kv[:, KV_DIM:].astype(BF16)
    for i in range(N_KV_HEADS):
        v_ref[i] = t[:, i * HEAD_DIM:(i + 1) * HEAD_DIM]
    p_off = v_off + KV_DIM
    pool_dim = u_ref.shape[-1]
    u_ref[...] = _dot(h, w_ref[:, p_off:p_off + pool_dim])
    g_off = p_off + pool_dim
    gate_dim = gate_ref.shape[-1]
    chunk = 512
    for j in range(gate_dim // chunk):
        t = _dot(h, w_ref[:, g_off + j * chunk:g_off + (j + 1) * chunk])
        gate_ref[:, j * chunk:(j + 1) * chunk] = _sigmoid(t).astype(BF16)


def _inproj(x, mod3, g, w_in_bf, cos, sa, sb, pool_dim):
    b, s, d = x.shape
    tm = min(PROJ_TILE, s)
    in_dim = w_in_bf.shape[1]
    gate_dim = in_dim - ATTN_DIM - 2 * KV_DIM - pool_dim
    grid = (b, s // tm)
    return pl.pallas_call(
        _inproj_body,
        grid=grid,
        in_specs=[
            pl.BlockSpec((None, tm, d), lambda bi, i: (bi, i, 0)),
            pl.BlockSpec((None, N_ADA, d), lambda bi, i: (bi, 0, 0)),
            pl.BlockSpec((1, d), lambda bi, i: (0, 0)),
            pl.BlockSpec((d, in_dim), lambda bi, i: (0, 0)),
            pl.BlockSpec((tm, LANES), lambda bi, i: (i, 0)),
            pl.BlockSpec((tm, LANES), lambda bi, i: (i, 0)),
            pl.BlockSpec((tm, LANES), lambda bi, i: (i, 0)),
        ],
        out_specs=[
            pl.BlockSpec((None, N_Q_HEADS, tm, HEAD_DIM), lambda bi, i: (bi, 0, i, 0)),
            pl.BlockSpec((None, N_KV_HEADS, tm, HEAD_DIM), lambda bi, i: (bi, 0, i, 0)),
            pl.BlockSpec((None, N_KV_HEADS, tm, HEAD_DIM), lambda bi, i: (bi, 0, i, 0)),
            pl.BlockSpec((None, tm, pool_dim), lambda bi, i: (bi, i, 0)),
            pl.BlockSpec((None, tm, gate_dim), lambda bi, i: (bi, i, 0)),
        ],
        out_shape=[
            jax.ShapeDtypeStruct((b, N_Q_HEADS, s, HEAD_DIM), BF16),
            jax.ShapeDtypeStruct((b, N_KV_HEADS, s, HEAD_DIM), BF16),
            jax.ShapeDtypeStruct((b, N_KV_HEADS, s, HEAD_DIM), BF16),
            jax.ShapeDtypeStruct((b, s, pool_dim), F32),
            jax.ShapeDtypeStruct((b, s, gate_dim), BF16),
        ],
        compiler_params=_params("parallel", "parallel"),
        name="in_proj",
    )(x, mod3, g, w_in_bf, cos, sa, sb)


def _ctxkv_body(ctx_ref, mod_ref, g_ref, w_ref, kc_ref, vc_ref):
    h = _norm_mod(ctx_ref[...], g_ref[...], mod_ref[0:1, :], mod_ref[1:2, :]).astype(BF16)
    t = _dot(h, w_ref[...]).astype(BF16)
    for i in range(N_KV_HEADS):
        kc_ref[i] = t[:, i * HEAD_DIM:(i + 1) * HEAD_DIM]
        vc_ref[i] = t[:, KV_DIM + i * HEAD_DIM:KV_DIM + (i + 1) * HEAD_DIM]


def _ctxkv(ctx, mod_c, g, w_kv_bf):
    b, c, d = ctx.shape
    out = jax.ShapeDtypeStruct((b, N_KV_HEADS, c, HEAD_DIM), BF16)
    spec = pl.BlockSpec((None, N_KV_HEADS, c, HEAD_DIM), lambda bi: (bi, 0, 0, 0))
    return pl.pallas_call(
        _ctxkv_body,
        grid=(b,),
        in_specs=[pl.BlockSpec((None, c, d), lambda bi: (bi, 0, 0)),
                  pl.BlockSpec((N_ADA, d), lambda bi: (0, 0)),
                  pl.BlockSpec((1, d), lambda bi: (0, 0)),
                  pl.BlockSpec((d, 2 * KV_DIM), lambda bi: (0, 0))],
        out_specs=[spec, spec],
        out_shape=[out, out],
        compiler_params=_params("parallel"),
        name="ctx_kv",
    )(ctx, mod_c, g, w_kv_bf)


def _attn_body(sink_ref, q_ref, *refs):
    nkb = ATTN_QBLOCKS + 2
    k_refs, v_refs = refs[:nkb], refs[nkb:2 * nkb]
    kc_ref, vc_ref, lo_ref, hi_ref, o_ref = refs[2 * nkb:]
    n = pl.program_id(1)
    last = pl.num_programs(1) - 1
    rows = Q_PER_KV * ATTN_BLOCK
    row_head = lax.broadcasted_iota(I32, (rows, 1), 0) // ATTN_BLOCK
    for qb in range(ATTN_QBLOCKS):
        bias_lo, bias_hi = lo_ref[...], hi_ref[...]
        if qb == 0:
            bias_lo = bias_lo + jnp.where(n == 0, NEG_INF, 0.0)
        if qb == ATTN_QBLOCKS - 1:
            bias_hi = bias_hi + jnp.where(n == last, NEG_INF, 0.0)
        q_rows = slice(qb * ATTN_BLOCK, (qb + 1) * ATTN_BLOCK)
        outs = []
        for kk in range(N_KV_HEADS):
            qs = q_ref[kk * Q_PER_KV:(kk + 1) * Q_PER_KV, q_rows, :].reshape(rows, HEAD_DIM)
            k_loc = jnp.concatenate([k_refs[qb + j][kk] for j in range(3)], axis=0)
            v_loc = jnp.concatenate([v_refs[qb + j][kk] for j in range(3)], axis=0)
            s_loc = _nt_dot(qs, k_loc)
            s0 = s_loc[:, :ATTN_BLOCK] + bias_lo
            s1 = s_loc[:, ATTN_BLOCK:2 * ATTN_BLOCK]
            s2 = s_loc[:, 2 * ATTN_BLOCK:] + bias_hi
            sc = _nt_dot(qs, kc_ref[kk])
            sink = jnp.zeros((rows, 1), F32)
            for g in range(Q_PER_KV):
                sink = jnp.where(row_head == g, sink_ref[kk * Q_PER_KV + g], sink)
            n_ctx = sc.shape[1] // ATTN_BLOCK
            folded = jnp.maximum(jnp.maximum(s0, s1), s2)
            for j in range(n_ctx):
                folded = jnp.maximum(folded, sc[:, j * ATTN_BLOCK:(j + 1) * ATTN_BLOCK])
            m = jnp.maximum(jnp.max(folded, axis=1, keepdims=True), sink)
            p0, p1, p2, pc = jnp.exp(s0 - m), jnp.exp(s1 - m), jnp.exp(s2 - m), jnp.exp(sc - m)
            folded = p0 + p1 + p2
            for j in range(n_ctx):
                folded = folded + pc[:, j * ATTN_BLOCK:(j + 1) * ATTN_BLOCK]
            denom = jnp.sum(folded, axis=1, keepdims=True) + jnp.exp(sink - m)
            p_loc = jnp.concatenate([p0, p1, p2], axis=1).astype(BF16)
            o = _dot(p_loc, v_loc) + _dot(pc.astype(BF16), vc_ref[kk])
            o = o / denom
            outs += [o[g * ATTN_BLOCK:(g + 1) * ATTN_BLOCK] for g in range(Q_PER_KV)]
        o_ref[q_rows, :] = jnp.concatenate(outs, axis=1).astype(BF16)


def _attention(q, k, v, kc, vc, sink):
    b, _, s, _ = q.shape
    c = kc.shape[2]
    nb = s // ATTN_BLOCK
    assert nb % ATTN_QBLOCKS == 0 and c % ATTN_BLOCK == 0
    rows = Q_PER_KV * ATTN_BLOCK
    q_rows = ATTN_QBLOCKS * ATTN_BLOCK
    qi = jnp.tile(jnp.arange(ATTN_BLOCK), Q_PER_KV)[:, None]
    kj = jnp.arange(ATTN_BLOCK)[None, :]
    bias_lo = jnp.where(kj >= qi, 0.0, NEG_INF).astype(F32)
    bias_hi = jnp.where(kj <= qi, 0.0, NEG_INF).astype(F32)

    def kv_spec(i):
        return pl.BlockSpec(
            (None, N_KV_HEADS, ATTN_BLOCK, HEAD_DIM),
            lambda bi, n: (bi, 0, jnp.clip(ATTN_QBLOCKS * n - 1 + i, 0, nb - 1), 0))

    kv_specs = [kv_spec(i) for i in range(ATTN_QBLOCKS + 2)]
    ctx_spec = pl.BlockSpec((None, N_KV_HEADS, c, HEAD_DIM), lambda bi, n: (bi, 0, 0, 0))
    bias_spec = pl.BlockSpec((rows, ATTN_BLOCK), lambda bi, n: (0, 0))
    return pl.pallas_call(
        _attn_body,
        grid=(b, nb // ATTN_QBLOCKS),
        in_specs=[
            pl.BlockSpec(memory_space=pltpu.SMEM),
            pl.BlockSpec((None, N_Q_HEADS, q_rows, HEAD_DIM), lambda bi, n: (bi, 0, n, 0)),
            *kv_specs, *kv_specs, ctx_spec, ctx_spec, bias_spec, bias_spec,
        ],
        out_specs=pl.BlockSpec((None, q_rows, ATTN_DIM), lambda bi, n: (bi, n, 0)),
        out_shape=jax.ShapeDtypeStruct((b, s, ATTN_DIM), BF16),
        compiler_params=_params("parallel", "parallel"),
        name="window_attn",
    )(sink, q, *([k] * (ATTN_QBLOCKS + 2)), *([v] * (ATTN_QBLOCKS + 2)), kc, vc, bias_lo, bias_hi)


def _pool_delta(u_ref, up, un, inv_cnt, bufs):
    tm = u_ref.shape[0]
    n_ext = tm + 2 * POOL_HALO
    pad = POOL_HALO
    group_dim = u_ref.shape[1] // len(POOL_WINDOWS)
    outs = []
    for g, w in enumerate(POOL_WINDOWS):
        lanes = slice(g * group_dim, (g + 1) * group_dim)
        level = bufs[g]
        level[0][pad:pad + POOL_HALO, :] = up[:, lanes]
        level[0][pad + POOL_HALO:pad + POOL_HALO + tm, :] = u_ref[:, lanes]
        level[0][pad + POOL_HALO + tm:pad + n_ext, :] = un[:, lanes]
        acc = level[0][pl.ds(pad - 1, n_ext), :] + level[0][pl.ds(pad, n_ext), :]
        step = 1
        for j in range(1, len(level)):
            level[j][pad:pad + n_ext, :] = acc
            acc = level[j][pl.ds(pad - step, n_ext), :] + level[j][pl.ds(pad + step, n_ext), :]
            step *= 2
        outs.append(acc[POOL_HALO:POOL_HALO + tm] * inv_cnt[:, g:g + 1] - u_ref[:, lanes])
    return outs


def _route_steps(s, sel, iota_ref, out):
    n_exp = s.shape[0]
    per_group = n_exp // N_EXPERT_GROUPS
    neg = float("-inf")
    iota_g = iota_ref[0:per_group, :]
    scores = []
    for g in range(N_EXPERT_GROUPS):
        blk = sel[g * per_group:(g + 1) * per_group]
        m1 = jnp.max(blk, axis=0, keepdims=True)
        first = jnp.min(jnp.where(blk == m1, iota_g, float(per_group)), axis=0, keepdims=True)
        m2 = jnp.max(jnp.where(iota_g == first, neg, blk), axis=0, keepdims=True)
        scores.append(m1 + m2)
        if g % 2 == 1:
            yield
    gs = jnp.concatenate(scores, axis=0)
    iota_ng = iota_ref[0:N_EXPERT_GROUPS, :]
    gsel = jnp.zeros(iota_ng.shape, F32)
    for _ in range(TOPK_GROUPS):
        m = jnp.max(gs, axis=0, keepdims=True)
        first = jnp.min(jnp.where(gs == m, iota_ng, float(N_EXPERT_GROUPS)), axis=0, keepdims=True)
        hit = iota_ng == first
        gsel = jnp.where(hit, 1.0, gsel)
        gs = jnp.where(hit, neg, gs)
    cur = jnp.concatenate(
        [jnp.where(gsel[g:g + 1] > 0.0, sel[g * per_group:(g + 1) * per_group], NEG_INF)
         for g in range(N_EXPERT_GROUPS)], axis=0)
    yield
    chosen = jnp.zeros(s.shape, F32)
    ids, aff = [], []
    for _ in range(TOP_K):
        m = jnp.max(cur, axis=0, keepdims=True)
        first = jnp.min(jnp.where(cur == m, iota_ref[...], float(n_exp)), axis=0, keepdims=True)
        hit = iota_ref[...] == first
        ids.append(first)
        aff.append(jnp.sum(jnp.where(hit, s, 0.0), axis=0, keepdims=True))
        chosen = jnp.where(hit, 1.0, chosen)
        cur = jnp.where(hit, neg, cur)
        yield
    out.update(ids=ids, aff=aff, chosen=chosen)


def _mixer_body(x_ref, attn_ref, up_ref, u_ref, un_ref, gate_ref, mod_ref, icnt_ref, iota_ref, n2g_ref,
                wpool_ref, pscale_ref, wua_ref, wup_ref, wout_ref, wrh_ref, wrl_ref, rbias_ref,
                wsg_ref, wsu_ref, wsd_ref,
                base_ref, h2_ref, eidx_ref, wts_ref, rank_ref, cnt_ref, score_buf, *pool_bufs, n_t):
    g = pl.program_id(0)
    n_tiles = pl.num_programs(0) - 1
    ti = lax.rem(jnp.minimum(g, n_tiles - 1), n_t)
    slot = g & 1
    tm, d = x_ref.shape
    pool_dim = u_ref.shape[-1]
    n_exp = score_buf.shape[1]

    @pl.when(g == 0)
    def _():
        cnt_ref[...] = jnp.zeros_like(cnt_ref)
        score_buf[1] = jnp.zeros((n_exp, tm), F32)
        for buf in pool_bufs:
            buf[0:POOL_HALO, :] = jnp.zeros((POOL_HALO, buf.shape[1]), F32)
            buf[buf.shape[0] - POOL_HALO:, :] = jnp.zeros((POOL_HALO, buf.shape[1]), F32)
    levels = [w.bit_length() - 1 for w in POOL_WINDOWS]
    group_bufs = [pool_bufs[sum(levels[:i]):sum(levels[:i + 1])] for i in range(len(levels))]

    def routing():
        s = score_buf[1 - slot]
        sel = s + rbias_ref[...]
        res = {}
        yield from _route_steps(s, sel, iota_ref, res)
        ids, aff, chosen = res["ids"], res["aff"], res["chosen"]
        total = aff[0]
        for a in aff[1:]:
            total = total + a
        wts_ref[...] = jnp.concatenate([a / total * ROUTED_SCALE for a in aff], axis=0)
        eidx_ref[...] = jnp.concatenate(ids, axis=0).astype(I32)
        yield
        before = (lax.broadcasted_iota(I32, (tm, tm), 0) < lax.broadcasted_iota(I32, (tm, tm), 1))
        prefix = _dot(chosen.astype(BF16), jnp.where(before, 1.0, 0.0).astype(BF16))
        rank_dense = prefix + cnt_ref[:, 0:1]
        rank_ref[...] = jnp.concatenate(
            [jnp.sum(jnp.where(iota_ref[...] == i, rank_dense, 0.0), axis=0, keepdims=True) for i in ids],
            axis=0).astype(I32)
        real = jnp.where(g > 0, 1.0, 0.0)
        cnt_ref[...] = cnt_ref[...] + real * jnp.sum(chosen, axis=1, keepdims=True)

    def chain():
        zero_halo = jnp.zeros((POOL_HALO, pool_dim), F32)
        u_prev = jnp.where(ti == 0, zero_halo, up_ref[...])
        u_next = jnp.where(ti == n_t - 1, zero_halo, un_ref[...])
        deltas = _pool_delta(u_ref, u_prev, u_next, icnt_ref[...], group_bufs)
        attn_up = _dot(attn_ref[...], wua_ref[...])
        yield
        pool = jnp.concatenate(
            [_dot(dl.astype(BF16), wpool_ref[i]) for i, dl in enumerate(deltas)], axis=1)
        pool = (pool * pscale_ref[...]).astype(BF16)
        yield
        pool_up = _dot(pool, wup_ref[...])
        yield
        y = gate_ref[:, :d] * attn_up.astype(BF16) + gate_ref[:, d:] * pool_up.astype(BF16)
        mix = _dot(y, wout_ref[...])
        yield
        x1 = x_ref[...] + mod_ref[2:3, :] * mix
        h2 = _norm_mod(x1, n2g_ref[...], mod_ref[3:4, :], mod_ref[4:5, :])
        h2_hi = h2.astype(BF16)
        h2_ref[...] = _pack_rows(h2)
        gate = _dot(h2_hi, wsg_ref[...])
        yield
        up = _dot(h2_hi, wsu_ref[...])
        yield
        shared = _dot((_silu(gate) * up).astype(BF16), wsd_ref[...])
        yield
        base_ref[...] = x1 + mod_ref[5:6, :] * shared
        h2_lo = (h2 - h2_hi.astype(F32)).astype(BF16)
        logits = _nt_dot(wrh_ref[...], h2_hi)
        yield
        logits = logits + _nt_dot(wrh_ref[...], h2_lo)
        yield
        logits = logits + _nt_dot(wrl_ref[...], h2_hi)
        score_buf[slot] = _sigmoid(logits)

    stages = [chain(), routing()]
    while stages:
        for gen in list(stages):
            try:
                next(gen)
            except StopIteration:
                stages.remove(gen)


def _mixer(x, attn, u, gates, mod3, n2g, wpool_bf, pscale, wua_bf, wup_bf, wout_bf,
           wr_hi, wr_lo, rbias, wsg_bf, wsu_bf, wsd_bf):
    b, s, d = x.shape
    tm = min(TOKEN_TILE, s)
    n_t = s // tm
    n_tiles = b * n_t
    t_all = b * s
    pool_dim = u.shape[-1]
    n_exp = wr_hi.shape[0]
    sh = wsg_bf.shape[1]
    halo_blocks = tm // POOL_HALO
    n_halo = s // POOL_HALO
    full = lambda shape: pl.BlockSpec(shape, lambda g: (0,) * len(shape))

    def tile(g):
        tg = jnp.minimum(g, n_tiles - 1)
        return tg // n_t, lax.rem(tg, n_t)

    def seq_block(g):
        bi, i = tile(g)
        return bi, i, 0

    def halo_prev(g):
        bi, i = tile(g)
        return bi, jnp.maximum(i * halo_blocks - 1, 0), 0

    def halo_next(g):
        bi, i = tile(g)
        return bi, jnp.minimum((i + 1) * halo_blocks, n_halo - 1), 0

    routed = lambda g: (0, jnp.maximum(g - 1, 0))
    t = jnp.arange(s)
    inv_cnt = jnp.stack(
        [1.0 / (jnp.minimum(t + w // 2, s) - jnp.maximum(t - w // 2, 0)).astype(F32) for w in POOL_WINDOWS], axis=1)
    expert_iota = jnp.broadcast_to(jnp.arange(n_exp, dtype=F32)[:, None], (n_exp, tm))
    return pl.pallas_call(
        functools.partial(_mixer_body, n_t=n_t),
        grid=(n_tiles + 1,),
        in_specs=[
            pl.BlockSpec((None, tm, d), seq_block),
            pl.BlockSpec((None, tm, ATTN_DIM), seq_block),
            pl.BlockSpec((None, POOL_HALO, pool_dim), halo_prev),
            pl.BlockSpec((None, tm, pool_dim), seq_block),
            pl.BlockSpec((None, POOL_HALO, pool_dim), halo_next),
            pl.BlockSpec((None, tm, 2 * d), seq_block),
            pl.BlockSpec((None, N_ADA, d), lambda g: (tile(g)[0], 0, 0)),
            pl.BlockSpec((tm, len(POOL_WINDOWS)), lambda g: (tile(g)[1], 0)),
            full((n_exp, tm)),
            full((1, d)),
            full((len(POOL_WINDOWS), pool_dim // len(POOL_WINDOWS), pool_dim // len(POOL_WINDOWS))),
            full((1, pool_dim)),
            full((ATTN_DIM, d)), full((pool_dim, d)), full((d, d)),
            full((n_exp, d)), full((n_exp, d)), full((n_exp, 1)),
            full((d, sh)), full((d, sh)), full((sh, d)),
        ],
        out_specs=[
            pl.BlockSpec((None, tm, d), seq_block),
            pl.BlockSpec((tm, d // 2), lambda g: (jnp.minimum(g, n_tiles - 1), 0)),
            pl.BlockSpec((TOP_K, tm), routed),
            pl.BlockSpec((TOP_K, tm), routed),
            pl.BlockSpec((TOP_K, tm), routed),
            pl.BlockSpec((n_exp, LANES), lambda g: (0, 0)),
        ],
        out_shape=[
            jax.ShapeDtypeStruct((b, s, d), F32),
            jax.ShapeDtypeStruct((t_all, d // 2), U32),
            jax.ShapeDtypeStruct((TOP_K, t_all), I32),
            jax.ShapeDtypeStruct((TOP_K, t_all), F32),
            jax.ShapeDtypeStruct((TOP_K, t_all), I32),
            jax.ShapeDtypeStruct((n_exp, LANES), F32),
        ],
        scratch_shapes=[pltpu.VMEM((2, n_exp, tm), F32)] + [
            pltpu.VMEM((tm + 4 * POOL_HALO, pool_dim // len(POOL_WINDOWS)), F32)
            for w in POOL_WINDOWS for _ in range(w.bit_length() - 1)],
        compiler_params=_params("arbitrary"),
        name="mixer_router",
    )(x, attn, u, u, u, gates, mod3, inv_cnt, expert_iota, n2g, wpool_bf, pscale, wua_bf, wup_bf, wout_bf,
      wr_hi, wr_lo, rbias, wsg_bf, wsu_bf, wsd_bf)


def _slots_body(eidx_ref, rank_ref, cnt_ref, pos_ref, bstart_ref, *, block_rows):
    n_exp = cnt_ref.shape[0]
    tm = eidx_ref.shape[1]
    cnt = cnt_ref[...]
    padded = jnp.floor((cnt + (block_rows - 1)) / block_rows) * block_rows
    hi = jnp.floor(padded / 256.0)
    lo = padded - hi * 256.0
    below = (lax.broadcasted_iota(I32, (n_exp, n_exp), 1) < lax.broadcasted_iota(I32, (n_exp, n_exp), 0))
    tri = jnp.where(below, 1.0, 0.0).astype(BF16)
    start = 256.0 * _dot(tri, hi.astype(BF16)) + _dot(tri, lo.astype(BF16))
    end = start + padded
    iota_e = lax.broadcasted_iota(I32, (n_exp, tm), 0)
    start_col = start[:, 0:1]
    rows = []
    for k in range(TOP_K):
        hit = iota_e == eidx_ref[k:k + 1, :]
        rows.append(jnp.sum(jnp.where(hit, start_col, 0.0), axis=0, keepdims=True))
    pos_ref[...] = jnp.concatenate(rows, axis=0).astype(I32) + rank_ref[...]

    @pl.when(pl.program_id(0) == 0)
    def _():
        bstart_ref[0:n_exp, :] = (start / block_rows).astype(I32)
        bstart_ref[n_exp:, :] = jnp.broadcast_to(
            (end[n_exp - 1:n_exp, :] / block_rows).astype(I32), (SUBLANES, LANES))


def _slots(eidx, rank, cnt, block_rows):
    t_all = eidx.shape[1]
    n_exp = cnt.shape[0]
    tm = min(SLOT_TILE, t_all)
    return pl.pallas_call(
        functools.partial(_slots_body, block_rows=block_rows),
        grid=(t_all // tm,),
        in_specs=[pl.BlockSpec((TOP_K, tm), lambda i: (0, i)),
                  pl.BlockSpec((TOP_K, tm), lambda i: (0, i)),
                  pl.BlockSpec((n_exp, LANES), lambda i: (0, 0))],
        out_specs=[pl.BlockSpec((TOP_K, tm), lambda i: (0, i)),
                   pl.BlockSpec((n_exp + SUBLANES, LANES), lambda i: (0, 0))],
        out_shape=[jax.ShapeDtypeStruct((TOP_K, t_all), I32),
                   jax.ShapeDtypeStruct((n_exp + SUBLANES, LANES), I32)],
        compiler_params=_params("arbitrary"),
        name="slot_positions",
    )(eidx, rank, cnt)


def _expert_body(bstart_ref, xs_hbm, wg_ref, wu_ref, wd_ref, y_hbm,
                 xbuf, ybuf, wgu, wdn, sem_in, sem_out, *, block_rows):
    e = pl.program_id(0)
    n_exp = pl.num_programs(0)
    first, last, total = bstart_ref[e], bstart_ref[e + 1], bstart_ref[n_exp]
    f = wg_ref.shape[1]
    ahead = EXPERT_RING - EXPERT_GROUP

    def rows(i):
        return pl.ds(pl.multiple_of(i * block_rows, block_rows), block_rows)

    def slot_of(i):
        return i & (EXPERT_RING - 1)

    def in_copy(i):
        return pltpu.make_async_copy(xs_hbm.at[rows(i)], xbuf.at[slot_of(i)], sem_in.at[slot_of(i)])

    def out_copy(i):
        return pltpu.make_async_copy(ybuf.at[slot_of(i)], y_hbm.at[rows(i)], sem_out.at[slot_of(i)])

    def start_in(i):
        @pl.when(i < total)
        def _():
            in_copy(i).start()

    def wait_out(i):
        @pl.when(i >= 0)
        def _():
            out_copy(i).wait()

    @pl.when(e == 0)
    def _():
        for j in range(ahead):
            start_in(jnp.int32(j))

    wgu[:, :f] = wg_ref[...].astype(BF16)
    wgu[:, f:] = wu_ref[...].astype(BF16)
    wdn[...] = wd_ref[...].astype(BF16)

    def ffn(i, n):
        parts = []
        for j in range(n):
            lo, hi = _unpack_rows(xbuf[slot_of(i + j)])
            parts.append(jnp.concatenate([lo, hi], axis=1).astype(BF16))
        x = jnp.concatenate(parts, axis=0)
        gu = _dot(x, wgu[...])
        y = _pack_rows(_dot((_silu(gu[:, :f]) * gu[:, f:]).astype(BF16), wdn[...]))
        return [y[j * block_rows:(j + 1) * block_rows] for j in range(n)]

    def run_blocks(i, n):
        for j in range(n):
            in_copy(i + j).wait()
        for j in range(n):
            start_in(i + ahead + j)
        res = ffn(i, n)
        for j in range(n):
            wait_out(i + j - EXPERT_RING)
        for j in range(n):
            ybuf[slot_of(i + j)] = res[j]
        for j in range(n):
            out_copy(i + j).start()

    n_blk = last - first
    n_groups = n_blk // EXPERT_GROUP

    def group(p, carry):
        run_blocks(first + EXPERT_GROUP * p, EXPERT_GROUP)
        return carry

    lax.fori_loop(0, n_groups, group, 0)

    done = first + n_groups * EXPERT_GROUP
    size = EXPERT_GROUP // 2
    while size >= 1:
        @pl.when((n_blk & size) != 0)
        def _(done=done, size=size):
            run_blocks(done, size)
        done = done + (n_blk & size)
        size //= 2

    @pl.when(e == n_exp - 1)
    def _():
        for j in range(EXPERT_RING, 0, -1):
            wait_out(total - j)


def _experts(xs, bstart, w_gate, w_up, w_down, block_rows):
    n_slots, dw = xs.shape
    n_exp, d, f = w_gate.shape
    expert = lambda e, bs: (e, 0, 0)
    grid_spec = pltpu.PrefetchScalarGridSpec(
        num_scalar_prefetch=1,
        grid=(n_exp,),
        in_specs=[
            pl.BlockSpec(memory_space=pl.ANY),
            pl.BlockSpec((None, d, f), expert),
            pl.BlockSpec((None, d, f), expert),
            pl.BlockSpec((None, f, d), expert),
        ],
        out_specs=pl.BlockSpec(memory_space=pl.ANY),
        scratch_shapes=[
            pltpu.VMEM((EXPERT_RING, block_rows, dw), U32),
            pltpu.VMEM((EXPERT_RING, block_rows, dw), U32),
            pltpu.VMEM((d, 2 * f), BF16),
            pltpu.VMEM((f, d), BF16),
            pltpu.SemaphoreType.DMA((EXPERT_RING,)),
            pltpu.SemaphoreType.DMA((EXPERT_RING,)),
        ],
    )
    return pl.pallas_call(
        functools.partial(_expert_body, block_rows=block_rows),
        grid_spec=grid_spec,
        out_shape=jax.ShapeDtypeStruct((n_slots, dw), U32),
        compiler_params=_params("arbitrary"),
        name="expert_ffn",
    )(bstart, xs, w_gate, w_up, w_down)


SC_ROWS = 128


def _sc_workers():
    info = plsc.get_sparse_core_info()
    return info.num_cores, info.num_subcores


def _dispatch(h_rows, pos, n_slots):
    t_all, dw = h_rows.shape
    n_cores, n_sub = _sc_workers()
    per_worker = t_all // (n_cores * n_sub)
    assert per_worker * n_cores * n_sub == t_all and per_worker % SC_ROWS == 0
    mesh = plsc.VectorSubcoreMesh(core_axis_name="c", subcore_axis_name="s")

    @functools.partial(
        pl.kernel, mesh=mesh,
        out_type=jax.ShapeDtypeStruct((n_slots, dw), h_rows.dtype),
        scratch_types=[pltpu.VMEM((TOP_K, SC_ROWS), I32), pltpu.VMEM((SC_ROWS, dw), h_rows.dtype),
                       pltpu.SemaphoreType.DMA],
    )
    def body(h_hbm, pos_hbm, out_hbm, idx_v, rows_v, sem):
        base = (lax.axis_index("s") * n_cores + lax.axis_index("c")) * per_worker

        @pl.loop(0, per_worker // SC_ROWS)
        def _(j):
            t0 = pl.multiple_of(base + j * SC_ROWS, SC_ROWS)
            pltpu.sync_copy(pos_hbm.at[:, pl.ds(t0, SC_ROWS)], idx_v)
            pltpu.sync_copy(h_hbm.at[pl.ds(t0, SC_ROWS)], rows_v)
            copies = [pltpu.async_copy(rows_v, out_hbm.at[idx_v.at[k]], sem) for k in range(TOP_K)]
            for cp in copies:
                cp.wait()

    return body(h_rows, pos)


def _gather_back(y_rows, pos):
    top_k, t_all = pos.shape
    dw = y_rows.shape[1]
    n_cores, n_sub = _sc_workers()
    per_worker = t_all // (n_cores * n_sub)
    assert per_worker * n_cores * n_sub == t_all and per_worker % SC_ROWS == 0
    half = SC_ROWS // 2
    mesh = plsc.VectorSubcoreMesh(core_axis_name="c", subcore_axis_name="s")

    @functools.partial(
        pl.kernel, mesh=mesh,
        out_type=jax.ShapeDtypeStruct((top_k, t_all, dw), y_rows.dtype),
        scratch_types=[pltpu.VMEM((top_k, SC_ROWS), I32),
                       pltpu.VMEM((half, dw), y_rows.dtype), pltpu.VMEM((half, dw), y_rows.dtype),
                       pltpu.SemaphoreType.DMA, pltpu.SemaphoreType.DMA, pltpu.SemaphoreType.DMA],
    )
    def body(y_hbm, pos_hbm, out_hbm, idx_v, buf_a, buf_b, sem_g, sem_a, sem_b):
        base = (lax.axis_index("s") * n_cores + lax.axis_index("c")) * per_worker
        bufs, sems = (buf_a, buf_b), (sem_a, sem_b)

        @pl.loop(0, per_worker // SC_ROWS)
        def _(j):
            t0 = pl.multiple_of(base + j * SC_ROWS, SC_ROWS)
            pltpu.sync_copy(pos_hbm.at[:, pl.ds(t0, SC_ROWS)], idx_v)
            pending = [None, None]
            for step in range(2 * top_k):
                k, h = step // 2, step % 2
                slot = step % 2
                if pending[slot] is not None:
                    pending[slot].wait()
                pltpu.async_copy(y_hbm.at[idx_v.at[k, pl.ds(h * half, half)]], bufs[slot], sem_g).wait()
                pending[slot] = pltpu.async_copy(
                    bufs[slot], out_hbm.at[k, pl.ds(t0 + h * half, half)], sems[slot])
            for p in pending:
                p.wait()

    return body(y_rows, pos)


def _combine_body(yg_ref, w_ref, base_ref, mod_ref, fg_ref, o_ref):
    w_cols = jnp.transpose(w_ref[...])
    acc_lo = acc_hi = None
    for k in range(TOP_K):
        lo, hi = _unpack_rows(yg_ref[k])
        wk = w_cols[:, k:k + 1]
        acc_lo = lo * wk if acc_lo is None else acc_lo + lo * wk
        acc_hi = hi * wk if acc_hi is None else acc_hi + hi * wk
    acc = jnp.concatenate([acc_lo, acc_hi], axis=1)
    x2 = base_ref[...] + mod_ref[5:6, :] * acc
    ms = jnp.mean(x2 * x2, axis=-1, keepdims=True)
    o_ref[...] = (x2 * lax.rsqrt(ms + NORM_EPS)) * fg_ref[...]


def _combine(yg, wts, base, mod3, final_g):
    b, s, d = base.shape
    tm = min(COMBINE_TILE, s)
    n_t = s // tm
    return pl.pallas_call(
        _combine_body,
        grid=(b, n_t),
        in_specs=[pl.BlockSpec((TOP_K, tm, d // 2), lambda bi, i: (0, bi * n_t + i, 0)),
                  pl.BlockSpec((TOP_K, tm), lambda bi, i: (0, bi * n_t + i)),
                  pl.BlockSpec((None, tm, d), lambda bi, i: (bi, i, 0)),
                  pl.BlockSpec((None, N_ADA, d), lambda bi, i: (bi, 0, 0)),
                  pl.BlockSpec((1, d), lambda bi, i: (0, 0))],
        out_specs=pl.BlockSpec((None, tm, d), lambda bi, i: (bi, i, 0)),
        out_shape=jax.ShapeDtypeStruct((b, s, d), F32),
        compiler_params=_params("parallel", "parallel"),
        name="combine_norm",
    )(yg, wts, base, mod3, final_g)


def _rope_tables(seq_len):
    rows = seq_len // GRID_W
    row = jnp.repeat(jnp.arange(rows), GRID_W).astype(F32)
    col = jnp.tile(jnp.arange(GRID_W), rows).astype(F32)
    n_freq = HEAD_DIM // 4
    inv = ROPE_THETA ** (-jnp.arange(n_freq, dtype=F32) / n_freq)
    ar, ac = row[:, None] * inv, col[:, None] * inv
    zeros = jnp.zeros_like(ar)
    reps = LANES // HEAD_DIM
    cos = jnp.tile(jnp.concatenate([jnp.cos(ar), jnp.cos(ar), jnp.cos(ac), jnp.cos(ac)], 1), (1, reps))
    sin_a = jnp.tile(jnp.concatenate([-jnp.sin(ar), zeros, -jnp.sin(ac), zeros], 1), (1, reps))
    sin_b = jnp.tile(jnp.concatenate([zeros, jnp.sin(ar), zeros, jnp.sin(ac)], 1), (1, reps))
    return cos, sin_a, sin_b


def kernel(x, c, ctx, c_ctx, w_ada, b_ada, norm1_g, w_in, attn_sink, w_pool, pool_scale,
           w_up_attn, w_up_pool, w_out, norm2_g, w_router, router_bias,
           w_exp_gate, w_exp_up, w_exp_down, w_sh_gate, w_sh_up, w_sh_down, final_g):
    b, s, d = x.shape
    assert w_ada.shape[0] == 1, "single-layer block"
    assert s % ATTN_BLOCK == 0 and s % GRID_W == 0 and d % LANES == 0
    pool_dim = w_up_pool.shape[1]
    n_exp = w_router.shape[-1]
    t_all = b * s

    pad_rows = -(-(b + 1) // SUBLANES) * SUBLANES
    c_rows = jnp.concatenate([c, c_ctx[None, :], jnp.zeros((pad_rows - b - 1, d), F32)], axis=0)
    mod3 = _ada(c_rows, w_ada[0], b_ada[0]).reshape(pad_rows, N_ADA, d)

    w_in_bf = w_in[0].astype(BF16)
    g1 = norm1_g[0].reshape(1, d)
    cos, sin_a, sin_b = _rope_tables(s)
    q, k, v, u, gates = _inproj(x, mod3, g1, w_in_bf, cos, sin_a, sin_b, pool_dim)
    kc, vc = _ctxkv(ctx, mod3[b], g1, w_in_bf[:, ATTN_DIM:ATTN_DIM + 2 * KV_DIM])
    attn = _attention(q, k, v, kc, vc, attn_sink[0])

    w_r_t = w_router[0].T
    w_r_hi = w_r_t.astype(BF16)
    w_r_lo = (w_r_t - w_r_hi.astype(F32)).astype(BF16)
    base, h2, eidx, wts, rank, cnt = _mixer(
        x, attn, u, gates, mod3, norm2_g[0].reshape(1, d), w_pool[0].astype(BF16),
        pool_scale[0].reshape(1, pool_dim), w_up_attn[0].astype(BF16), w_up_pool[0].astype(BF16),
        w_out[0].astype(BF16), w_r_hi, w_r_lo, router_bias[0].reshape(n_exp, 1),
        w_sh_gate[0].astype(BF16), w_sh_up[0].astype(BF16), w_sh_down[0].astype(BF16))

    br = EXPERT_BLOCK_ROWS
    n_blocks = (t_all * TOP_K + n_exp * (br - 1) + br - 1) // br
    n_slots = n_blocks * br
    pos, bstart = _slots(eidx, rank, cnt, br)
    xs = _dispatch(h2, pos, n_slots)
    y = _experts(xs, bstart[:n_exp + 1, 0], w_exp_gate[0], w_exp_up[0], w_exp_down[0], br)
    yg = _gather_back(y, pos)

    return _combine(yg, wts, base, mod3, final_g.reshape(1, d))
```

```python
import functools

import jax
import jax.numpy as jnp
from jax import lax
from jax.experimental import pallas as pl
from jax.experimental.pallas import tpu as pltpu
from jax.experimental.pallas import tpu_sc as plsc

F32 = jnp.float32
BF16 = jnp.bfloat16
I32 = jnp.int32
U32 = jnp.uint32

GRID_W = 64
HEAD_DIM = 64
N_Q_HEADS = 8
N_KV_HEADS = 2
Q_PER_KV = N_Q_HEADS // N_KV_HEADS
ATTN_DIM = N_Q_HEADS * HEAD_DIM
KV_DIM = N_KV_HEADS * HEAD_DIM
ATTN_BLOCK = 128
ATTN_SCALE = HEAD_DIM ** -0.5
LOG2_E = 1.4426950408889634
ROPE_THETA = 10000.0
POOL_WINDOWS = (2, 4, 8, 16)
POOL_HALO = 8
N_EXPERT_GROUPS = 8
TOPK_GROUPS = 4
TOP_K = 8
ROUTED_SCALE = 2.5
N_ADA = 6
NORM_EPS = 1e-6
NEG_INF = -1e30
LANES = 128
SUBLANES = 8
ATTN_QBLOCKS = 4
TOKEN_TILE = 512
SLOT_TILE = 2048
COMBINE_TILE = 512
PROJ_TILE = 1024
EXPERT_BLOCK_ROWS = 256
EXPERT_GROUP = 4
EXPERT_RING = 16
VMEM_LIMIT = 56 * 1024 * 1024


def _sigmoid(x):
    return 1.0 / (1.0 + jnp.exp(-x))


def _silu(x):
    return x * _sigmoid(x)


def _nt_dot(a, b):
    return lax.dot_general(a, b, (((1,), (1,)), ((), ())), preferred_element_type=F32)


def _dot(a, b):
    return jnp.dot(a, b, preferred_element_type=F32)


def _pack_rows(x):
    n = x.shape[1] // 2
    bits = lax.bitcast_convert_type(x.astype(BF16).astype(F32), U32)
    return (bits[:, :n] >> 16) | (bits[:, n:] & jnp.uint32(0xFFFF0000))


def _unpack_rows(w):
    lo = lax.bitcast_convert_type(w << 16, F32)
    hi = lax.bitcast_convert_type(w & jnp.uint32(0xFFFF0000), F32)
    return lo, hi


def _params(*sem):
    return pltpu.CompilerParams(dimension_semantics=sem, vmem_limit_bytes=VMEM_LIMIT)


def _ada_body(c_ref, w_ref, b_ref, o_ref):
    s = _silu(c_ref[...])
    o_ref[...] = jnp.dot(s, w_ref[...], preferred_element_type=F32,
                         precision=lax.Precision.HIGHEST) + b_ref[...]


def _ada(c_rows, w_ada, b_ada):
    rows, d = c_rows.shape
    n = w_ada.shape[1]
    bn = d
    return pl.pallas_call(
        _ada_body,
        grid=(n // bn,),
        in_specs=[pl.BlockSpec((rows, d), lambda j: (0, 0)),
                  pl.BlockSpec((d, bn), lambda j: (0, j)),
                  pl.BlockSpec((1, bn), lambda j: (0, j))],
        out_specs=pl.BlockSpec((rows, bn), lambda j: (0, j)),
        out_shape=jax.ShapeDtypeStruct((rows, n), F32),
        compiler_params=_params("arbitrary"),
        name="ada_mod",
    )(c_rows, w_ada, b_ada.reshape(1, n))


def _norm_mod(x, g, shift, scale):
    ms = jnp.mean(x * x, axis=-1, keepdims=True)
    y = x * lax.rsqrt(ms + NORM_EPS)
    return y * (g * (1.0 + scale)) + shift


def _rope(t, cos, sin_a, sin_b):
    return (t * cos + pltpu.roll(t, LANES - HEAD_DIM // 4, 1) * sin_a
            + pltpu.roll(t, HEAD_DIM // 4, 1) * sin_b)


def _inproj_body(x_ref, mod_ref, g_ref, w_ref, cos_ref, sa_ref, sb_ref,
                 q_ref, k_ref, v_ref, u_ref, gate_ref):
    h = _norm_mod(x_ref[...], g_ref[...], mod_ref[0:1, :], mod_ref[1:2, :]).astype(BF16)
    cos, sa, sb = cos_ref[...], sa_ref[...], sb_ref[...]
    heads_per_chunk = LANES // HEAD_DIM
    wide = 2 * LANES
    for j in range(ATTN_DIM // wide):
        t2 = _dot(h, w_ref[:, j * wide:(j + 1) * wide])
        for c in range(wide // LANES):
            t = (_rope(t2[:, c * LANES:(c + 1) * LANES], cos, sa, sb) * (ATTN_SCALE * LOG2_E)).astype(BF16)
            for i in range(heads_per_chunk):
                head = (j * (wide // LANES) + c) * heads_per_chunk + i
                q_ref[head] = t[:, i * HEAD_DIM:(i + 1) * HEAD_DIM]
    k_off = ATTN_DIM
    v_off = k_off + KV_DIM
    kv = _dot(h, w_ref[:, k_off:v_off + KV_DIM])
    t = jnp.transpose(_rope(kv[:, :KV_DIM], cos, sa, sb)).astype(BF16)
    for i in range(N_KV_HEADS):
        k_ref[i] = t[i * HEAD_DIM:(i + 1) * HEAD_DIM, :]
    t = kv[:, KV_DIM:].astype(BF16)
    for i in range(N_KV_HEADS):
        v_ref[i] = t[:, i * HEAD_DIM:(i + 1) * HEAD_DIM]
    p_off = v_off + KV_DIM
    pool_dim = u_ref.shape[-1]
    u_ref[...] = _dot(h, w_ref[:, p_off:p_off + pool_dim])
    g_off = p_off + pool_dim
    gate_dim = gate_ref.shape[-1]
    chunk = 512
    for j in range(gate_dim // chunk):
        t = _dot(h, w_ref[:, g_off + j * chunk:g_off + (j + 1) * chunk])
        gate_ref[:, j * chunk:(j + 1) * chunk] = _sigmoid(t).astype(BF16)


def _inproj(x, mod3, g, w_in_bf, cos, sa, sb, pool_dim):
    b, s, d = x.shape
    tm = min(PROJ_TILE, s)
    in_dim = w_in_bf.shape[1]
    gate_dim = in_dim - ATTN_DIM - 2 * KV_DIM - pool_dim
    grid = (b, s // tm)
    return pl.pallas_call(
        _inproj_body,
        grid=grid,
        in_specs=[
            pl.BlockSpec((None, tm, d), lambda bi, i: (bi, i, 0)),
            pl.BlockSpec((None, N_ADA, d), lambda bi, i: (bi, 0, 0)),
            pl.BlockSpec((1, d), lambda bi, i: (0, 0)),
            pl.BlockSpec((d, in_dim), lambda bi, i: (0, 0)),
            pl.BlockSpec((tm, LANES), lambda bi, i: (i, 0)),
            pl.BlockSpec((tm, LANES), lambda bi, i: (i, 0)),
            pl.BlockSpec((tm, LANES), lambda bi, i: (i, 0)),
        ],
        out_specs=[
            pl.BlockSpec((None, N_Q_HEADS, tm, HEAD_DIM), lambda bi, i: (bi, 0, i, 0)),
            pl.BlockSpec((None, N_KV_HEADS, HEAD_DIM, tm), lambda bi, i: (bi, 0, 0, i)),
            pl.BlockSpec((None, N_KV_HEADS, tm, HEAD_DIM), lambda bi, i: (bi, 0, i, 0)),
            pl.BlockSpec((None, tm, pool_dim), lambda bi, i: (bi, i, 0)),
            pl.BlockSpec((None, tm, gate_dim), lambda bi, i: (bi, i, 0)),
        ],
        out_shape=[
            jax.ShapeDtypeStruct((b, N_Q_HEADS, s, HEAD_DIM), BF16),
            jax.ShapeDtypeStruct((b, N_KV_HEADS, HEAD_DIM, s), BF16),
            jax.ShapeDtypeStruct((b, N_KV_HEADS, s, HEAD_DIM), BF16),
            jax.ShapeDtypeStruct((b, s, pool_dim), F32),
            jax.ShapeDtypeStruct((b, s, gate_dim), BF16),
        ],
        compiler_params=_params("parallel", "parallel"),
        name="in_proj",
    )(x, mod3, g, w_in_bf, cos, sa, sb)


def _ctxkv_body(ctx_ref, mod_ref, g_ref, w_ref, kc_ref, vc_ref):
    h = _norm_mod(ctx_ref[...], g_ref[...], mod_ref[0:1, :], mod_ref[1:2, :]).astype(BF16)
    t = _dot(h, w_ref[...])
    kt = jnp.transpose(t[:, :KV_DIM]).astype(BF16)
    vt = t[:, KV_DIM:].astype(BF16)
    for i in range(N_KV_HEADS):
        kc_ref[i] = kt[i * HEAD_DIM:(i + 1) * HEAD_DIM, :]
        vc_ref[i] = vt[:, i * HEAD_DIM:(i + 1) * HEAD_DIM]


def _ctxkv(ctx, mod_c, g, w_kv_bf):
    b, c, d = ctx.shape
    v_out = jax.ShapeDtypeStruct((b, N_KV_HEADS, c, HEAD_DIM), BF16)
    k_out = jax.ShapeDtypeStruct((b, N_KV_HEADS, HEAD_DIM, c), BF16)
    v_spec = pl.BlockSpec((None, N_KV_HEADS, c, HEAD_DIM), lambda bi: (bi, 0, 0, 0))
    k_spec = pl.BlockSpec((None, N_KV_HEADS, HEAD_DIM, c), lambda bi: (bi, 0, 0, 0))
    return pl.pallas_call(
        _ctxkv_body,
        grid=(b,),
        in_specs=[pl.BlockSpec((None, c, d), lambda bi: (bi, 0, 0)),
                  pl.BlockSpec((N_ADA, d), lambda bi: (0, 0)),
                  pl.BlockSpec((1, d), lambda bi: (0, 0)),
                  pl.BlockSpec((d, 2 * KV_DIM), lambda bi: (0, 0))],
        out_specs=[k_spec, v_spec],
        out_shape=[k_out, v_out],
        compiler_params=_params("parallel"),
        name="ctx_kv",
    )(ctx, mod_c, g, w_kv_bf)


def _attn_body(sink_ref, q_ref, *refs):
    nkb = ATTN_QBLOCKS + 2
    k_refs, v_refs = refs[:nkb], refs[nkb:2 * nkb]
    kc_ref, vc_ref, lo_ref, hi_ref, o_ref = refs[2 * nkb:]
    n = pl.program_id(1)
    last = pl.num_programs(1) - 1
    rows = Q_PER_KV * ATTN_BLOCK
    row_head = lax.broadcasted_iota(I32, (rows, 1), 0) // ATTN_BLOCK
    for qb in range(ATTN_QBLOCKS):
        bias_lo, bias_hi = lo_ref[...], hi_ref[...]
        if qb == 0:
            bias_lo = bias_lo + jnp.where(n == 0, NEG_INF, 0.0)
        if qb == ATTN_QBLOCKS - 1:
            bias_hi = bias_hi + jnp.where(n == last, NEG_INF, 0.0)
        q_rows = slice(qb * ATTN_BLOCK, (qb + 1) * ATTN_BLOCK)
        outs = []
        for kk in range(N_KV_HEADS):
            qs = q_ref[kk * Q_PER_KV:(kk + 1) * Q_PER_KV, q_rows, :].reshape(rows, HEAD_DIM)
            k_loc = jnp.concatenate([k_refs[qb + j][kk] for j in range(3)], axis=1)
            v_loc = jnp.concatenate([v_refs[qb + j][kk] for j in range(3)], axis=0)
            s_loc = _dot(qs, k_loc)
            s0 = s_loc[:, :ATTN_BLOCK] + bias_lo
            s1 = s_loc[:, ATTN_BLOCK:2 * ATTN_BLOCK]
            s2 = s_loc[:, 2 * ATTN_BLOCK:] + bias_hi
            sc = _dot(qs, kc_ref[kk])
            sink = jnp.zeros((rows, 1), F32)
            for g in range(Q_PER_KV):
                sink = jnp.where(row_head == g, sink_ref[kk * Q_PER_KV + g] * LOG2_E, sink)
            n_ctx = sc.shape[1] // ATTN_BLOCK
            folded = jnp.maximum(jnp.maximum(s0, s1), s2)
            for j in range(n_ctx):
                folded = jnp.maximum(folded, sc[:, j * ATTN_BLOCK:(j + 1) * ATTN_BLOCK])
            m = jnp.maximum(jnp.max(folded, axis=1, keepdims=True), sink)
            p0, p1, p2, pc = jnp.exp2(s0 - m), jnp.exp2(s1 - m), jnp.exp2(s2 - m), jnp.exp2(sc - m)
            folded = p0 + p1 + p2
            for j in range(n_ctx):
                folded = folded + pc[:, j * ATTN_BLOCK:(j + 1) * ATTN_BLOCK]
            denom = jnp.sum(folded, axis=1, keepdims=True) + jnp.exp2(sink - m)
            p_loc = jnp.concatenate([p0, p1, p2], axis=1).astype(BF16)
            o = _dot(p_loc, v_loc) + _dot(pc.astype(BF16), vc_ref[kk])
            o = o / denom
            outs += [o[g * ATTN_BLOCK:(g + 1) * ATTN_BLOCK] for g in range(Q_PER_KV)]
        o_ref[q_rows, :] = jnp.concatenate(outs, axis=1).astype(BF16)


def _attention(q, k, v, kc, vc, sink):
    b, _, s, _ = q.shape
    c = vc.shape[2]
    nb = s // ATTN_BLOCK
    assert nb % ATTN_QBLOCKS == 0 and c % ATTN_BLOCK == 0
    rows = Q_PER_KV * ATTN_BLOCK
    q_rows = ATTN_QBLOCKS * ATTN_BLOCK
    qi = jnp.tile(jnp.arange(ATTN_BLOCK), Q_PER_KV)[:, None]
    kj = jnp.arange(ATTN_BLOCK)[None, :]
    bias_lo = jnp.where(kj >= qi, 0.0, NEG_INF).astype(F32)
    bias_hi = jnp.where(kj <= qi, 0.0, NEG_INF).astype(F32)

    def blk(n, i):
        return jnp.clip(ATTN_QBLOCKS * n - 1 + i, 0, nb - 1)

    k_specs = [pl.BlockSpec((None, N_KV_HEADS, HEAD_DIM, ATTN_BLOCK), lambda bi, n, i=i: (bi, 0, 0, blk(n, i)))
               for i in range(ATTN_QBLOCKS + 2)]
    v_specs = [pl.BlockSpec((None, N_KV_HEADS, ATTN_BLOCK, HEAD_DIM), lambda bi, n, i=i: (bi, 0, blk(n, i), 0))
               for i in range(ATTN_QBLOCKS + 2)]
    kc_spec = pl.BlockSpec((None, N_KV_HEADS, HEAD_DIM, c), lambda bi, n: (bi, 0, 0, 0))
    vc_spec = pl.BlockSpec((None, N_KV_HEADS, c, HEAD_DIM), lambda bi, n: (bi, 0, 0, 0))
    bias_spec = pl.BlockSpec((rows, ATTN_BLOCK), lambda bi, n: (0, 0))
    return pl.pallas_call(
        _attn_body,
        grid=(b, nb // ATTN_QBLOCKS),
        in_specs=[
            pl.BlockSpec(memory_space=pltpu.SMEM),
            pl.BlockSpec((None, N_Q_HEADS, q_rows, HEAD_DIM), lambda bi, n: (bi, 0, n, 0)),
            *k_specs, *v_specs, kc_spec, vc_spec, bias_spec, bias_spec,
        ],
        out_specs=pl.BlockSpec((None, q_rows, ATTN_DIM), lambda bi, n: (bi, n, 0)),
        out_shape=jax.ShapeDtypeStruct((b, s, ATTN_DIM), BF16),
        compiler_params=_params("parallel", "parallel"),
        name="window_attn",
    )(sink, q, *([k] * (ATTN_QBLOCKS + 2)), *([v] * (ATTN_QBLOCKS + 2)), kc, vc, bias_lo, bias_hi)


def _pool_delta(u_ref, up, un, inv_cnt, bufs):
    tm = u_ref.shape[0]
    n_ext = tm + 2 * POOL_HALO
    pad = POOL_HALO
    group_dim = u_ref.shape[1] // len(POOL_WINDOWS)
    outs = []
    for g, w in enumerate(POOL_WINDOWS):
        lanes = slice(g * group_dim, (g + 1) * group_dim)
        level = bufs[g]
        level[0][pad:pad + POOL_HALO, :] = up[:, lanes]
        level[0][pad + POOL_HALO:pad + POOL_HALO + tm, :] = u_ref[:, lanes]
        level[0][pad + POOL_HALO + tm:pad + n_ext, :] = un[:, lanes]
        acc = level[0][pl.ds(pad - 1, n_ext), :] + level[0][pl.ds(pad, n_ext), :]
        step = 1
        for j in range(1, len(level)):
            level[j][pad:pad + n_ext, :] = acc
            acc = level[j][pl.ds(pad - step, n_ext), :] + level[j][pl.ds(pad + step, n_ext), :]
            step *= 2
        outs.append(acc[POOL_HALO:POOL_HALO + tm] * inv_cnt[:, g:g + 1] - u_ref[:, lanes])
    return outs


def _route_steps(s, sel, iota_ref, out):
    n_exp = s.shape[0]
    per_group = n_exp // N_EXPERT_GROUPS
    neg = float("-inf")
    iota_g = iota_ref[0:per_group, :]
    scores = []
    for g in range(N_EXPERT_GROUPS):
        blk = sel[g * per_group:(g + 1) * per_group]
        m1 = jnp.max(blk, axis=0, keepdims=True)
        first = jnp.min(jnp.where(blk == m1, iota_g, float(per_group)), axis=0, keepdims=True)
        m2 = jnp.max(jnp.where(iota_g == first, neg, blk), axis=0, keepdims=True)
        scores.append(m1 + m2)
        if g % 2 == 1:
            yield
    gs = jnp.concatenate(scores, axis=0)
    iota_ng = iota_ref[0:N_EXPERT_GROUPS, :]
    gsel = jnp.zeros(iota_ng.shape, F32)
    for _ in range(TOPK_GROUPS):
        m = jnp.max(gs, axis=0, keepdims=True)
        first = jnp.min(jnp.where(gs == m, iota_ng, float(N_EXPERT_GROUPS)), axis=0, keepdims=True)
        hit = iota_ng == first
        gsel = jnp.where(hit, 1.0, gsel)
        gs = jnp.where(hit, neg, gs)
    cur = jnp.concatenate(
        [jnp.where(gsel[g:g + 1] > 0.0, sel[g * per_group:(g + 1) * per_group], NEG_INF)
         for g in range(N_EXPERT_GROUPS)], axis=0)
    yield
    chosen = jnp.zeros(s.shape, F32)
    ids, aff = [], []
    for _ in range(TOP_K):
        m = jnp.max(cur, axis=0, keepdims=True)
        first = jnp.min(jnp.where(cur == m, iota_ref[...], float(n_exp)), axis=0, keepdims=True)
        hit = iota_ref[...] == first
        ids.append(first)
        aff.append(jnp.sum(jnp.where(hit, s, 0.0), axis=0, keepdims=True))
        chosen = jnp.where(hit, 1.0, chosen)
        cur = jnp.where(hit, neg, cur)
        yield
    out.update(ids=ids, aff=aff, chosen=chosen)


def _mixer_body(x_ref, attn_ref, up_ref, u_ref, un_ref, gate_ref, mod_ref, icnt_ref, iota_ref, n2g_ref,
                wpool_ref, pscale_ref, wua_ref, wup_ref, wout_ref, wrh_ref, wrl_ref, rbias_ref,
                wsg_ref, wsu_ref, wsd_ref,
                base_ref, h2_ref, eidx_ref, wts_ref, rank_ref, cnt_ref, score_buf, *pool_bufs, n_t):
    g = pl.program_id(0)
    n_tiles = pl.num_programs(0) - 1
    ti = lax.rem(jnp.minimum(g, n_tiles - 1), n_t)
    slot = g & 1
    tm, d = x_ref.shape
    pool_dim = u_ref.shape[-1]
    n_exp = score_buf.shape[1]

    @pl.when(g == 0)
    def _():
        cnt_ref[...] = jnp.zeros_like(cnt_ref)
        score_buf[1] = jnp.zeros((n_exp, tm), F32)
        for buf in pool_bufs:
            buf[0:POOL_HALO, :] = jnp.zeros((POOL_HALO, buf.shape[1]), F32)
            buf[buf.shape[0] - POOL_HALO:, :] = jnp.zeros((POOL_HALO, buf.shape[1]), F32)
    levels = [w.bit_length() - 1 for w in POOL_WINDOWS]
    group_bufs = [pool_bufs[sum(levels[:i]):sum(levels[:i + 1])] for i in range(len(levels))]

    def routing():
        s = score_buf[1 - slot]
        sel = s + rbias_ref[...]
        res = {}
        yield from _route_steps(s, sel, iota_ref, res)
        ids, aff, chosen = res["ids"], res["aff"], res["chosen"]
        total = aff[0]
        for a in aff[1:]:
            total = total + a
        wts_ref[...] = jnp.concatenate([a / total * ROUTED_SCALE for a in aff], axis=0)
        eidx_ref[...] = jnp.concatenate(ids, axis=0).astype(I32)
        yield
        before = (lax.broadcasted_iota(I32, (tm, tm), 0) < lax.broadcasted_iota(I32, (tm, tm), 1))
        prefix = _dot(chosen.astype(BF16), jnp.where(before, 1.0, 0.0).astype(BF16))
        rank_dense = prefix + cnt_ref[:, 0:1]
        rank_ref[...] = jnp.concatenate(
            [jnp.sum(jnp.where(iota_ref[...] == i, rank_dense, 0.0), axis=0, keepdims=True) for i in ids],
            axis=0).astype(I32)
        real = jnp.where(g > 0, 1.0, 0.0)
        cnt_ref[...] = cnt_ref[...] + real * jnp.sum(chosen, axis=1, keepdims=True)

    def chain():
        zero_halo = jnp.zeros((POOL_HALO, pool_dim), F32)
        u_prev = jnp.where(ti == 0, zero_halo, up_ref[...])
        u_next = jnp.where(ti == n_t - 1, zero_halo, un_ref[...])
        deltas = _pool_delta(u_ref, u_prev, u_next, icnt_ref[...], group_bufs)
        attn_up = _dot(attn_ref[...], wua_ref[...])
        yield
        pool = jnp.concatenate(
            [_dot(dl.astype(BF16), wpool_ref[i]) for i, dl in enumerate(deltas)], axis=1)
        pool = (pool * pscale_ref[...]).astype(BF16)
        yield
        pool_up = _dot(pool, wup_ref[...])
        yield
        y = gate_ref[:, :d] * attn_up.astype(BF16) + gate_ref[:, d:] * pool_up.astype(BF16)
        mix = _dot(y, wout_ref[...])
        yield
        x1 = x_ref[...] + mod_ref[2:3, :] * mix
        h2 = _norm_mod(x1, n2g_ref[...], mod_ref[3:4, :], mod_ref[4:5, :])
        h2_hi = h2.astype(BF16)
        h2_ref[...] = _pack_rows(h2)
        gate = _dot(h2_hi, wsg_ref[...])
        yield
        up = _dot(h2_hi, wsu_ref[...])
        yield
        shared = _dot((_silu(gate) * up).astype(BF16), wsd_ref[...])
        yield
        base_ref[...] = x1 + mod_ref[5:6, :] * shared
        h2_lo = (h2 - h2_hi.astype(F32)).astype(BF16)
        logits = _nt_dot(wrh_ref[...], h2_hi)
        yield
        logits = logits + _nt_dot(wrh_ref[...], h2_lo)
        yield
        logits = logits + _nt_dot(wrl_ref[...], h2_hi)
        score_buf[slot] = _sigmoid(logits)

    stages = [chain(), routing()]
    while stages:
        for gen in list(stages):
            try:
                next(gen)
            except StopIteration:
                stages.remove(gen)


def _mixer(x, attn, u, gates, mod3, n2g, wpool_bf, pscale, wua_bf, wup_bf, wout_bf,
           wr_hi, wr_lo, rbias, wsg_bf, wsu_bf, wsd_bf):
    b, s, d = x.shape
    tm = min(TOKEN_TILE, s)
    n_t = s // tm
    n_tiles = b * n_t
    t_all = b * s
    pool_dim = u.shape[-1]
    n_exp = wr_hi.shape[0]
    sh = wsg_bf.shape[1]
    halo_blocks = tm // POOL_HALO
    n_halo = s // POOL_HALO
    full = lambda shape: pl.BlockSpec(shape, lambda g: (0,) * len(shape))

    def tile(g):
        tg = jnp.minimum(g, n_tiles - 1)
        return tg // n_t, lax.rem(tg, n_t)

    def seq_block(g):
        bi, i = tile(g)
        return bi, i, 0

    def halo_prev(g):
        bi, i = tile(g)
        return bi, jnp.maximum(i * halo_blocks - 1, 0), 0

    def halo_next(g):
        bi, i = tile(g)
        return bi, jnp.minimum((i + 1) * halo_blocks, n_halo - 1), 0

    routed = lambda g: (0, jnp.maximum(g - 1, 0))
    t = jnp.arange(s)
    inv_cnt = jnp.stack(
        [1.0 / (jnp.minimum(t + w // 2, s) - jnp.maximum(t - w // 2, 0)).astype(F32) for w in POOL_WINDOWS], axis=1)
    expert_iota = jnp.broadcast_to(jnp.arange(n_exp, dtype=F32)[:, None], (n_exp, tm))
    return pl.pallas_call(
        functools.partial(_mixer_body, n_t=n_t),
        grid=(n_tiles + 1,),
        in_specs=[
            pl.BlockSpec((None, tm, d), seq_block),
            pl.BlockSpec((None, tm, ATTN_DIM), seq_block),
            pl.BlockSpec((None, POOL_HALO, pool_dim), halo_prev),
            pl.BlockSpec((None, tm, pool_dim), seq_block),
            pl.BlockSpec((None, POOL_HALO, pool_dim), halo_next),
            pl.BlockSpec((None, tm, 2 * d), seq_block),
            pl.BlockSpec((None, N_ADA, d), lambda g: (tile(g)[0], 0, 0)),
            pl.BlockSpec((tm, len(POOL_WINDOWS)), lambda g: (tile(g)[1], 0)),
            full((n_exp, tm)),
            full((1, d)),
            full((len(POOL_WINDOWS), pool_dim // len(POOL_WINDOWS), pool_dim // len(POOL_WINDOWS))),
            full((1, pool_dim)),
            full((ATTN_DIM, d)), full((pool_dim, d)), full((d, d)),
            full((n_exp, d)), full((n_exp, d)), full((n_exp, 1)),
            full((d, sh)), full((d, sh)), full((sh, d)),
        ],
        out_specs=[
            pl.BlockSpec((None, tm, d), seq_block),
            pl.BlockSpec((tm, d // 2), lambda g: (jnp.minimum(g, n_tiles - 1), 0)),
            pl.BlockSpec((TOP_K, tm), routed),
            pl.BlockSpec((TOP_K, tm), routed),
            pl.BlockSpec((TOP_K, tm), routed),
            pl.BlockSpec((n_exp, LANES), lambda g: (0, 0)),
        ],
        out_shape=[
            jax.ShapeDtypeStruct((b, s, d), F32),
            jax.ShapeDtypeStruct((t_all, d // 2), U32),
            jax.ShapeDtypeStruct((TOP_K, t_all), I32),
            jax.ShapeDtypeStruct((TOP_K, t_all), F32),
            jax.ShapeDtypeStruct((TOP_K, t_all), I32),
            jax.ShapeDtypeStruct((n_exp, LANES), F32),
        ],
        scratch_shapes=[pltpu.VMEM((2, n_exp, tm), F32)] + [
            pltpu.VMEM((tm + 4 * POOL_HALO, pool_dim // len(POOL_WINDOWS)), F32)
            for w in POOL_WINDOWS for _ in range(w.bit_length() - 1)],
        compiler_params=_params("arbitrary"),
        name="mixer_router",
    )(x, attn, u, u, u, gates, mod3, inv_cnt, expert_iota, n2g, wpool_bf, pscale, wua_bf, wup_bf, wout_bf,
      wr_hi, wr_lo, rbias, wsg_bf, wsu_bf, wsd_bf)


def _slots_body(eidx_ref, rank_ref, cnt_ref, pos_ref, bstart_ref, *, block_rows):
    n_exp = cnt_ref.shape[0]
    tm = eidx_ref.shape[1]
    cnt = cnt_ref[...]
    padded = jnp.floor((cnt + (block_rows - 1)) / block_rows) * block_rows
    hi = jnp.floor(padded / 256.0)
    lo = padded - hi * 256.0
    below = (lax.broadcasted_iota(I32, (n_exp, n_exp), 1) < lax.broadcasted_iota(I32, (n_exp, n_exp), 0))
    tri = jnp.where(below, 1.0, 0.0).astype(BF16)
    start = 256.0 * _dot(tri, hi.astype(BF16)) + _dot(tri, lo.astype(BF16))
    end = start + padded
    iota_e = lax.broadcasted_iota(I32, (n_exp, tm), 0)
    start_col = start[:, 0:1]
    rows = []
    for k in range(TOP_K):
        hit = iota_e == eidx_ref[k:k + 1, :]
        rows.append(jnp.sum(jnp.where(hit, start_col, 0.0), axis=0, keepdims=True))
    pos_ref[...] = jnp.concatenate(rows, axis=0).astype(I32) + rank_ref[...]

    @pl.when(pl.program_id(0) == 0)
    def _():
        bstart_ref[0:n_exp, :] = (start / block_rows).astype(I32)
        bstart_ref[n_exp:, :] = jnp.broadcast_to(
            (end[n_exp - 1:n_exp, :] / block_rows).astype(I32), (SUBLANES, LANES))


def _slots(eidx, rank, cnt, block_rows):
    t_all = eidx.shape[1]
    n_exp = cnt.shape[0]
    tm = min(SLOT_TILE, t_all)
    return pl.pallas_call(
        functools.partial(_slots_body, block_rows=block_rows),
        grid=(t_all // tm,),
        in_specs=[pl.BlockSpec((TOP_K, tm), lambda i: (0, i)),
                  pl.BlockSpec((TOP_K, tm), lambda i: (0, i)),
                  pl.BlockSpec((n_exp, LANES), lambda i: (0, 0))],
        out_specs=[pl.BlockSpec((TOP_K, tm), lambda i: (0, i)),
                   pl.BlockSpec((n_exp + SUBLANES, LANES), lambda i: (0, 0))],
        out_shape=[jax.ShapeDtypeStruct((TOP_K, t_all), I32),
                   jax.ShapeDtypeStruct((n_exp + SUBLANES, LANES), I32)],
        compiler_params=_params("arbitrary"),
        name="slot_positions",
    )(eidx, rank, cnt)


def _expert_body(bstart_ref, xs_hbm, wg_ref, wu_ref, wd_ref, y_hbm,
                 xbuf, ybuf, wgu, wdn, sem_in, sem_out, *, block_rows):
    e = pl.program_id(0)
    n_exp = pl.num_programs(0)
    first, last, total = bstart_ref[e], bstart_ref[e + 1], bstart_ref[n_exp]
    f = wg_ref.shape[1]
    ahead = EXPERT_RING - EXPERT_GROUP

    def rows(i):
        return pl.ds(pl.multiple_of(i * block_rows, block_rows), block_rows)

    def slot_of(i):
        return i & (EXPERT_RING - 1)

    def in_copy(i):
        return pltpu.make_async_copy(xs_hbm.at[rows(i)], xbuf.at[slot_of(i)], sem_in.at[slot_of(i)])

    def out_copy(i):
        return pltpu.make_async_copy(ybuf.at[slot_of(i)], y_hbm.at[rows(i)], sem_out.at[slot_of(i)])

    def start_in(i):
        @pl.when(i < total)
        def _():
            in_copy(i).start()

    def wait_out(i):
        @pl.when(i >= 0)
        def _():
            out_copy(i).wait()

    @pl.when(e == 0)
    def _():
        for j in range(ahead):
            start_in(jnp.int32(j))

    wgu[:, :f] = wg_ref[...].astype(BF16)
    wgu[:, f:] = wu_ref[...].astype(BF16)
    wdn[...] = wd_ref[...].astype(BF16)

    def ffn(i, n):
        parts = []
        for j in range(n):
            lo, hi = _unpack_rows(xbuf[slot_of(i + j)])
            parts.append(jnp.concatenate([lo, hi], axis=1).astype(BF16))
        x = jnp.concatenate(parts, axis=0)
        gu = _dot(x, wgu[...])
        y = _pack_rows(_dot((_silu(gu[:, :f]) * gu[:, f:]).astype(BF16), wdn[...]))
        return [y[j * block_rows:(j + 1) * block_rows] for j in range(n)]

    def run_blocks(i, n):
        for j in range(n):
            in_copy(i + j).wait()
        for j in range(n):
            start_in(i + ahead + j)
        res = ffn(i, n)
        for j in range(n):
            wait_out(i + j - EXPERT_RING)
        for j in range(n):
            ybuf[slot_of(i + j)] = res[j]
        for j in range(n):
            out_copy(i + j).start()

    n_blk = last - first
    n_groups = n_blk // EXPERT_GROUP

    def group(p, carry):
        run_blocks(first + EXPERT_GROUP * p, EXPERT_GROUP)
        return carry

    lax.fori_loop(0, n_groups, group, 0)

    done = first + n_groups * EXPERT_GROUP
    size = EXPERT_GROUP // 2
    while size >= 1:
        @pl.when((n_blk & size) != 0)
        def _(done=done, size=size):
            run_blocks(done, size)
        done = done + (n_blk & size)
        size //= 2

    @pl.when(e == n_exp - 1)
    def _():
        for j in range(EXPERT_RING, 0, -1):
            wait_out(total - j)


def _experts(xs, bstart, w_gate, w_up, w_down, block_rows):
    n_slots, dw = xs.shape
    n_exp, d, f = w_gate.shape
    expert = lambda e, bs: (e, 0, 0)
    grid_spec = pltpu.PrefetchScalarGridSpec(
        num_scalar_prefetch=1,
        grid=(n_exp,),
        in_specs=[
            pl.BlockSpec(memory_space=pl.ANY),
            pl.BlockSpec((None, d, f), expert),
            pl.BlockSpec((None, d, f), expert),
            pl.BlockSpec((None, f, d), expert),
        ],
        out_specs=pl.BlockSpec(memory_space=pl.ANY),
        scratch_shapes=[
            pltpu.VMEM((EXPERT_RING, block_rows, dw), U32),
            pltpu.VMEM((EXPERT_RING, block_rows, dw), U32),
            pltpu.VMEM((d, 2 * f), BF16),
            pltpu.VMEM((f, d), BF16),
            pltpu.SemaphoreType.DMA((EXPERT_RING,)),
            pltpu.SemaphoreType.DMA((EXPERT_RING,)),
        ],
    )
    return pl.pallas_call(
        functools.partial(_expert_body, block_rows=block_rows),
        grid_spec=grid_spec,
        out_shape=jax.ShapeDtypeStruct((n_slots, dw), U32),
        compiler_params=_params("arbitrary"),
        name="expert_ffn",
    )(bstart, xs, w_gate, w_up, w_down)


SC_ROWS = 128


def _sc_workers():
    info = plsc.get_sparse_core_info()
    return info.num_cores, info.num_subcores


def _dispatch(h_rows, pos, n_slots):
    t_all, dw = h_rows.shape
    n_cores, n_sub = _sc_workers()
    per_worker = t_all // (n_cores * n_sub)
    assert per_worker * n_cores * n_sub == t_all and per_worker % SC_ROWS == 0
    mesh = plsc.VectorSubcoreMesh(core_axis_name="c", subcore_axis_name="s")

    @functools.partial(
        pl.kernel, mesh=mesh,
        out_type=jax.ShapeDtypeStruct((n_slots, dw), h_rows.dtype),
        scratch_types=[pltpu.VMEM((TOP_K, SC_ROWS), I32), pltpu.VMEM((SC_ROWS, dw), h_rows.dtype),
                       pltpu.SemaphoreType.DMA],
    )
    def body(h_hbm, pos_hbm, out_hbm, idx_v, rows_v, sem):
        base = (lax.axis_index("s") * n_cores + lax.axis_index("c")) * per_worker

        @pl.loop(0, per_worker // SC_ROWS)
        def _(j):
            t0 = pl.multiple_of(base + j * SC_ROWS, SC_ROWS)
            pltpu.sync_copy(pos_hbm.at[:, pl.ds(t0, SC_ROWS)], idx_v)
            pltpu.sync_copy(h_hbm.at[pl.ds(t0, SC_ROWS)], rows_v)
            copies = [pltpu.async_copy(rows_v, out_hbm.at[idx_v.at[k]], sem) for k in range(TOP_K)]
            for cp in copies:
                cp.wait()

    return body(h_rows, pos)


def _gather_back(y_rows, pos):
    top_k, t_all = pos.shape
    dw = y_rows.shape[1]
    n_cores, n_sub = _sc_workers()
    per_worker = t_all // (n_cores * n_sub)
    assert per_worker * n_cores * n_sub == t_all and per_worker % SC_ROWS == 0
    half = SC_ROWS // 2
    mesh = plsc.VectorSubcoreMesh(core_axis_name="c", subcore_axis_name="s")

    @functools.partial(
        pl.kernel, mesh=mesh,
        out_type=jax.ShapeDtypeStruct((top_k, t_all, dw), y_rows.dtype),
        scratch_types=[pltpu.VMEM((top_k, SC_ROWS), I32),
                       pltpu.VMEM((half, dw), y_rows.dtype), pltpu.VMEM((half, dw), y_rows.dtype),
                       pltpu.SemaphoreType.DMA, pltpu.SemaphoreType.DMA, pltpu.SemaphoreType.DMA],
    )
    def body(y_hbm, pos_hbm, out_hbm, idx_v, buf_a, buf_b, sem_g, sem_a, sem_b):
        base = (lax.axis_index("s") * n_cores + lax.axis_index("c")) * per_worker
        bufs, sems = (buf_a, buf_b), (sem_a, sem_b)

        @pl.loop(0, per_worker // SC_ROWS)
        def _(j):
            t0 = pl.multiple_of(base + j * SC_ROWS, SC_ROWS)
            pltpu.sync_copy(pos_hbm.at[:, pl.ds(t0, SC_ROWS)], idx_v)
            pending = [None, None]
            for step in range(2 * top_k):
                k, h = step // 2, step % 2
                slot = step % 2
                if pending[slot] is not None:
                    pending[slot].wait()
                pltpu.async_copy(y_hbm.at[idx_v.at[k, pl.ds(h * half, half)]], bufs[slot], sem_g).wait()
                pending[slot] = pltpu.async_copy(
                    bufs[slot], out_hbm.at[k, pl.ds(t0 + h * half, half)], sems[slot])
            for p in pending:
                p.wait()

    return body(y_rows, pos)


def _combine_body(yg_ref, w_ref, base_ref, mod_ref, fg_ref, o_ref):
    w_cols = jnp.transpose(w_ref[...])
    acc_lo = acc_hi = None
    for k in range(TOP_K):
        lo, hi = _unpack_rows(yg_ref[k])
        wk = w_cols[:, k:k + 1]
        acc_lo = lo * wk if acc_lo is None else acc_lo + lo * wk
        acc_hi = hi * wk if acc_hi is None else acc_hi + hi * wk
    acc = jnp.concatenate([acc_lo, acc_hi], axis=1)
    x2 = base_ref[...] + mod_ref[5:6, :] * acc
    ms = jnp.mean(x2 * x2, axis=-1, keepdims=True)
    o_ref[...] = (x2 * lax.rsqrt(ms + NORM_EPS)) * fg_ref[...]


def _combine(yg, wts, base, mod3, final_g):
    b, s, d = base.shape
    tm = min(COMBINE_TILE, s)
    n_t = s // tm
    return pl.pallas_call(
        _combine_body,
        grid=(b, n_t),
        in_specs=[pl.BlockSpec((TOP_K, tm, d // 2), lambda bi, i: (0, bi * n_t + i, 0)),
                  pl.BlockSpec((TOP_K, tm), lambda bi, i: (0, bi * n_t + i)),
                  pl.BlockSpec((None, tm, d), lambda bi, i: (bi, i, 0)),
                  pl.BlockSpec((None, N_ADA, d), lambda bi, i: (bi, 0, 0)),
                  pl.BlockSpec((1, d), lambda bi, i: (0, 0))],
        out_specs=pl.BlockSpec((None, tm, d), lambda bi, i: (bi, i, 0)),
        out_shape=jax.ShapeDtypeStruct((b, s, d), F32),
        compiler_params=_params("parallel", "parallel"),
        name="combine_norm",
    )(yg, wts, base, mod3, final_g)


def _rope_tables(seq_len):
    rows = seq_len // GRID_W
    row = jnp.repeat(jnp.arange(rows), GRID_W).astype(F32)
    col = jnp.tile(jnp.arange(GRID_W), rows).astype(F32)
    n_freq = HEAD_DIM // 4
    inv = ROPE_THETA ** (-jnp.arange(n_freq, dtype=F32) / n_freq)
    ar, ac = row[:, None] * inv, col[:, None] * inv
    zeros = jnp.zeros_like(ar)
    reps = LANES // HEAD_DIM
    cos = jnp.tile(jnp.concatenate([jnp.cos(ar), jnp.cos(ar), jnp.cos(ac), jnp.cos(ac)], 1), (1, reps))
    sin_a = jnp.tile(jnp.concatenate([-jnp.sin(ar), zeros, -jnp.sin(ac), zeros], 1), (1, reps))
    sin_b = jnp.tile(jnp.concatenate([zeros, jnp.sin(ar), zeros, jnp.sin(ac)], 1), (1, reps))
    return cos, sin_a, sin_b


def kernel(x, c, ctx, c_ctx, w_ada, b_ada, norm1_g, w_in, attn_sink, w_pool, pool_scale,
           w_up_attn, w_up_pool, w_out, norm2_g, w_router, router_bias,
           w_exp_gate, w_exp_up, w_exp_down, w_sh_gate, w_sh_up, w_sh_down, final_g):
    b, s, d = x.shape
    assert w_ada.shape[0] == 1, "single-layer block"
    assert s % ATTN_BLOCK == 0 and s % GRID_W == 0 and d % LANES == 0
    pool_dim = w_up_pool.shape[1]
    n_exp = w_router.shape[-1]
    t_all = b * s

    pad_rows = -(-(b + 1) // SUBLANES) * SUBLANES
    c_rows = jnp.concatenate([c, c_ctx[None, :], jnp.zeros((pad_rows - b - 1, d), F32)], axis=0)
    mod3 = _ada(c_rows, w_ada[0], b_ada[0]).reshape(pad_rows, N_ADA, d)

    w_in_bf = w_in[0].astype(BF16)
    g1 = norm1_g[0].reshape(1, d)
    cos, sin_a, sin_b = _rope_tables(s)
    q, k, v, u, gates = _inproj(x, mod3, g1, w_in_bf, cos, sin_a, sin_b, pool_dim)
    kc, vc = _ctxkv(ctx, mod3[b], g1, w_in_bf[:, ATTN_DIM:ATTN_DIM + 2 * KV_DIM])
    attn = _attention(q, k, v, kc, vc, attn_sink[0])

    w_r_t = w_router[0].T
    w_r_hi = w_r_t.astype(BF16)
    w_r_lo = (w_r_t - w_r_hi.astype(F32)).astype(BF16)
    base, h2, eidx, wts, rank, cnt = _mixer(
        x, attn, u, gates, mod3, norm2_g[0].reshape(1, d), w_pool[0].astype(BF16),
        pool_scale[0].reshape(1, pool_dim), w_up_attn[0].astype(BF16), w_up_pool[0].astype(BF16),
        w_out[0].astype(BF16), w_r_hi, w_r_lo, router_bias[0].reshape(n_exp, 1),
        w_sh_gate[0].astype(BF16), w_sh_up[0].astype(BF16), w_sh_down[0].astype(BF16))

    br = EXPERT_BLOCK_ROWS
    n_blocks = (t_all * TOP_K + n_exp * (br - 1) + br - 1) // br
    n_slots = n_blocks * br
    pos, bstart = _slots(eidx, rank, cnt, br)
    xs = _dispatch(h2, pos, n_slots)
    y = _experts(xs, bstart[:n_exp + 1, 0], w_exp_gate[0], w_exp_up[0], w_exp_down[0], br)
    yg = _gather_back(y, pos)

    return _combine(yg, wts, base, mod3, final_g.reshape(1, d))
```

```python
import functools

import jax
import jax.numpy as jnp
from jax import lax
from jax.experimental import pallas as pl
from jax.experimental.pallas import tpu as pltpu
from jax.experimental.pallas import tpu_sc as plsc

F32 = jnp.float32
BF16 = jnp.bfloat16
I32 = jnp.int32
U32 = jnp.uint32

GRID_W = 64
HEAD_DIM = 64
N_Q_HEADS = 8
N_KV_HEADS = 2
Q_PER_KV = N_Q_HEADS // N_KV_HEADS
ATTN_DIM = N_Q_HEADS * HEAD_DIM
KV_DIM = N_KV_HEADS * HEAD_DIM
ATTN_BLOCK = 128
ATTN_SCALE = HEAD_DIM ** -0.5
LOG2_E = 1.4426950408889634
ROPE_THETA = 10000.0
POOL_WINDOWS = (2, 4, 8, 16)
POOL_HALO = 8
N_EXPERT_GROUPS = 8
TOPK_GROUPS = 4
TOP_K = 8
ROUTED_SCALE = 2.5
N_ADA = 6
NORM_EPS = 1e-6
NEG_INF = -1e30
LANES = 128
SUBLANES = 8
ATTN_QBLOCKS = 4
TOKEN_TILE = 512
SLOT_TILE = 2048
COMBINE_TILE = 512
PROJ_TILE = 1024
EXPERT_BLOCK_ROWS = 256
EXPERT_GROUP = 4
EXPERT_RING = 16
V7X_VMEM_BYTES = 64 * 1024 * 1024
VMEM_LIMIT = V7X_VMEM_BYTES - 8 * 1024 * 1024
BF16_EXACT_INT = 256.0


def _sigmoid(x):
    return 1.0 / (1.0 + jnp.exp(-x))


def _silu(x):
    return x * _sigmoid(x)


def _nt_dot(a, b):
    return lax.dot_general(a, b, (((1,), (1,)), ((), ())), preferred_element_type=F32)


def _dot(a, b):
    return jnp.dot(a, b, preferred_element_type=F32)


def _pack_rows(x):
    n = x.shape[1] // 2
    bits = lax.bitcast_convert_type(x.astype(BF16).astype(F32), U32)
    return (bits[:, :n] >> 16) | (bits[:, n:] & jnp.uint32(0xFFFF0000))


def _unpack_rows(w):
    lo = lax.bitcast_convert_type(w << 16, F32)
    hi = lax.bitcast_convert_type(w & jnp.uint32(0xFFFF0000), F32)
    return lo, hi


def _params(*sem):
    return pltpu.CompilerParams(dimension_semantics=sem, vmem_limit_bytes=VMEM_LIMIT)


def _ada_body(c_ref, w_ref, b_ref, o_ref):
    s = _silu(c_ref[...])
    o_ref[...] = jnp.dot(s, w_ref[...], preferred_element_type=F32,
                         precision=lax.Precision.HIGHEST) + b_ref[...]


def _ada(c_rows, w_ada, b_ada):
    rows, d = c_rows.shape
    n = w_ada.shape[1]
    bn = d
    return pl.pallas_call(
        _ada_body,
        grid=(n // bn,),
        in_specs=[pl.BlockSpec((rows, d), lambda j: (0, 0)),
                  pl.BlockSpec((d, bn), lambda j: (0, j)),
                  pl.BlockSpec((1, bn), lambda j: (0, j))],
        out_specs=pl.BlockSpec((rows, bn), lambda j: (0, j)),
        out_shape=jax.ShapeDtypeStruct((rows, n), F32),
        compiler_params=_params("arbitrary"),
        name="ada_mod",
    )(c_rows, w_ada, b_ada.reshape(1, n))


def _norm_mod(x, g, shift, scale):
    ms = jnp.mean(x * x, axis=-1, keepdims=True)
    y = x * lax.rsqrt(ms + NORM_EPS)
    return y * (g * (1.0 + scale)) + shift


def _rope(t, cos, sin_a, sin_b):
    return (t * cos + pltpu.roll(t, LANES - HEAD_DIM // 4, 1) * sin_a
            + pltpu.roll(t, HEAD_DIM // 4, 1) * sin_b)


def _inproj_body(x_ref, mod_ref, g_ref, w_ref, cos_ref, sa_ref, sb_ref,
                 q_ref, k_ref, v_ref, u_ref, gate_ref):
    h = _norm_mod(x_ref[...], g_ref[...], mod_ref[0:1, :], mod_ref[1:2, :]).astype(BF16)
    cos, sa, sb = cos_ref[...], sa_ref[...], sb_ref[...]
    heads_per_chunk = LANES // HEAD_DIM
    wide = 2 * LANES
    for j in range(ATTN_DIM // wide):
        t2 = _dot(h, w_ref[:, j * wide:(j + 1) * wide])
        for c in range(wide // LANES):
            t = (_rope(t2[:, c * LANES:(c + 1) * LANES], cos, sa, sb) * (ATTN_SCALE * LOG2_E)).astype(BF16)
            for i in range(heads_per_chunk):
                head = (j * (wide // LANES) + c) * heads_per_chunk + i
                q_ref[head] = t[:, i * HEAD_DIM:(i + 1) * HEAD_DIM]
    k_off = ATTN_DIM
    v_off = k_off + KV_DIM
    kv = _dot(h, w_ref[:, k_off:v_off + KV_DIM])
    t = jnp.transpose(_rope(kv[:, :KV_DIM], cos, sa, sb)).astype(BF16)
    for i in range(N_KV_HEADS):
        k_ref[i] = t[i * HEAD_DIM:(i + 1) * HEAD_DIM, :]
    t = kv[:, KV_DIM:].astype(BF16)
    for i in range(N_KV_HEADS):
        v_ref[i] = t[:, i * HEAD_DIM:(i + 1) * HEAD_DIM]
    p_off = v_off + KV_DIM
    pool_dim = u_ref.shape[-1]
    u_ref[...] = _dot(h, w_ref[:, p_off:p_off + pool_dim])
    g_off = p_off + pool_dim
    gate_dim = gate_ref.shape[-1]
    chunk = 512
    for j in range(gate_dim // chunk):
        t = _dot(h, w_ref[:, g_off + j * chunk:g_off + (j + 1) * chunk])
        gate_ref[:, j * chunk:(j + 1) * chunk] = _sigmoid(t).astype(BF16)


def _inproj(x, mod3, g, w_in_bf, cos, sa, sb, pool_dim):
    b, s, d = x.shape
    tm = min(PROJ_TILE, s)
    in_dim = w_in_bf.shape[1]
    gate_dim = in_dim - ATTN_DIM - 2 * KV_DIM - pool_dim
    grid = (b, s // tm)
    return pl.pallas_call(
        _inproj_body,
        grid=grid,
        in_specs=[
            pl.BlockSpec((None, tm, d), lambda bi, i: (bi, i, 0)),
            pl.BlockSpec((None, N_ADA, d), lambda bi, i: (bi, 0, 0)),
            pl.BlockSpec((1, d), lambda bi, i: (0, 0)),
            pl.BlockSpec((d, in_dim), lambda bi, i: (0, 0)),
            pl.BlockSpec((tm, LANES), lambda bi, i: (i, 0)),
            pl.BlockSpec((tm, LANES), lambda bi, i: (i, 0)),
            pl.BlockSpec((tm, LANES), lambda bi, i: (i, 0)),
        ],
        out_specs=[
            pl.BlockSpec((None, N_Q_HEADS, tm, HEAD_DIM), lambda bi, i: (bi, 0, i, 0)),
            pl.BlockSpec((None, N_KV_HEADS, HEAD_DIM, tm), lambda bi, i: (bi, 0, 0, i)),
            pl.BlockSpec((None, N_KV_HEADS, tm, HEAD_DIM), lambda bi, i: (bi, 0, i, 0)),
            pl.BlockSpec((None, tm, pool_dim), lambda bi, i: (bi, i, 0)),
            pl.BlockSpec((None, tm, gate_dim), lambda bi, i: (bi, i, 0)),
        ],
        out_shape=[
            jax.ShapeDtypeStruct((b, N_Q_HEADS, s, HEAD_DIM), BF16),
            jax.ShapeDtypeStruct((b, N_KV_HEADS, HEAD_DIM, s), BF16),
            jax.ShapeDtypeStruct((b, N_KV_HEADS, s, HEAD_DIM), BF16),
            jax.ShapeDtypeStruct((b, s, pool_dim), F32),
            jax.ShapeDtypeStruct((b, s, gate_dim), BF16),
        ],
        compiler_params=_params("parallel", "parallel"),
        name="in_proj",
    )(x, mod3, g, w_in_bf, cos, sa, sb)


def _ctxkv_body(ctx_ref, mod_ref, g_ref, w_ref, kc_ref, vc_ref):
    h = _norm_mod(ctx_ref[...], g_ref[...], mod_ref[0:1, :], mod_ref[1:2, :]).astype(BF16)
    t = _dot(h, w_ref[...])
    kt = jnp.transpose(t[:, :KV_DIM]).astype(BF16)
    vt = t[:, KV_DIM:].astype(BF16)
    for i in range(N_KV_HEADS):
        kc_ref[i] = kt[i * HEAD_DIM:(i + 1) * HEAD_DIM, :]
        vc_ref[i] = vt[:, i * HEAD_DIM:(i + 1) * HEAD_DIM]


def _ctxkv(ctx, mod_c, g, w_kv_bf):
    b, c, d = ctx.shape
    v_out = jax.ShapeDtypeStruct((b, N_KV_HEADS, c, HEAD_DIM), BF16)
    k_out = jax.ShapeDtypeStruct((b, N_KV_HEADS, HEAD_DIM, c), BF16)
    v_spec = pl.BlockSpec((None, N_KV_HEADS, c, HEAD_DIM), lambda bi: (bi, 0, 0, 0))
    k_spec = pl.BlockSpec((None, N_KV_HEADS, HEAD_DIM, c), lambda bi: (bi, 0, 0, 0))
    return pl.pallas_call(
        _ctxkv_body,
        grid=(b,),
        in_specs=[pl.BlockSpec((None, c, d), lambda bi: (bi, 0, 0)),
                  pl.BlockSpec((N_ADA, d), lambda bi: (0, 0)),
                  pl.BlockSpec((1, d), lambda bi: (0, 0)),
                  pl.BlockSpec((d, 2 * KV_DIM), lambda bi: (0, 0))],
        out_specs=[k_spec, v_spec],
        out_shape=[k_out, v_out],
        compiler_params=_params("parallel"),
        name="ctx_kv",
    )(ctx, mod_c, g, w_kv_bf)


def _attn_body(sink_ref, q_ref, *refs):
    nkb = ATTN_QBLOCKS + 2
    k_refs, v_refs = refs[:nkb], refs[nkb:2 * nkb]
    kc_ref, vc_ref, lo_ref, hi_ref, o_ref = refs[2 * nkb:]
    n = pl.program_id(1)
    last = pl.num_programs(1) - 1
    rows = Q_PER_KV * ATTN_BLOCK
    row_head = lax.broadcasted_iota(I32, (rows, 1), 0) // ATTN_BLOCK
    for qb in range(ATTN_QBLOCKS):
        bias_lo, bias_hi = lo_ref[...], hi_ref[...]
        if qb == 0:
            bias_lo = bias_lo + jnp.where(n == 0, NEG_INF, 0.0)
        if qb == ATTN_QBLOCKS - 1:
            bias_hi = bias_hi + jnp.where(n == last, NEG_INF, 0.0)
        q_rows = slice(qb * ATTN_BLOCK, (qb + 1) * ATTN_BLOCK)
        outs = []
        for kk in range(N_KV_HEADS):
            qs = q_ref[kk * Q_PER_KV:(kk + 1) * Q_PER_KV, q_rows, :].reshape(rows, HEAD_DIM)
            k_loc = jnp.concatenate([k_refs[qb + j][kk] for j in range(3)], axis=1)
            v_loc = jnp.concatenate([v_refs[qb + j][kk] for j in range(3)], axis=0)
            s_loc = _dot(qs, k_loc)
            s0 = s_loc[:, :ATTN_BLOCK] + bias_lo
            s1 = s_loc[:, ATTN_BLOCK:2 * ATTN_BLOCK]
            s2 = s_loc[:, 2 * ATTN_BLOCK:] + bias_hi
            sc = _dot(qs, kc_ref[kk])
            sink = jnp.zeros((rows, 1), F32)
            for g in range(Q_PER_KV):
                sink = jnp.where(row_head == g, sink_ref[kk * Q_PER_KV + g] * LOG2_E, sink)
            n_ctx = sc.shape[1] // ATTN_BLOCK
            folded = jnp.maximum(jnp.maximum(s0, s1), s2)
            for j in range(n_ctx):
                folded = jnp.maximum(folded, sc[:, j * ATTN_BLOCK:(j + 1) * ATTN_BLOCK])
            m = jnp.maximum(jnp.max(folded, axis=1, keepdims=True), sink)
            p0, p1, p2, pc = jnp.exp2(s0 - m), jnp.exp2(s1 - m), jnp.exp2(s2 - m), jnp.exp2(sc - m)
            folded = p0 + p1 + p2
            for j in range(n_ctx):
                folded = folded + pc[:, j * ATTN_BLOCK:(j + 1) * ATTN_BLOCK]
            denom = jnp.sum(folded, axis=1, keepdims=True) + jnp.exp2(sink - m)
            p_loc = jnp.concatenate([p0, p1, p2], axis=1).astype(BF16)
            o = _dot(p_loc, v_loc) + _dot(pc.astype(BF16), vc_ref[kk])
            o = o / denom
            outs += [o[g * ATTN_BLOCK:(g + 1) * ATTN_BLOCK] for g in range(Q_PER_KV)]
        o_ref[q_rows, :] = jnp.concatenate(outs, axis=1).astype(BF16)


def _attention(q, k, v, kc, vc, sink):
    b, _, s, _ = q.shape
    c = vc.shape[2]
    nb = s // ATTN_BLOCK
    assert nb % ATTN_QBLOCKS == 0 and c % ATTN_BLOCK == 0
    rows = Q_PER_KV * ATTN_BLOCK
    q_rows = ATTN_QBLOCKS * ATTN_BLOCK
    qi = jnp.tile(jnp.arange(ATTN_BLOCK), Q_PER_KV)[:, None]
    kj = jnp.arange(ATTN_BLOCK)[None, :]
    bias_lo = jnp.where(kj >= qi, 0.0, NEG_INF).astype(F32)
    bias_hi = jnp.where(kj <= qi, 0.0, NEG_INF).astype(F32)

    def blk(n, i):
        return jnp.clip(ATTN_QBLOCKS * n - 1 + i, 0, nb - 1)

    k_specs = [pl.BlockSpec((None, N_KV_HEADS, HEAD_DIM, ATTN_BLOCK), lambda bi, n, i=i: (bi, 0, 0, blk(n, i)))
               for i in range(ATTN_QBLOCKS + 2)]
    v_specs = [pl.BlockSpec((None, N_KV_HEADS, ATTN_BLOCK, HEAD_DIM), lambda bi, n, i=i: (bi, 0, blk(n, i), 0))
               for i in range(ATTN_QBLOCKS + 2)]
    kc_spec = pl.BlockSpec((None, N_KV_HEADS, HEAD_DIM, c), lambda bi, n: (bi, 0, 0, 0))
    vc_spec = pl.BlockSpec((None, N_KV_HEADS, c, HEAD_DIM), lambda bi, n: (bi, 0, 0, 0))
    bias_spec = pl.BlockSpec((rows, ATTN_BLOCK), lambda bi, n: (0, 0))
    return pl.pallas_call(
        _attn_body,
        grid=(b, nb // ATTN_QBLOCKS),
        in_specs=[
            pl.BlockSpec(memory_space=pltpu.SMEM),
            pl.BlockSpec((None, N_Q_HEADS, q_rows, HEAD_DIM), lambda bi, n: (bi, 0, n, 0)),
            *k_specs, *v_specs, kc_spec, vc_spec, bias_spec, bias_spec,
        ],
        out_specs=pl.BlockSpec((None, q_rows, ATTN_DIM), lambda bi, n: (bi, n, 0)),
        out_shape=jax.ShapeDtypeStruct((b, s, ATTN_DIM), BF16),
        compiler_params=_params("parallel", "parallel"),
        name="window_attn",
    )(sink, q, *([k] * (ATTN_QBLOCKS + 2)), *([v] * (ATTN_QBLOCKS + 2)), kc, vc, bias_lo, bias_hi)


def _pool_delta(u_ref, up, un, inv_cnt, bufs):
    tm = u_ref.shape[0]
    n_ext = tm + 2 * POOL_HALO
    pad = POOL_HALO
    group_dim = u_ref.shape[1] // len(POOL_WINDOWS)
    outs = []
    for g, w in enumerate(POOL_WINDOWS):
        lanes = slice(g * group_dim, (g + 1) * group_dim)
        level = bufs[g]
        level[0][pad:pad + POOL_HALO, :] = up[:, lanes]
        level[0][pad + POOL_HALO:pad + POOL_HALO + tm, :] = u_ref[:, lanes]
        level[0][pad + POOL_HALO + tm:pad + n_ext, :] = un[:, lanes]
        acc = level[0][pl.ds(pad - 1, n_ext), :] + level[0][pl.ds(pad, n_ext), :]
        step = 1
        for j in range(1, len(level)):
            level[j][pad:pad + n_ext, :] = acc
            acc = level[j][pl.ds(pad - step, n_ext), :] + level[j][pl.ds(pad + step, n_ext), :]
            step *= 2
        outs.append(acc[POOL_HALO:POOL_HALO + tm] * inv_cnt[:, g:g + 1] - u_ref[:, lanes])
    return outs


def _route_steps(s, sel, iota_ref, out):
    n_exp = s.shape[0]
    per_group = n_exp // N_EXPERT_GROUPS
    neg = float("-inf")
    iota_g = iota_ref[0:per_group, :]
    scores = []
    for g in range(N_EXPERT_GROUPS):
        blk = sel[g * per_group:(g + 1) * per_group]
        m1 = jnp.max(blk, axis=0, keepdims=True)
        first = jnp.min(jnp.where(blk == m1, iota_g, float(per_group)), axis=0, keepdims=True)
        m2 = jnp.max(jnp.where(iota_g == first, neg, blk), axis=0, keepdims=True)
        scores.append(m1 + m2)
        if g % 2 == 1:
            yield
    gs = jnp.concatenate(scores, axis=0)
    iota_ng = iota_ref[0:N_EXPERT_GROUPS, :]
    gsel = jnp.zeros(iota_ng.shape, F32)
    for _ in range(TOPK_GROUPS):
        m = jnp.max(gs, axis=0, keepdims=True)
        first = jnp.min(jnp.where(gs == m, iota_ng, float(N_EXPERT_GROUPS)), axis=0, keepdims=True)
        hit = iota_ng == first
        gsel = jnp.where(hit, 1.0, gsel)
        gs = jnp.where(hit, neg, gs)
    cur = jnp.concatenate(
        [jnp.where(gsel[g:g + 1] > 0.0, sel[g * per_group:(g + 1) * per_group], NEG_INF)
         for g in range(N_EXPERT_GROUPS)], axis=0)
    yield
    chosen = jnp.zeros(s.shape, F32)
    ids, aff = [], []
    for _ in range(TOP_K):
        m = jnp.max(cur, axis=0, keepdims=True)
        first = jnp.min(jnp.where(cur == m, iota_ref[...], float(n_exp)), axis=0, keepdims=True)
        hit = iota_ref[...] == first
        ids.append(first)
        aff.append(jnp.sum(jnp.where(hit, s, 0.0), axis=0, keepdims=True))
        chosen = jnp.where(hit, 1.0, chosen)
        cur = jnp.where(hit, neg, cur)
        yield
    out.update(ids=ids, aff=aff, chosen=chosen)


def _mixer_body(x_ref, attn_ref, up_ref, u_ref, un_ref, gate_ref, mod_ref, icnt_ref, iota_ref, n2g_ref,
                wpool_ref, pscale_ref, wua_ref, wup_ref, wout_ref, wrh_ref, wrl_ref, rbias_ref,
                wsg_ref, wsu_ref, wsd_ref,
                base_ref, h2_ref, eidx_ref, wts_ref, rank_ref, cnt_ref, score_buf, *pool_bufs, n_t):
    g = pl.program_id(0)
    n_tiles = pl.num_programs(0) - 1
    ti = lax.rem(jnp.minimum(g, n_tiles - 1), n_t)
    slot = g & 1
    tm, d = x_ref.shape
    pool_dim = u_ref.shape[-1]
    n_exp = score_buf.shape[1]

    @pl.when(g == 0)
    def _():
        cnt_ref[...] = jnp.zeros_like(cnt_ref)
        score_buf[1] = jnp.zeros((n_exp, tm), F32)
        for buf in pool_bufs:
            buf[0:POOL_HALO, :] = jnp.zeros((POOL_HALO, buf.shape[1]), F32)
            buf[buf.shape[0] - POOL_HALO:, :] = jnp.zeros((POOL_HALO, buf.shape[1]), F32)
    levels = [w.bit_length() - 1 for w in POOL_WINDOWS]
    group_bufs = [pool_bufs[sum(levels[:i]):sum(levels[:i + 1])] for i in range(len(levels))]

    def routing():
        s = score_buf[1 - slot]
        sel = s + rbias_ref[...]
        res = {}
        yield from _route_steps(s, sel, iota_ref, res)
        ids, aff, chosen = res["ids"], res["aff"], res["chosen"]
        total = aff[0]
        for a in aff[1:]:
            total = total + a
        wts_ref[...] = jnp.concatenate([a / total * ROUTED_SCALE for a in aff], axis=0)
        eidx_ref[...] = jnp.concatenate(ids, axis=0).astype(I32)
        yield
        before = (lax.broadcasted_iota(I32, (tm, tm), 0) < lax.broadcasted_iota(I32, (tm, tm), 1))
        prefix = _dot(chosen.astype(BF16), jnp.where(before, 1.0, 0.0).astype(BF16))
        rank_dense = prefix + cnt_ref[:, 0:1]
        rank_ref[...] = jnp.concatenate(
            [jnp.sum(jnp.where(iota_ref[...] == i, rank_dense, 0.0), axis=0, keepdims=True) for i in ids],
            axis=0).astype(I32)
        real = jnp.where(g > 0, 1.0, 0.0)
        cnt_ref[...] = cnt_ref[...] + real * jnp.sum(chosen, axis=1, keepdims=True)

    def chain():
        zero_halo = jnp.zeros((POOL_HALO, pool_dim), F32)
        u_prev = jnp.where(ti == 0, zero_halo, up_ref[...])
        u_next = jnp.where(ti == n_t - 1, zero_halo, un_ref[...])
        deltas = _pool_delta(u_ref, u_prev, u_next, icnt_ref[...], group_bufs)
        attn_up = _dot(attn_ref[...], wua_ref[...])
        yield
        pool = jnp.concatenate(
            [_dot(dl.astype(BF16), wpool_ref[i]) for i, dl in enumerate(deltas)], axis=1)
        pool = (pool * pscale_ref[...]).astype(BF16)
        yield
        pool_up = _dot(pool, wup_ref[...])
        yield
        y = gate_ref[:, :d] * attn_up.astype(BF16) + gate_ref[:, d:] * pool_up.astype(BF16)
        mix = _dot(y, wout_ref[...])
        yield
        x1 = x_ref[...] + mod_ref[2:3, :] * mix
        h2 = _norm_mod(x1, n2g_ref[...], mod_ref[3:4, :], mod_ref[4:5, :])
        h2_hi = h2.astype(BF16)
        h2_ref[...] = _pack_rows(h2)
        gate = _dot(h2_hi, wsg_ref[...])
        yield
        up = _dot(h2_hi, wsu_ref[...])
        yield
        shared = _dot((_silu(gate) * up).astype(BF16), wsd_ref[...])
        yield
        base_ref[...] = x1 + mod_ref[5:6, :] * shared
        h2_lo = (h2 - h2_hi.astype(F32)).astype(BF16)
        logits = _nt_dot(wrh_ref[...], h2_hi)
        yield
        logits = logits + _nt_dot(wrh_ref[...], h2_lo)
        yield
        logits = logits + _nt_dot(wrl_ref[...], h2_hi)
        score_buf[slot] = _sigmoid(logits)

    stages = [chain(), routing()]
    while stages:
        for gen in list(stages):
            try:
                next(gen)
            except StopIteration:
                stages.remove(gen)


def _mixer(x, attn, u, gates, mod3, n2g, wpool_bf, pscale, wua_bf, wup_bf, wout_bf,
           wr_hi, wr_lo, rbias, wsg_bf, wsu_bf, wsd_bf):
    b, s, d = x.shape
    tm = min(TOKEN_TILE, s)
    n_t = s // tm
    n_tiles = b * n_t
    t_all = b * s
    pool_dim = u.shape[-1]
    n_exp = wr_hi.shape[0]
    sh = wsg_bf.shape[1]
    halo_blocks = tm // POOL_HALO
    n_halo = s // POOL_HALO
    full = lambda shape: pl.BlockSpec(shape, lambda g: (0,) * len(shape))

    def tile(g):
        tg = jnp.minimum(g, n_tiles - 1)
        return tg // n_t, lax.rem(tg, n_t)

    def seq_block(g):
        bi, i = tile(g)
        return bi, i, 0

    def halo_prev(g):
        bi, i = tile(g)
        return bi, jnp.maximum(i * halo_blocks - 1, 0), 0

    def halo_next(g):
        bi, i = tile(g)
        return bi, jnp.minimum((i + 1) * halo_blocks, n_halo - 1), 0

    routed = lambda g: (0, jnp.maximum(g - 1, 0))
    t = jnp.arange(s)
    inv_cnt = jnp.stack(
        [1.0 / (jnp.minimum(t + w // 2, s) - jnp.maximum(t - w // 2, 0)).astype(F32) for w in POOL_WINDOWS], axis=1)
    expert_iota = jnp.broadcast_to(jnp.arange(n_exp, dtype=F32)[:, None], (n_exp, tm))
    return pl.pallas_call(
        functools.partial(_mixer_body, n_t=n_t),
        grid=(n_tiles + 1,),
        in_specs=[
            pl.BlockSpec((None, tm, d), seq_block),
            pl.BlockSpec((None, tm, ATTN_DIM), seq_block),
            pl.BlockSpec((None, POOL_HALO, pool_dim), halo_prev),
            pl.BlockSpec((None, tm, pool_dim), seq_block),
            pl.BlockSpec((None, POOL_HALO, pool_dim), halo_next),
            pl.BlockSpec((None, tm, 2 * d), seq_block),
            pl.BlockSpec((None, N_ADA, d), lambda g: (tile(g)[0], 0, 0)),
            pl.BlockSpec((tm, len(POOL_WINDOWS)), lambda g: (tile(g)[1], 0)),
            full((n_exp, tm)),
            full((1, d)),
            full((len(POOL_WINDOWS), pool_dim // len(POOL_WINDOWS), pool_dim // len(POOL_WINDOWS))),
            full((1, pool_dim)),
            full((ATTN_DIM, d)), full((pool_dim, d)), full((d, d)),
            full((n_exp, d)), full((n_exp, d)), full((n_exp, 1)),
            full((d, sh)), full((d, sh)), full((sh, d)),
        ],
        out_specs=[
            pl.BlockSpec((None, tm, d), seq_block),
            pl.BlockSpec((tm, d // 2), lambda g: (jnp.minimum(g, n_tiles - 1), 0)),
            pl.BlockSpec((TOP_K, tm), routed),
            pl.BlockSpec((TOP_K, tm), routed),
            pl.BlockSpec((TOP_K, tm), routed),
            pl.BlockSpec((n_exp, LANES), lambda g: (0, 0)),
        ],
        out_shape=[
            jax.ShapeDtypeStruct((b, s, d), F32),
            jax.ShapeDtypeStruct((t_all, d // 2), U32),
            jax.ShapeDtypeStruct((TOP_K, t_all), I32),
            jax.ShapeDtypeStruct((TOP_K, t_all), F32),
            jax.ShapeDtypeStruct((TOP_K, t_all), I32),
            jax.ShapeDtypeStruct((n_exp, LANES), F32),
        ],
        scratch_shapes=[pltpu.VMEM((2, n_exp, tm), F32)] + [
            pltpu.VMEM((tm + 4 * POOL_HALO, pool_dim // len(POOL_WINDOWS)), F32)
            for w in POOL_WINDOWS for _ in range(w.bit_length() - 1)],
        compiler_params=_params("arbitrary"),
        name="mixer_router",
    )(x, attn, u, u, u, gates, mod3, inv_cnt, expert_iota, n2g, wpool_bf, pscale, wua_bf, wup_bf, wout_bf,
      wr_hi, wr_lo, rbias, wsg_bf, wsu_bf, wsd_bf)


def _slots_body(eidx_ref, rank_ref, cnt_ref, pos_ref, bstart_ref, *, block_rows):
    n_exp = cnt_ref.shape[0]
    tm = eidx_ref.shape[1]
    cnt = cnt_ref[...]
    padded = jnp.floor((cnt + (block_rows - 1)) / block_rows) * block_rows
    hi = jnp.floor(padded / BF16_EXACT_INT)
    lo = padded - hi * BF16_EXACT_INT
    below = (lax.broadcasted_iota(I32, (n_exp, n_exp), 1) < lax.broadcasted_iota(I32, (n_exp, n_exp), 0))
    tri = jnp.where(below, 1.0, 0.0).astype(BF16)
    start = BF16_EXACT_INT * _dot(tri, hi.astype(BF16)) + _dot(tri, lo.astype(BF16))
    end = start + padded
    iota_e = lax.broadcasted_iota(I32, (n_exp, tm), 0)
    start_col = start[:, 0:1]
    rows = []
    for k in range(TOP_K):
        hit = iota_e == eidx_ref[k:k + 1, :]
        rows.append(jnp.sum(jnp.where(hit, start_col, 0.0), axis=0, keepdims=True))
    pos_ref[...] = jnp.concatenate(rows, axis=0).astype(I32) + rank_ref[...]

    @pl.when(pl.program_id(0) == 0)
    def _():
        bstart_ref[0:n_exp, :] = (start / block_rows).astype(I32)
        bstart_ref[n_exp:, :] = jnp.broadcast_to(
            (end[n_exp - 1:n_exp, :] / block_rows).astype(I32), (SUBLANES, LANES))


def _slots(eidx, rank, cnt, block_rows):
    t_all = eidx.shape[1]
    n_exp = cnt.shape[0]
    tm = min(SLOT_TILE, t_all)
    return pl.pallas_call(
        functools.partial(_slots_body, block_rows=block_rows),
        grid=(t_all // tm,),
        in_specs=[pl.BlockSpec((TOP_K, tm), lambda i: (0, i)),
                  pl.BlockSpec((TOP_K, tm), lambda i: (0, i)),
                  pl.BlockSpec((n_exp, LANES), lambda i: (0, 0))],
        out_specs=[pl.BlockSpec((TOP_K, tm), lambda i: (0, i)),
                   pl.BlockSpec((n_exp + SUBLANES, LANES), lambda i: (0, 0))],
        out_shape=[jax.ShapeDtypeStruct((TOP_K, t_all), I32),
                   jax.ShapeDtypeStruct((n_exp + SUBLANES, LANES), I32)],
        compiler_params=_params("arbitrary"),
        name="slot_positions",
    )(eidx, rank, cnt)


def _expert_body(bstart_ref, xs_hbm, wg_ref, wu_ref, wd_ref, y_hbm,
                 xbuf, ybuf, wgu, wdn, sem_in, sem_out, *, block_rows):
    e = pl.program_id(0)
    n_exp = pl.num_programs(0)
    first, last, total = bstart_ref[e], bstart_ref[e + 1], bstart_ref[n_exp]
    f = wg_ref.shape[1]
    ahead = EXPERT_RING - EXPERT_GROUP

    def rows(i):
        return pl.ds(pl.multiple_of(i * block_rows, block_rows), block_rows)

    def slot_of(i):
        return i & (EXPERT_RING - 1)

    def in_copy(i):
        return pltpu.make_async_copy(xs_hbm.at[rows(i)], xbuf.at[slot_of(i)], sem_in.at[slot_of(i)])

    def out_copy(i):
        return pltpu.make_async_copy(ybuf.at[slot_of(i)], y_hbm.at[rows(i)], sem_out.at[slot_of(i)])

    def start_in(i):
        @pl.when(i < total)
        def _():
            in_copy(i).start()

    def wait_out(i):
        @pl.when(i >= 0)
        def _():
            out_copy(i).wait()

    @pl.when(e == 0)
    def _():
        for j in range(ahead):
            start_in(jnp.int32(j))

    wgu[:, :f] = wg_ref[...].astype(BF16)
    wgu[:, f:] = wu_ref[...].astype(BF16)
    wdn[...] = wd_ref[...].astype(BF16)

    def ffn(i, n):
        parts = []
        for j in range(n):
            lo, hi = _unpack_rows(xbuf[slot_of(i + j)])
            parts.append(jnp.concatenate([lo, hi], axis=1).astype(BF16))
        x = jnp.concatenate(parts, axis=0)
        gu = _dot(x, wgu[...])
        y = _pack_rows(_dot((_silu(gu[:, :f]) * gu[:, f:]).astype(BF16), wdn[...]))
        return [y[j * block_rows:(j + 1) * block_rows] for j in range(n)]

    def run_blocks(i, n):
        for j in range(n):
            in_copy(i + j).wait()
        for j in range(n):
            start_in(i + ahead + j)
        res = ffn(i, n)
        for j in range(n):
            wait_out(i + j - EXPERT_RING)
        for j in range(n):
            ybuf[slot_of(i + j)] = res[j]
        for j in range(n):
            out_copy(i + j).start()

    n_blk = last - first
    n_groups = n_blk // EXPERT_GROUP

    def group(p, carry):
        run_blocks(first + EXPERT_GROUP * p, EXPERT_GROUP)
        return carry

    lax.fori_loop(0, n_groups, group, 0)

    done = first + n_groups * EXPERT_GROUP
    size = EXPERT_GROUP // 2
    while size >= 1:
        @pl.when((n_blk & size) != 0)
        def _(done=done, size=size):
            run_blocks(done, size)
        done = done + (n_blk & size)
        size //= 2

    @pl.when(e == n_exp - 1)
    def _():
        for j in range(EXPERT_RING, 0, -1):
            wait_out(total - j)


def _experts(xs, bstart, w_gate, w_up, w_down, block_rows):
    n_slots, dw = xs.shape
    n_exp, d, f = w_gate.shape
    expert = lambda e, bs: (e, 0, 0)
    grid_spec = pltpu.PrefetchScalarGridSpec(
        num_scalar_prefetch=1,
        grid=(n_exp,),
        in_specs=[
            pl.BlockSpec(memory_space=pl.ANY),
            pl.BlockSpec((None, d, f), expert),
            pl.BlockSpec((None, d, f), expert),
            pl.BlockSpec((None, f, d), expert),
        ],
        out_specs=pl.BlockSpec(memory_space=pl.ANY),
        scratch_shapes=[
            pltpu.VMEM((EXPERT_RING, block_rows, dw), U32),
            pltpu.VMEM((EXPERT_RING, block_rows, dw), U32),
            pltpu.VMEM((d, 2 * f), BF16),
            pltpu.VMEM((f, d), BF16),
            pltpu.SemaphoreType.DMA((EXPERT_RING,)),
            pltpu.SemaphoreType.DMA((EXPERT_RING,)),
        ],
    )
    return pl.pallas_call(
        functools.partial(_expert_body, block_rows=block_rows),
        grid_spec=grid_spec,
        out_shape=jax.ShapeDtypeStruct((n_slots, dw), U32),
        compiler_params=_params("arbitrary"),
        name="expert_ffn",
    )(bstart, xs, w_gate, w_up, w_down)


SC_ROWS = 128


def _sc_workers():
    info = plsc.get_sparse_core_info()
    return info.num_cores, info.num_subcores


def _dispatch(h_rows, pos, n_slots):
    t_all, dw = h_rows.shape
    n_cores, n_sub = _sc_workers()
    per_worker = t_all // (n_cores * n_sub)
    assert per_worker * n_cores * n_sub == t_all and per_worker % SC_ROWS == 0
    mesh = plsc.VectorSubcoreMesh(core_axis_name="c", subcore_axis_name="s")

    @functools.partial(
        pl.kernel, mesh=mesh,
        out_type=jax.ShapeDtypeStruct((n_slots, dw), h_rows.dtype),
        scratch_types=[pltpu.VMEM((TOP_K, SC_ROWS), I32), pltpu.VMEM((SC_ROWS, dw), h_rows.dtype),
                       pltpu.SemaphoreType.DMA],
    )
    def body(h_hbm, pos_hbm, out_hbm, idx_v, rows_v, sem):
        base = (lax.axis_index("s") * n_cores + lax.axis_index("c")) * per_worker

        @pl.loop(0, per_worker // SC_ROWS)
        def _(j):
            t0 = pl.multiple_of(base + j * SC_ROWS, SC_ROWS)
            pltpu.sync_copy(pos_hbm.at[:, pl.ds(t0, SC_ROWS)], idx_v)
            pltpu.sync_copy(h_hbm.at[pl.ds(t0, SC_ROWS)], rows_v)
            copies = [pltpu.async_copy(rows_v, out_hbm.at[idx_v.at[k]], sem) for k in range(TOP_K)]
            for cp in copies:
                cp.wait()

    return body(h_rows, pos)


def _gather_back(y_rows, pos):
    top_k, t_all = pos.shape
    dw = y_rows.shape[1]
    n_cores, n_sub = _sc_workers()
    per_worker = t_all // (n_cores * n_sub)
    assert per_worker * n_cores * n_sub == t_all and per_worker % SC_ROWS == 0
    half = SC_ROWS // 2
    mesh = plsc.VectorSubcoreMesh(core_axis_name="c", subcore_axis_name="s")

    @functools.partial(
        pl.kernel, mesh=mesh,
        out_type=jax.ShapeDtypeStruct((top_k, t_all, dw), y_rows.dtype),
        scratch_types=[pltpu.VMEM((top_k, SC_ROWS), I32),
                       pltpu.VMEM((half, dw), y_rows.dtype), pltpu.VMEM((half, dw), y_rows.dtype),
                       pltpu.SemaphoreType.DMA, pltpu.SemaphoreType.DMA, pltpu.SemaphoreType.DMA],
    )
    def body(y_hbm, pos_hbm, out_hbm, idx_v, buf_a, buf_b, sem_g, sem_a, sem_b):
        base = (lax.axis_index("s") * n_cores + lax.axis_index("c")) * per_worker
        bufs, sems = (buf_a, buf_b), (sem_a, sem_b)

        @pl.loop(0, per_worker // SC_ROWS)
        def _(j):
            t0 = pl.multiple_of(base + j * SC_ROWS, SC_ROWS)
            pltpu.sync_copy(pos_hbm.at[:, pl.ds(t0, SC_ROWS)], idx_v)
            pending = [None, None]
            for step in range(2 * top_k):
                k, h = step // 2, step % 2
                slot = step % 2
                if pending[slot] is not None:
                    pending[slot].wait()
                pltpu.async_copy(y_hbm.at[idx_v.at[k, pl.ds(h * half, half)]], bufs[slot], sem_g).wait()
                pending[slot] = pltpu.async_copy(
                    bufs[slot], out_hbm.at[k, pl.ds(t0 + h * half, half)], sems[slot])
            for p in pending:
                p.wait()

    return body(y_rows, pos)


def _combine_body(yg_ref, w_ref, base_ref, mod_ref, fg_ref, o_ref):
    w_cols = jnp.transpose(w_ref[...])
    acc_lo = acc_hi = None
    for k in range(TOP_K):
        lo, hi = _unpack_rows(yg_ref[k])
        wk = w_cols[:, k:k + 1]
        acc_lo = lo * wk if acc_lo is None else acc_lo + lo * wk
        acc_hi = hi * wk if acc_hi is None else acc_hi + hi * wk
    acc = jnp.concatenate([acc_lo, acc_hi], axis=1)
    x2 = base_ref[...] + mod_ref[5:6, :] * acc
    ms = jnp.mean(x2 * x2, axis=-1, keepdims=True)
    o_ref[...] = (x2 * lax.rsqrt(ms + NORM_EPS)) * fg_ref[...]


def _combine(yg, wts, base, mod3, final_g):
    b, s, d = base.shape
    tm = min(COMBINE_TILE, s)
    n_t = s // tm
    return pl.pallas_call(
        _combine_body,
        grid=(b, n_t),
        in_specs=[pl.BlockSpec((TOP_K, tm, d // 2), lambda bi, i: (0, bi * n_t + i, 0)),
                  pl.BlockSpec((TOP_K, tm), lambda bi, i: (0, bi * n_t + i)),
                  pl.BlockSpec((None, tm, d), lambda bi, i: (bi, i, 0)),
                  pl.BlockSpec((None, N_ADA, d), lambda bi, i: (bi, 0, 0)),
                  pl.BlockSpec((1, d), lambda bi, i: (0, 0))],
        out_specs=pl.BlockSpec((None, tm, d), lambda bi, i: (bi, i, 0)),
        out_shape=jax.ShapeDtypeStruct((b, s, d), F32),
        compiler_params=_params("parallel", "parallel"),
        name="combine_norm",
    )(yg, wts, base, mod3, final_g)


def _rope_tables(seq_len):
    rows = seq_len // GRID_W
    row = jnp.repeat(jnp.arange(rows), GRID_W).astype(F32)
    col = jnp.tile(jnp.arange(GRID_W), rows).astype(F32)
    n_freq = HEAD_DIM // 4
    inv = ROPE_THETA ** (-jnp.arange(n_freq, dtype=F32) / n_freq)
    ar, ac = row[:, None] * inv, col[:, None] * inv
    zeros = jnp.zeros_like(ar)
    reps = LANES // HEAD_DIM
    cos = jnp.tile(jnp.concatenate([jnp.cos(ar), jnp.cos(ar), jnp.cos(ac), jnp.cos(ac)], 1), (1, reps))
    sin_a = jnp.tile(jnp.concatenate([-jnp.sin(ar), zeros, -jnp.sin(ac), zeros], 1), (1, reps))
    sin_b = jnp.tile(jnp.concatenate([zeros, jnp.sin(ar), zeros, jnp.sin(ac)], 1), (1, reps))
    return cos, sin_a, sin_b


def kernel(x, c, ctx, c_ctx, w_ada, b_ada, norm1_g, w_in, attn_sink, w_pool, pool_scale,
           w_up_attn, w_up_pool, w_out, norm2_g, w_router, router_bias,
           w_exp_gate, w_exp_up, w_exp_down, w_sh_gate, w_sh_up, w_sh_down, final_g):
    b, s, d = x.shape
    assert w_ada.shape[0] == 1, "single-layer block"
    assert s % ATTN_BLOCK == 0 and s % GRID_W == 0 and d % LANES == 0
    assert b * s + EXPERT_BLOCK_ROWS <= BF16_EXACT_INT ** 2, "per-expert counts must split into two bf16 digits"
    pool_dim = w_up_pool.shape[1]
    n_exp = w_router.shape[-1]
    t_all = b * s

    pad_rows = -(-(b + 1) // SUBLANES) * SUBLANES
    c_rows = jnp.concatenate([c, c_ctx[None, :], jnp.zeros((pad_rows - b - 1, d), F32)], axis=0)
    mod3 = _ada(c_rows, w_ada[0], b_ada[0]).reshape(pad_rows, N_ADA, d)

    w_in_bf = w_in[0].astype(BF16)
    g1 = norm1_g[0].reshape(1, d)
    cos, sin_a, sin_b = _rope_tables(s)
    q, k, v, u, gates = _inproj(x, mod3, g1, w_in_bf, cos, sin_a, sin_b, pool_dim)
    kc, vc = _ctxkv(ctx, mod3[b], g1, w_in_bf[:, ATTN_DIM:ATTN_DIM + 2 * KV_DIM])
    attn = _attention(q, k, v, kc, vc, attn_sink[0])

    w_r_t = w_router[0].T
    w_r_hi = w_r_t.astype(BF16)
    w_r_lo = (w_r_t - w_r_hi.astype(F32)).astype(BF16)
    base, h2, eidx, wts, rank, cnt = _mixer(
        x, attn, u, gates, mod3, norm2_g[0].reshape(1, d), w_pool[0].astype(BF16),
        pool_scale[0].reshape(1, pool_dim), w_up_attn[0].astype(BF16), w_up_pool[0].astype(BF16),
        w_out[0].astype(BF16), w_r_hi, w_r_lo, router_bias[0].reshape(n_exp, 1),
        w_sh_gate[0].astype(BF16), w_sh_up[0].astype(BF16), w_sh_down[0].astype(BF16))

    br = EXPERT_BLOCK_ROWS
    n_blocks = (t_all * TOP_K + n_exp * (br - 1) + br - 1) // br
    n_slots = n_blocks * br
    pos, bstart = _slots(eidx, rank, cnt, br)
    xs = _dispatch(h2, pos, n_slots)
    y = _experts(xs, bstart[:n_exp + 1, 0], w_exp_gate[0], w_exp_up[0], w_exp_down[0], br)
    yg = _gather_back(y, pos)

    return _combine(yg, wts, base, mod3, final_g.reshape(1, d))
```
